```python
import jax
import jax.numpy as jnp
from jax import lax
import numpy as np

D_MODEL = 2048
BATCH = 1
SEQ = 16384
DEPTH = 1
DEC_BATCH = 8
DEC_SEQ = 32
PAST_LEN = 4096

CHUNK = 64
D_CONV = 2048
CONV_W = 3
N_HEADS = 16
QK_NOPE = 128
QK_ROPE = 64
V_DIM = 128
KV_RANK = 512
ROPE_THETA = 10000.0
Q_BLOCK = 128
PEER_HEADS = 8
PEER_NKEYS = 128
PEER_EXPERTS = PEER_NKEYS * PEER_NKEYS
PEER_DKEY = 256
PEER_HALF = PEER_DKEY // 2
PEER_TOPK = 16
PEER_BLOCK = 128
PLE_DIM = 256
RMS_EPS = 1e-6
IN_SIZES = [D_CONV, D_CONV, D_CONV, N_HEADS * (QK_NOPE + QK_ROPE), KV_RANK, QK_ROPE, D_MODEL, D_MODEL]
IN_SPLITS = [int(s) for s in np.cumsum(IN_SIZES)[:-1]]
D_IN = sum(IN_SIZES)

kernel_name = 'hybrid_conv_mla_peer_stream_step'


def _rmsnorm(x, g):
    xf = x.astype(jnp.float32)
    y = xf * lax.rsqrt(jnp.mean(xf * xf, axis=-1, keepdims=True) + RMS_EPS)
    return y.astype(x.dtype) * g


def _rope_tables(pos):
    inv = 1.0 / (ROPE_THETA ** (jnp.arange(0, QK_ROPE, 2, dtype=jnp.float32) / QK_ROPE))
    ang = pos.astype(jnp.float32)[:, None] * inv[None, :]
    return jnp.cos(ang), jnp.sin(ang)


def _rope(x, cos, sin):
    half = x.shape[-1] // 2
    c = cos.astype(x.dtype)
    s = sin.astype(x.dtype)
    x1, x2 = x[..., :half], x[..., half:]
    return jnp.concatenate([x1 * c - x2 * s, x2 * c + x1 * s], axis=-1)


def _chunk_causal_attention(q_nope, q_rope, k_nope, k_rope, v, q_pos, k_pos):
    b, sq, h, _ = q_nope.shape
    qb = min(Q_BLOCK, sq)
    nb = sq // qb
    scale = (QK_NOPE + QK_ROPE) ** -0.5
    k_chunk = k_pos // CHUNK

    def block(args):
        qn, qr, qp = args
        s = jnp.einsum('bqhd,bkhd->bhqk', qn, k_nope) + jnp.einsum('bqhr,bkr->bhqk', qr, k_rope)
        s = s.astype(jnp.float32) * scale
        mask = k_chunk[None, :] <= (qp // CHUNK)[:, None]
        s = jnp.where(mask[None, None], s, jnp.finfo(jnp.float32).min)
        p = jax.nn.softmax(s, axis=-1).astype(v.dtype)
        return jnp.einsum('bhqk,bkhd->bqhd', p, v)

    qn_b = jnp.moveaxis(q_nope.reshape(b, nb, qb, h, QK_NOPE), 1, 0)
    qr_b = jnp.moveaxis(q_rope.reshape(b, nb, qb, h, QK_ROPE), 1, 0)
    qp_b = q_pos.reshape(nb, qb)
    o = lax.map(block, (qn_b, qr_b, qp_b))
    return jnp.moveaxis(o, 0, 1).reshape(b, sq, h, V_DIM)


def _peer(x, w_pq, sub_keys, u_tab, v_tab):
    t = x.shape[0]
    q = (x @ w_pq).reshape(t, PEER_HEADS, 2, PEER_HALF)
    s = jnp.einsum('thcd,ckd->thck', q, sub_keys).astype(jnp.float32)
    s1, i1 = lax.top_k(s[:, :, 0], PEER_TOPK)
    s2, i2 = lax.top_k(s[:, :, 1], PEER_TOPK)
    cand_s = (s1[..., :, None] + s2[..., None, :]).reshape(t, PEER_HEADS, PEER_TOPK * PEER_TOPK)
    cand_i = (i1[..., :, None] * PEER_NKEYS + i2[..., None, :]).reshape(t, PEER_HEADS, PEER_TOPK * PEER_TOPK)
    top_s, pos = lax.top_k(cand_s, PEER_TOPK)
    idx = jnp.take_along_axis(cand_i, pos, axis=-1)
    g = jax.nn.softmax(top_s, axis=-1).astype(x.dtype)
    blk = min(PEER_BLOCK, t)
    nblk = -(-t // blk)
    pad = nblk * blk - t
    xb = jnp.pad(x, ((0, pad), (0, 0))).reshape(nblk, blk, x.shape[-1])
    ib = jnp.pad(idx, ((0, pad), (0, 0), (0, 0))).reshape(nblk, blk, PEER_HEADS, PEER_TOPK)
    gb = jnp.pad(g, ((0, pad), (0, 0), (0, 0))).reshape(nblk, blk, PEER_HEADS, PEER_TOPK)

    def block(args):
        xt, it, gt = args
        act = jax.nn.gelu(jnp.einsum('td,thkd->thk', xt, u_tab[it]))
        return jnp.einsum('thk,thkd->td', gt * act, v_tab[it])

    out = lax.map(block, (xb, ib, gb)).reshape(nblk * blk, x.shape[-1])
    return out[:t]


def _layer(h, p, conv_state, past_ckv, past_krope, pos0, g_mix, w_in, conv_w, g_kv, w_kv_b,
           w_a_out, w_b_out, w_o, g_ffn, w_pq, sub_keys, u_tab, v_tab, g_ple, w_ple_gate, w_ple):
    b, s, d = h.shape
    a = _rmsnorm(h, g_mix)
    z = a @ w_in
    cb, cc, cx, q, ckv, kr, ga, gbr = jnp.split(z, IN_SPLITS, axis=-1)
    u = cc * cx
    u_pad = jnp.concatenate([conv_state.astype(u.dtype), u], axis=1)
    yconv = conv_w[0] * u_pad[:, 0:s] + conv_w[1] * u_pad[:, 1:s + 1] + conv_w[2] * u_pad[:, 2:s + 2]
    y_a = (cb * yconv) @ w_a_out
    new_conv = u_pad[:, -(CONV_W - 1):]
    pos = pos0 + jnp.arange(s)
    cos, sin = _rope_tables(pos)
    q = q.reshape(b, s, N_HEADS, QK_NOPE + QK_ROPE)
    q_nope = q[..., :QK_NOPE]
    q_rope = _rope(q[..., QK_NOPE:], cos[:, None, :], sin[:, None, :])
    c_new = _rmsnorm(ckv, g_kv)
    kr_new = _rope(kr, cos, sin)
    if past_ckv is None:
        c_all, kr_all, k_pos = c_new, kr_new, pos
    else:
        past = past_ckv.shape[1]
        c_all = jnp.concatenate([past_ckv.astype(c_new.dtype), c_new], axis=1)
        kr_all = jnp.concatenate([past_krope.astype(kr_new.dtype), kr_new], axis=1)
        k_pos = jnp.arange(past + s)
    sk = c_all.shape[1]
    kv = (c_all @ w_kv_b).reshape(b, sk, N_HEADS, QK_NOPE + V_DIM)
    k_nope, v = kv[..., :QK_NOPE], kv[..., QK_NOPE:]
    o = _chunk_causal_attention(q_nope, q_rope, k_nope, kr_all, v, pos, k_pos)
    y_b = o.reshape(b, s, N_HEADS * V_DIM) @ w_b_out
    m = jax.nn.sigmoid(ga) * y_a + jax.nn.sigmoid(gbr) * y_b
    h = h + m @ w_o
    hn = _rmsnorm(h, g_ffn)
    h = h + _peer(hn.reshape(b * s, d), w_pq, sub_keys, u_tab, v_tab).reshape(b, s, d)
    h = h + (p @ w_ple) * jax.nn.sigmoid(_rmsnorm(h, g_ple) @ w_ple_gate)
    return h, new_conv, c_new, kr_new


def setup_inputs(seed: int = 0) -> dict:
    key = jax.random.key(seed)
    ks = jax.random.split(key, 26)

    def nrm(k, shape, scale):
        return jax.random.normal(k, shape, jnp.float32) * scale

    def gain(k, shape):
        return 1.0 + 0.01 * jax.random.normal(k, shape, jnp.float32)

    return {
        'x_prompt': nrm(ks[0], (BATCH, SEQ, D_MODEL), 1.0),
        'x_sample': nrm(ks[1], (DEC_BATCH, DEC_SEQ, D_MODEL), 1.0),
        'cache_conv': nrm(ks[2], (DEPTH, DEC_BATCH, CONV_W - 1, D_CONV), 0.5),
        'cache_ckv': nrm(ks[3], (DEPTH, DEC_BATCH, PAST_LEN, KV_RANK), 1.0),
        'cache_krope': nrm(ks[4], (DEPTH, DEC_BATCH, PAST_LEN, QK_ROPE), 1.0),
        'p_prompt': nrm(ks[5], (DEPTH, BATCH, SEQ, PLE_DIM), 1.0),
        'p_sample': nrm(ks[6], (DEPTH, DEC_BATCH, DEC_SEQ, PLE_DIM), 1.0),
        'g_mix': gain(ks[7], (DEPTH, D_MODEL)),
        'w_in': nrm(ks[8], (DEPTH, D_MODEL, D_IN), D_MODEL ** -0.5),
        'conv_w': nrm(ks[9], (DEPTH, CONV_W, D_CONV), 0.5),
        'g_kv': gain(ks[10], (DEPTH, KV_RANK)),
        'w_kv_b': nrm(ks[11], (DEPTH, KV_RANK, N_HEADS * (QK_NOPE + V_DIM)), KV_RANK ** -0.5),
        'w_a_out': nrm(ks[12], (DEPTH, D_CONV, D_MODEL), D_CONV ** -0.5),
        'w_b_out': nrm(ks[13], (DEPTH, N_HEADS * V_DIM, D_MODEL), (N_HEADS * V_DIM) ** -0.5),
        'w_o': nrm(ks[14], (DEPTH, D_MODEL, D_MODEL), D_MODEL ** -0.5),
        'g_ffn': gain(ks[15], (DEPTH, D_MODEL)),
        'w_pq': nrm(ks[16], (DEPTH, D_MODEL, PEER_HEADS * PEER_DKEY), D_MODEL ** -0.5),
        'sub_keys': nrm(ks[17], (DEPTH, 2, PEER_NKEYS, PEER_HALF), PEER_HALF ** -0.5),
        'u_tab': nrm(ks[18], (DEPTH, PEER_EXPERTS, D_MODEL), D_MODEL ** -0.5),
        'v_tab': nrm(ks[19], (DEPTH, PEER_EXPERTS, D_MODEL), PEER_HEADS ** -0.5),
        'g_ple': gain(ks[20], (DEPTH, D_MODEL)),
        'w_ple_gate': nrm(ks[21], (DEPTH, D_MODEL, D_MODEL), D_MODEL ** -0.5),
        'w_ple': nrm(ks[22], (DEPTH, PLE_DIM, D_MODEL), PLE_DIM ** -0.5),
        'g_final': gain(ks[23], (D_MODEL,)),
    }


def reference(x_prompt, x_sample, cache_conv, cache_ckv, cache_krope, p_prompt, p_sample,
              g_mix, w_in, conv_w, g_kv, w_kv_b, w_a_out, w_b_out, w_o, g_ffn, w_pq, sub_keys,
              u_tab, v_tab, g_ple, w_ple_gate, w_ple, g_final):
    def run(x, p, conv0, ckv0, kr0, pos0):
        h = x
        convs, ckvs, krs = [], [], []
        for i in range(DEPTH):
            h, cs, ck, kr = _layer(
                h, p[i], conv0[i],
                None if ckv0 is None else ckv0[i],
                None if kr0 is None else kr0[i],
                pos0, g_mix[i], w_in[i], conv_w[i], g_kv[i], w_kv_b[i], w_a_out[i], w_b_out[i],
                w_o[i], g_ffn[i], w_pq[i], sub_keys[i], u_tab[i], v_tab[i], g_ple[i],
                w_ple_gate[i], w_ple[i])
            convs.append(cs)
            ckvs.append(ck)
            krs.append(kr)
        return _rmsnorm(h, g_final), jnp.stack(convs), jnp.stack(ckvs), jnp.stack(krs)

    conv_zero = jnp.zeros((DEPTH, x_prompt.shape[0], CONV_W - 1, D_CONV), x_prompt.dtype)
    y_prompt, conv_p, ckv_p, kr_p = run(x_prompt, p_prompt, conv_zero, None, None, 0)
    past_len = cache_ckv.shape[2]
    y_sample, conv_s, ckv_s, kr_s = run(x_sample, p_sample, cache_conv, cache_ckv, cache_krope, past_len)
    return (y_prompt, y_sample, conv_p, ckv_p, kr_p, conv_s, ckv_s, kr_s)
```

```python
import functools
import math

import numpy as np
import jax
import jax.numpy as jnp
from jax import lax
from jax.experimental import pallas as pl
from jax.experimental.pallas import tpu as pltpu

D_MODEL = 2048
D_CONV = 2048
N_HEADS = 16
QK_NOPE = 128
QK_ROPE = 64
QK_DIM = QK_NOPE + QK_ROPE
V_DIM = 128
KV_RANK = 512
CHUNK = 64
ROPE_THETA = 10000.0
PEER_HEADS = 8
PEER_NKEYS = 128
PEER_TOPK = 16
PEER_DKEY = 256
PEER_EXPERTS = PEER_NKEYS * PEER_NKEYS
RMS_EPS = 1e-6
IN_SIZES = (D_CONV, D_CONV, D_CONV, N_HEADS * QK_DIM, KV_RANK, QK_ROPE, D_MODEL, D_MODEL)

BF16 = jnp.bfloat16
F32 = jnp.float32
VMEM_LIMIT_BYTES = 56 * 1024 * 1024
MASK_VALUE = -1e30
NT_DIMS = (((1,), (1,)), ((), ()))
TN_DIMS = (((0,), (0,)), ((), ()))

_PAIRS = tuple((p, q) for p in range(PEER_TOPK) for q in range(PEER_TOPK)
               if (p + 1) * (q + 1) <= PEER_TOPK)


def _pick(n, prefs):
    for p in prefs:
        if n % p == 0:
            return p
    return n


def _params(*sem):
    return pltpu.CompilerParams(dimension_semantics=sem, vmem_limit_bytes=VMEM_LIMIT_BYTES)


def _rms_kernel(x_ref, g_ref, o_ref):
    x = x_ref[...]
    ms = jnp.mean(x * x, axis=-1, keepdims=True)
    o_ref[...] = (x * lax.rsqrt(ms + RMS_EPS) * g_ref[...]).astype(o_ref.dtype)


def _rmsnorm(x, g, out_dtype):
    t, d = x.shape
    tm = _pick(t, (640, 512, 256, 128))
    return pl.pallas_call(
        _rms_kernel,
        grid=(t // tm,),
        in_specs=[pl.BlockSpec((tm, d), lambda i: (i, 0)),
                  pl.BlockSpec((1, d), lambda i: (0, 0))],
        out_specs=pl.BlockSpec((tm, d), lambda i: (i, 0)),
        out_shape=jax.ShapeDtypeStruct((t, d), out_dtype),
        compiler_params=_params("parallel"),
        name="rmsnorm",
    )(x, g.reshape(1, d))


def _add_rms_kernel(a_ref, b_ref, g_ref, s_ref, o_ref):
    x = a_ref[...] + b_ref[...]
    s_ref[...] = x
    ms = jnp.mean(x * x, axis=-1, keepdims=True)
    o_ref[...] = (x * lax.rsqrt(ms + RMS_EPS) * g_ref[...]).astype(o_ref.dtype)


def _add_rmsnorm(a, b, g, out_dtype):
    t, d = a.shape
    tm = _pick(t, (640, 512, 256, 128))
    row = pl.BlockSpec((tm, d), lambda i: (i, 0))
    return pl.pallas_call(
        _add_rms_kernel,
        grid=(t // tm,),
        in_specs=[row, row, pl.BlockSpec((1, d), lambda i: (0, 0))],
        out_specs=[row, row],
        out_shape=[jax.ShapeDtypeStruct((t, d), F32), jax.ShapeDtypeStruct((t, d), out_dtype)],
        compiler_params=_params("parallel"),
        name="add_rmsnorm",
    )(a, b, g.reshape(1, d))


def _mm_kernel(x_ref, w_ref, o_ref):
    x = x_ref[...].astype(BF16)
    o_ref[...] = jnp.dot(x, w_ref[...], preferred_element_type=F32).astype(o_ref.dtype)


def _mm_res_kernel(x_ref, w_ref, r_ref, o_ref):
    x = x_ref[...].astype(BF16)
    o_ref[...] = (r_ref[...] + jnp.dot(x, w_ref[...], preferred_element_type=F32)).astype(o_ref.dtype)


def _matmul(x, w, out_dtype, residual=None, tm_prefs=(1280, 1024, 768, 512, 256, 128),
            tn_prefs=(1024, 512, 256, 128), name="matmul"):
    t, k = x.shape
    n = w.shape[1]
    tm = _pick(t, tm_prefs)
    tn = _pick(n, tn_prefs)
    in_specs = [pl.BlockSpec((tm, k), lambda i, j: (i, 0)),
                pl.BlockSpec((k, tn), lambda i, j: (0, j))]
    args = [x, w]
    kern = _mm_kernel
    if residual is not None:
        in_specs.append(pl.BlockSpec((tm, tn), lambda i, j: (i, j)))
        args.append(residual)
        kern = _mm_res_kernel
    return pl.pallas_call(
        kern,
        grid=(t // tm, n // tn),
        in_specs=in_specs,
        out_specs=pl.BlockSpec((tm, tn), lambda i, j: (i, j)),
        out_shape=jax.ShapeDtypeStruct((t, n), out_dtype),
        compiler_params=_params("parallel", "arbitrary"),
        name=name,
    )(*args)


def _kv_proj(c, w_kv):
    r, k = c.shape
    hw = QK_NOPE + V_DIM
    tm = _pick(r, (2064, 2048, 1024, 768, 512, 256, 128, 32))
    return pl.pallas_call(
        _mm_kernel,
        grid=(r // tm, N_HEADS),
        in_specs=[pl.BlockSpec((tm, k), lambda i, j: (i, 0)),
                  pl.BlockSpec((k, hw), lambda i, j: (0, j))],
        out_specs=pl.BlockSpec((None, tm, hw), lambda i, j: (j, i, 0)),
        out_shape=jax.ShapeDtypeStruct((N_HEADS, r, hw), BF16),
        compiler_params=_params("parallel", "arbitrary"),
        name="kv_proj",
    )(c, w_kv)


def _post_small_kernel(z_ref, g_ref, cos_ref, sin_ref, c_ref, kr_ref):
    z = z_ref[...]
    ckv = z[:, :KV_RANK]
    ms = jnp.mean(ckv * ckv, axis=-1, keepdims=True)
    c_ref[...] = ckv * lax.rsqrt(ms + RMS_EPS) * g_ref[...]
    kr = z[:, KV_RANK:KV_RANK + QK_ROPE]
    kr_sw = z[:, KV_RANK + QK_ROPE:KV_RANK + 2 * QK_ROPE]
    kr_ref[...] = kr * cos_ref[...] + kr_sw * sin_ref[...]


def _post_small(z_small, g_kv, cos64, sin64):
    t, w = z_small.shape
    tm = _pick(t, (640, 512, 256, 128))
    return pl.pallas_call(
        _post_small_kernel,
        grid=(t // tm,),
        in_specs=[pl.BlockSpec((tm, w), lambda i: (i, 0)),
                  pl.BlockSpec((1, KV_RANK), lambda i: (0, 0)),
                  pl.BlockSpec((tm, QK_ROPE), lambda i: (i, 0)),
                  pl.BlockSpec((tm, QK_ROPE), lambda i: (i, 0))],
        out_specs=[pl.BlockSpec((tm, KV_RANK), lambda i: (i, 0)),
                   pl.BlockSpec((tm, QK_ROPE), lambda i: (i, 0))],
        out_shape=[jax.ShapeDtypeStruct((t, KV_RANK), F32),
                   jax.ShapeDtypeStruct((t, QK_ROPE), F32)],
        compiler_params=_params("parallel"),
        name="latent_post",
    )(z_small, g_kv.reshape(1, KV_RANK), cos64, sin64)


def _q_prep_kernel(qn_ref, qr_ref, qsw_ref, cos_ref, sin_ref, o_ref):
    scale = QK_DIM ** -0.5
    cos = jnp.tile(cos_ref[...], (1, N_HEADS))
    sin = jnp.tile(sin_ref[...], (1, N_HEADS))
    qr = (qr_ref[...].astype(F32) * cos + qsw_ref[...].astype(F32) * sin) * scale
    qn = qn_ref[...].astype(F32) * scale
    for h in range(N_HEADS):
        o_ref[h, :, 0:QK_NOPE] = qn[:, h * QK_NOPE:(h + 1) * QK_NOPE].astype(o_ref.dtype)
        o_ref[h, :, QK_NOPE:QK_DIM] = qr[:, h * QK_ROPE:(h + 1) * QK_ROPE].astype(o_ref.dtype)


def _q_prep(z_big, cos64, sin64, row0, rows):
    tm = _pick(math.gcd(rows, row0) if row0 else rows, (256, 128, 32))
    b0 = row0 // tm
    nope_blk = 3 * D_CONV // (N_HEADS * QK_NOPE)
    rope_blk = (3 * D_CONV + N_HEADS * QK_NOPE) // (N_HEADS * QK_ROPE)
    return pl.pallas_call(
        _q_prep_kernel,
        grid=(rows // tm,),
        in_specs=[pl.BlockSpec((tm, N_HEADS * QK_NOPE), lambda i: (b0 + i, nope_blk)),
                  pl.BlockSpec((tm, N_HEADS * QK_ROPE), lambda i: (b0 + i, rope_blk)),
                  pl.BlockSpec((tm, N_HEADS * QK_ROPE), lambda i: (b0 + i, rope_blk + 1)),
                  pl.BlockSpec((tm, QK_ROPE), lambda i: (b0 + i, 0)),
                  pl.BlockSpec((tm, QK_ROPE), lambda i: (b0 + i, 0))],
        out_specs=pl.BlockSpec((N_HEADS, tm, QK_DIM), lambda i: (0, i, 0)),
        out_shape=jax.ShapeDtypeStruct((N_HEADS, rows, QK_DIM), BF16),
        compiler_params=_params("parallel"),
        name="q_prep",
    )(z_big, z_big, z_big, cos64, sin64)


def _conv_kernel(state_ref, pc_ref, px_ref, b_ref, c_ref, x_ref, w_ref, o_ref, last_ref, *, tm, halo):
    i = pl.program_id(1)
    u = c_ref[...].astype(F32) * x_ref[...].astype(F32)
    prev = pc_ref[...].astype(F32) * px_ref[...].astype(F32)
    st = state_ref[...]
    first = i == 0
    um1 = jnp.where(first, st[7:8, :], prev[halo - 1:halo, :])
    um2 = jnp.where(first, st[6:7, :], prev[halo - 2:halo - 1, :])
    row = lax.broadcasted_iota(jnp.int32, u.shape, 0)
    s1 = jnp.where(row == 0, um1, pltpu.roll(u, 1, 0))
    s2 = jnp.where(row == 0, um2, jnp.where(row == 1, um1, pltpu.roll(u, 2, 0)))
    w = w_ref[...]
    y = w[0:1, :] * s2 + w[1:2, :] * s1 + w[2:3, :] * u
    o_ref[...] = (b_ref[...].astype(F32) * y).astype(o_ref.dtype)
    last_ref[...] = u[tm - 8:tm, :]


def _conv_gate(z_big, state, conv_w, row0, nseq, seq_len):
    tm = _pick(seq_len, (512, 256, 128, 32))
    halo = 16
    nb = seq_len // tm
    b0 = row0 // tm
    h0 = row0 // halo
    per = tm // halo
    d = D_CONV

    def hmap(col):
        return lambda s, i: (jnp.maximum(h0 + (s * nb + i) * per - 1, 0), col)

    def bmap(col):
        return lambda s, i: (b0 + s * nb + i, col)

    return pl.pallas_call(
        functools.partial(_conv_kernel, tm=tm, halo=halo),
        grid=(nseq, nb),
        in_specs=[pl.BlockSpec((None, 8, d), lambda s, i: (s, 0, 0)),
                  pl.BlockSpec((halo, d), hmap(1)),
                  pl.BlockSpec((halo, d), hmap(2)),
                  pl.BlockSpec((tm, d), bmap(0)),
                  pl.BlockSpec((tm, d), bmap(1)),
                  pl.BlockSpec((tm, d), bmap(2)),
                  pl.BlockSpec((8, d), lambda s, i: (0, 0))],
        out_specs=[pl.BlockSpec((tm, d), lambda s, i: (s * nb + i, 0)),
                   pl.BlockSpec((None, 8, d), lambda s, i: (s * nb + i, 0, 0))],
        out_shape=[jax.ShapeDtypeStruct((nseq * seq_len, d), BF16),
                   jax.ShapeDtypeStruct((nseq * nb, 8, d), F32)],
        compiler_params=_params("parallel", "arbitrary"),
        name="conv_gate",
    )(state, z_big, z_big, z_big, z_big, z_big, jnp.pad(conv_w, ((0, 5), (0, 0))))


def _attn_kernel(q_ref, k_ref, v_ref, kr_ref, o_ref, *, tq, tk, sk, q_pos0, q_block_stride):
    i = pl.program_id(1)
    q = q_ref[...]
    base = q_pos0 + i * q_block_stride
    chunk_lo = base // CHUNK
    chunk_hi = (base + tq - 1) // CHUNK
    k_end = jnp.minimum((chunk_hi + 1) * CHUNK, sk)
    n_kv = (k_end + tk - 1) // tk
    n_full = jnp.minimum(((chunk_lo + 1) * CHUNK) // tk, n_kv)
    q_pos = base + lax.broadcasted_iota(jnp.int32, (tq, 1), 0)
    row_end = (q_pos // CHUNK + 1) * CHUNK

    def step(j, carry, masked):
        m, l, acc = carry
        ks = pl.ds(pl.multiple_of(j * tk, tk), tk)
        k = jnp.concatenate([k_ref[ks, :], kr_ref[ks, :]], axis=1)
        s = lax.dot_general(q, k, NT_DIMS, preferred_element_type=F32)
        if masked:
            k_pos = j * tk + lax.broadcasted_iota(jnp.int32, (tq, tk), 1)
            s = jnp.where(k_pos < row_end, s, MASK_VALUE)
        m_new = jnp.maximum(m, jnp.max(s, axis=1, keepdims=True))
        p = jnp.exp(s - m_new)
        alpha = jnp.exp(m - m_new)
        l = alpha * l + jnp.sum(p, axis=1, keepdims=True)
        acc = alpha * acc + jnp.dot(p.astype(BF16), v_ref[ks, :], preferred_element_type=F32)
        return m_new, l, acc

    init = (jnp.full((tq, 1), MASK_VALUE, F32), jnp.zeros((tq, 1), F32), jnp.zeros((tq, V_DIM), F32))
    carry = lax.fori_loop(0, n_full, functools.partial(step, masked=False), init)
    m, l, acc = lax.fori_loop(n_full, n_kv, functools.partial(step, masked=True), carry)
    o_ref[...] = (acc / l).astype(o_ref.dtype)


def _attention(q, kv, kr, nseq, sq, sk, q_pos0, per_block_positions):
    if per_block_positions:
        tq = _pick(sq, (256, 128, 32))
        stride = tq
    else:
        tq = sq
        stride = 0
    nqb = sq // tq
    tk = _pick(sk, (512, 256, 128))
    return pl.pallas_call(
        functools.partial(_attn_kernel, tq=tq, tk=tk, sk=sk, q_pos0=q_pos0, q_block_stride=stride),
        grid=(N_HEADS, nseq * nqb),
        in_specs=[pl.BlockSpec((None, tq, QK_DIM), lambda h, i: (h, i, 0)),
                  pl.BlockSpec((None, sk, QK_NOPE), lambda h, i: (h, i // nqb, 0)),
                  pl.BlockSpec((None, sk, V_DIM), lambda h, i: (h, i // nqb, 1)),
                  pl.BlockSpec((sk, QK_ROPE), lambda h, i: (i // nqb, 0))],
        out_specs=pl.BlockSpec((tq, V_DIM), lambda h, i: (i, h)),
        out_shape=jax.ShapeDtypeStruct((nseq * sq, N_HEADS * V_DIM), BF16),
        compiler_params=_params("parallel", "arbitrary"),
        name="attention",
    )(q, kv, kv, kr)


def _merge_kernel(a_ref, o_ref, wa_ref, wb_ref, ga_ref, gb_ref, m_ref):
    ya = jnp.dot(a_ref[...], wa_ref[...], preferred_element_type=F32)
    yb = jnp.dot(o_ref[...], wb_ref[...], preferred_element_type=F32)
    m = jax.nn.sigmoid(ga_ref[...].astype(F32)) * ya + jax.nn.sigmoid(gb_ref[...].astype(F32)) * yb
    m_ref[...] = m.astype(m_ref.dtype)


def _merge(ya_in, o, wa, wb, z_big):
    t, k = ya_in.shape
    n = wa.shape[1]
    tm = _pick(t, (1280, 1024, 512, 256, 128))
    tn = _pick(n, (512, 256, 128))
    ga_blk = (3 * D_CONV + N_HEADS * (QK_NOPE + 2 * QK_ROPE)) // tn
    gb_blk = ga_blk + D_MODEL // tn
    return pl.pallas_call(
        _merge_kernel,
        grid=(t // tm, n // tn),
        in_specs=[pl.BlockSpec((tm, k), lambda i, j: (i, 0)),
                  pl.BlockSpec((tm, k), lambda i, j: (i, 0)),
                  pl.BlockSpec((k, tn), lambda i, j: (0, j)),
                  pl.BlockSpec((k, tn), lambda i, j: (0, j)),
                  pl.BlockSpec((tm, tn), lambda i, j: (i, ga_blk + j)),
                  pl.BlockSpec((tm, tn), lambda i, j: (i, gb_blk + j))],
        out_specs=pl.BlockSpec((tm, tn), lambda i, j: (i, j)),
        out_shape=jax.ShapeDtypeStruct((t, n), BF16),
        compiler_params=_params("parallel", "arbitrary"),
        name="merge",
    )(ya_in, o, wa, wb, z_big, z_big)


def _extract_top(work_ref, rank_ref, val_ref, idx_ref, *, track_rank):
    nk = PEER_NKEYS
    tm = work_ref.shape[1]
    neg = jnp.full((8, tm), -jnp.inf, F32)

    def round_(p, carry):
        idx_prev, pf = carry
        parts = [neg, neg, neg, neg]
        for k in range(nk):
            rows = slice(8 * k, 8 * k + 8)
            hit = idx_prev == float(k)
            w = jnp.where(hit, -jnp.inf, work_ref[rows, :])
            work_ref[rows, :] = w
            if track_rank:
                rank_ref[rows, :] = jnp.where(hit, pf - 1.0, rank_ref[rows, :])
            parts[k % 4] = jnp.maximum(parts[k % 4], w)
        m = jnp.maximum(jnp.maximum(parts[0], parts[1]), jnp.maximum(parts[2], parts[3]))
        big = jnp.full((8, tm), float(nk), F32)
        iparts = [big, big, big, big]
        for k in range(nk):
            w = work_ref[8 * k:8 * k + 8, :]
            iparts[k % 4] = jnp.minimum(iparts[k % 4], jnp.where(w == m, float(k), float(nk)))
        idx = jnp.minimum(jnp.minimum(iparts[0], iparts[1]), jnp.minimum(iparts[2], iparts[3]))
        val_ref[p] = m
        idx_ref[p] = idx
        return idx, pf + 1.0

    idx_last, _ = lax.fori_loop(0, PEER_TOPK, round_,
                                (jnp.full((8, tm), -1.0, F32), jnp.zeros((8, tm), F32)))
    if track_rank:
        for k in range(nk):
            rows = slice(8 * k, 8 * k + 8)
            rank_ref[rows, :] = jnp.where(idx_last == float(k), float(PEER_TOPK - 1), rank_ref[rows, :])


def _peer_select_kernel(qp_ref, k1_ref, k2_ref, k2h_ref, perm_ref,
                        r2_ref, e2_ref, n1_ref, w1_ref,
                        work_ref, rank_ref, a_ref, ia_ref, b_ref, ib_ref):
    nk = PEER_NKEYS
    qp = qp_ref[...]
    tm = qp.shape[0]
    s2h = lax.dot_general(k2h_ref[...], qp, NT_DIMS, preferred_element_type=F32)
    for h in range(PEER_HEADS):
        blk = s2h[h * nk:(h + 1) * nk, :]
        e2_ref[h * nk:(h + 1) * nk, :] = jnp.exp(blk - jnp.max(blk, axis=0, keepdims=True))

    work_ref[...] = lax.dot_general(k1_ref[...], qp, NT_DIMS, preferred_element_type=F32)
    _extract_top(work_ref, rank_ref, a_ref, ia_ref, track_rank=False)
    work_ref[...] = lax.dot_general(k2_ref[...], qp, NT_DIMS, preferred_element_type=F32)
    rank_ref[...] = jnp.full(rank_ref.shape, float(PEER_TOPK), F32)
    _extract_top(work_ref, rank_ref, b_ref, ib_ref, track_rank=True)
    r2_ref[...] = jnp.dot(perm_ref[...], rank_ref[...].astype(BF16), preferred_element_type=F32)

    a = [a_ref[p] for p in range(PEER_TOPK)]
    b = [b_ref[q] for q in range(PEER_TOPK)]
    cand = [a[p] + b[q] for (p, q) in _PAIRS]
    npair = len(_PAIRS)
    beaten = [jnp.zeros((8, tm), F32) for _ in range(npair)]
    for x in range(npair):
        px, qx = _PAIRS[x]
        for y in range(x + 1, npair):
            py, qy = _PAIRS[y]
            if px <= py and qx <= qy:
                beaten[y] = beaten[y] + 1.0
            else:
                gt = jnp.where(cand[y] > cand[x], 1.0, 0.0)
                beaten[x] = beaten[x] + gt
                beaten[y] = beaten[y] + (1.0 - gt)
    sel = [jnp.where(bt < float(PEER_TOPK), 1.0, 0.0) for bt in beaten]
    ea = [jnp.exp(a[p] - a[0]) for p in range(PEER_TOPK)]
    eb = [jnp.exp(b[q] - b[0]) for q in range(PEER_TOPK)]
    z = jnp.zeros((8, tm), F32)
    cnt = [jnp.zeros((8, tm), F32) for _ in range(PEER_TOPK)]
    for x, (p, q) in enumerate(_PAIRS):
        z = z + sel[x] * (ea[p] * eb[q])
        cnt[p] = cnt[p] + sel[x]
    inv_z = 1.0 / z
    w1 = [ea[p] * inv_z for p in range(PEER_TOPK)]
    ia = [ia_ref[p] for p in range(PEER_TOPK)]
    zero = jnp.zeros((8, tm), F32)
    for k in range(nk):
        n1k = zero
        w1k = zero
        for p in range(PEER_TOPK):
            hit = ia[p] == float(k)
            n1k = jnp.where(hit, cnt[p], n1k)
            w1k = jnp.where(hit, w1[p], w1k)
        n1_ref[8 * k:8 * k + 8, :] = n1k
        w1_ref[8 * k:8 * k + 8, :] = w1k


def _peer_select(qp, k1, k2, k2h, perm):
    t, d = qp.shape
    tm = _pick(t, (256, 128))
    rows = PEER_NKEYS * PEER_HEADS
    full = lambda i: (0, 0)
    out = jax.ShapeDtypeStruct((rows, t), F32)
    ospec = pl.BlockSpec((rows, tm), lambda i: (0, i))
    return pl.pallas_call(
        _peer_select_kernel,
        grid=(t // tm,),
        in_specs=[pl.BlockSpec((tm, d), lambda i: (i, 0)),
                  pl.BlockSpec((rows, d), full),
                  pl.BlockSpec((rows, d), full),
                  pl.BlockSpec((rows, d), full),
                  pl.BlockSpec((rows, rows), full)],
        out_specs=[ospec, ospec, ospec, ospec],
        out_shape=[out, out, out, out],
        scratch_shapes=[pltpu.VMEM((rows, tm), F32), pltpu.VMEM((rows, tm), F32),
                        pltpu.VMEM((PEER_TOPK, 8, tm), F32), pltpu.VMEM((PEER_TOPK, 8, tm), F32),
                        pltpu.VMEM((PEER_TOPK, 8, tm), F32), pltpu.VMEM((PEER_TOPK, 8, tm), F32)],
        compiler_params=_params("parallel"),
        name="peer_select",
    )(qp, k1, k2, k2h, perm)


def _gelu_tanh(x):
    c = math.sqrt(2.0 / math.pi)
    return 0.5 * x * (1.0 + jnp.tanh(c * (x + 0.044715 * (x * x * x))))


def _peer_main_kernel(hn_ref, u_ref, v_ref, r2_ref, e2_ref, n1_ref, w1_ref, o_ref,
                      act_ref, pt_ref, *, tc):
    nk = PEER_NKEYS
    te, tm = act_ref.shape
    ei = pl.program_id(1)

    @pl.when(ei == 0)
    def _():
        o_ref[...] = jnp.zeros(o_ref.shape, o_ref.dtype)

    act_ref[...] = _gelu_tanh(lax.dot_general(u_ref[...], hn_ref[...], NT_DIMS, preferred_element_type=F32))
    for il in range(te // nk):
        rows = slice(il * nk, (il + 1) * nk)
        for c in range(tm // tc):
            cols = slice(c * tc, (c + 1) * tc)
            gate = jnp.zeros((nk, tc), F32)
            for h in range(PEER_HEADS):
                hr = slice(h * nk, (h + 1) * nk)
                r = il * PEER_HEADS + h
                keep = r2_ref[hr, cols] < n1_ref[r:r + 1, cols]
                gate = gate + jnp.where(keep, e2_ref[hr, cols], 0.0) * w1_ref[r:r + 1, cols]
            pt_ref[rows, cols] = (gate * act_ref[rows, cols]).astype(pt_ref.dtype)
    o_ref[...] += lax.dot_general(pt_ref[...], v_ref[...], TN_DIMS, preferred_element_type=F32)


def _peer_main(hn, u, v, r2, e2, n1, w1):
    t, d = hn.shape
    e = u.shape[0]
    tm = _pick(t, (640, 512, 256, 128))
    te = 512
    tc = 128
    rows = PEER_NKEYS * PEER_HEADS
    sub = te // PEER_NKEYS * PEER_HEADS
    return pl.pallas_call(
        functools.partial(_peer_main_kernel, tc=tc),
        grid=(t // tm, e // te),
        in_specs=[pl.BlockSpec((tm, d), lambda i, j: (i, 0)),
                  pl.BlockSpec((te, d), lambda i, j: (j, 0)),
                  pl.BlockSpec((te, d), lambda i, j: (j, 0)),
                  pl.BlockSpec((rows, tm), lambda i, j: (0, i)),
                  pl.BlockSpec((rows, tm), lambda i, j: (0, i)),
                  pl.BlockSpec((sub, tm), lambda i, j: (j, i)),
                  pl.BlockSpec((sub, tm), lambda i, j: (j, i))],
        out_specs=pl.BlockSpec((tm, d), lambda i, j: (i, 0)),
        out_shape=jax.ShapeDtypeStruct((t, d), F32),
        scratch_shapes=[pltpu.VMEM((te, tm), F32), pltpu.VMEM((te, tm), BF16)],
        compiler_params=_params("parallel", "arbitrary"),
        name="peer_main",
    )(hn, u, v, r2, e2, n1, w1)


def _ple_kernel(xn_ref, p_ref, wg_ref, wp_ref, h_ref, o_ref):
    gate = jnp.dot(xn_ref[...], wg_ref[...], preferred_element_type=F32)
    pe = jnp.dot(p_ref[...].astype(BF16), wp_ref[...], preferred_element_type=F32)
    o_ref[...] = h_ref[...] + pe * jax.nn.sigmoid(gate)


def _ple(xn, p, wg, wp, h):
    t, d = xn.shape
    pd = p.shape[1]
    tm = _pick(t, (1280, 1024, 512, 256, 128))
    tn = _pick(d, (512, 256, 128))
    return pl.pallas_call(
        _ple_kernel,
        grid=(t // tm, d // tn),
        in_specs=[pl.BlockSpec((tm, d), lambda i, j: (i, 0)),
                  pl.BlockSpec((tm, pd), lambda i, j: (i, 0)),
                  pl.BlockSpec((d, tn), lambda i, j: (0, j)),
                  pl.BlockSpec((pd, tn), lambda i, j: (0, j)),
                  pl.BlockSpec((tm, tn), lambda i, j: (i, j))],
        out_specs=pl.BlockSpec((tm, tn), lambda i, j: (i, j)),
        out_shape=jax.ShapeDtypeStruct((t, d), F32),
        compiler_params=_params("parallel", "arbitrary"),
        name="ple",
    )(xn, p, wg, wp, h)


def _swap_halves(w):
    half = w.shape[-1] // 2
    return jnp.concatenate([w[..., half:], w[..., :half]], axis=-1)


def _split_w_in(w_in):
    d = w_in.shape[0]
    splits = [int(s) for s in np.cumsum(IN_SIZES)[:-1]]
    wb, wc, wx, wq, wckv, wkr, wga, wgb = jnp.split(w_in, splits, axis=1)
    wq = wq.reshape(d, N_HEADS, QK_DIM)
    wq_n = wq[:, :, :QK_NOPE].reshape(d, N_HEADS * QK_NOPE)
    wq_r = wq[:, :, QK_NOPE:]
    w_big = jnp.concatenate(
        [wb, wc, wx, wq_n, wq_r.reshape(d, -1), _swap_halves(wq_r).reshape(d, -1), wga, wgb], axis=1)
    w_small = jnp.concatenate([wckv, wkr, _swap_halves(wkr)], axis=1)
    return w_big.astype(BF16), w_small.astype(BF16)


def _rope_tables(pos):
    inv = 1.0 / (ROPE_THETA ** (jnp.arange(0, QK_ROPE, 2, dtype=F32) / QK_ROPE))
    ang = pos.astype(F32)[:, None] * inv[None, :]
    cos, sin = jnp.cos(ang), jnp.sin(ang)
    return jnp.concatenate([cos, cos], axis=1), jnp.concatenate([-sin, sin], axis=1)


def _key_matrices(sub_keys):
    nk, hd, half = PEER_NKEYS, PEER_HEADS, PEER_DKEY // 2
    eye = jnp.eye(hd, dtype=F32)

    def build(c, head_major):
        sel = jnp.zeros((2,), F32).at[c].set(1.0)
        m = sub_keys[c][:, None, None, None, :] * eye[None, :, :, None, None] * sel[None, None, None, :, None]
        if head_major:
            m = jnp.transpose(m, (1, 0, 2, 3, 4))
        return m.reshape(nk * hd, hd * PEER_DKEY).astype(BF16)

    r = np.arange(nk * hd)
    perm = np.zeros((nk * hd, nk * hd), np.float32)
    perm[(r % hd) * nk + r // hd, r] = 1.0
    return build(0, False), build(1, False), build(1, True), jnp.asarray(perm, BF16)


def kernel(x_prompt, x_sample, cache_conv, cache_ckv, cache_krope, p_prompt, p_sample, g_mix, w_in, conv_w, g_kv, w_kv_b, w_a_out, w_b_out, w_o, g_ffn, w_pq, sub_keys, u_tab, v_tab, g_ple, w_ple_gate, w_ple, g_final):
    assert x_prompt.shape[0] == 1 and w_in.shape[0] == 1
    seq = x_prompt.shape[1]
    nb, dseq = x_sample.shape[0], x_sample.shape[1]
    past = cache_ckv.shape[2]
    d = D_MODEL
    ns = nb * dseq
    t = seq + ns

    x_all = jnp.concatenate([x_prompt.reshape(seq, d), x_sample.reshape(ns, d)], axis=0)
    p_all = jnp.concatenate([p_prompt[0].reshape(seq, -1), p_sample[0].reshape(ns, -1)], axis=0)
    pos = jnp.concatenate([jnp.arange(seq), jnp.tile(past + jnp.arange(dseq), nb)])
    cos64, sin64 = _rope_tables(pos)
    w_big, w_small = _split_w_in(w_in[0])

    xn = _rmsnorm(x_all, g_mix[0], BF16)
    z_big = _matmul(xn, w_big, BF16, name="in_proj")
    z_small = _matmul(xn, w_small, F32, tn_prefs=(w_small.shape[1],), name="in_proj_small")
    c_new, kr_new = _post_small(z_small, g_kv[0], cos64, sin64)

    zero_state = jnp.zeros((1, 8, D_CONV), F32)
    ya_p, last_p = _conv_gate(z_big, zero_state, conv_w[0], 0, 1, seq)
    state_s = jnp.pad(cache_conv[0], ((0, 0), (6, 0), (0, 0)))
    ya_s, last_s = _conv_gate(z_big, state_s, conv_w[0], seq, nb, dseq)
    ya_in = jnp.concatenate([ya_p, ya_s], axis=0)
    conv_p = last_p[-1, 6:8, :].reshape(1, 1, 2, D_CONV)
    if dseq >= 2:
        conv_s = last_s[:, 6:8, :].reshape(1, nb, 2, D_CONV)
    else:
        raise NotImplementedError("sample blocks shorter than the convolution state")

    w_kv = w_kv_b[0].astype(BF16)
    q_p = _q_prep(z_big, cos64, sin64, 0, seq)
    q_s = _q_prep(z_big, cos64, sin64, seq, ns)
    kv_p = _kv_proj(c_new[:seq], w_kv)
    o_p = _attention(q_p, kv_p, kr_new[:seq].astype(BF16), 1, seq, seq, 0, True)
    c_all = jnp.concatenate([cache_ckv[0], c_new[seq:].reshape(nb, dseq, KV_RANK)], axis=1)
    kr_all = jnp.concatenate([cache_krope[0], kr_new[seq:].reshape(nb, dseq, QK_ROPE)], axis=1)
    sk = past + dseq
    kv_s = _kv_proj(c_all.reshape(nb * sk, KV_RANK), w_kv)
    o_s = _attention(q_s, kv_s, kr_all.reshape(nb * sk, QK_ROPE).astype(BF16), nb, dseq, sk, past, False)
    o_all = jnp.concatenate([o_p, o_s], axis=0)

    m = _merge(ya_in, o_all, w_a_out[0].astype(BF16), w_b_out[0].astype(BF16), z_big)
    h1 = _matmul(m, w_o[0].astype(BF16), F32, residual=x_all, name="out_proj")

    hn = _rmsnorm(h1, g_ffn[0], BF16)
    qp = _matmul(hn, w_pq[0].astype(BF16), BF16, name="peer_query")
    k1, k2, k2h, perm = _key_matrices(sub_keys[0])
    r2, e2, n1, w1 = _peer_select(qp, k1, k2, k2h, perm)
    peer = _peer_main(hn, u_tab[0].astype(BF16), v_tab[0].astype(BF16), r2, e2, n1, w1)

    h2, x2n = _add_rmsnorm(h1, peer, g_ple[0], BF16)
    h3 = _ple(x2n, p_all, w_ple_gate[0].astype(BF16), w_ple[0].astype(BF16), h2)
    y = _rmsnorm(h3, g_final, F32)

    y_prompt = y[:seq].reshape(1, seq, d)
    y_sample = y[seq:].reshape(nb, dseq, d)
    ckv_p = c_new[:seq].reshape(1, 1, seq, KV_RANK)
    kr_p = kr_new[:seq].reshape(1, 1, seq, QK_ROPE)
    ckv_s = c_new[seq:].reshape(1, nb, dseq, KV_RANK)
    kr_s = kr_new[seq:].reshape(1, nb, dseq, QK_ROPE)
    return (y_prompt, y_sample, conv_p, ckv_p, kr_p, conv_s, ckv_s, kr_s)
```

```python
import functools
import math

import numpy as np
import jax
import jax.numpy as jnp
from jax import lax
from jax.experimental import pallas as pl
from jax.experimental.pallas import tpu as pltpu

D_MODEL = 2048
D_CONV = 2048
N_HEADS = 16
QK_NOPE = 128
QK_ROPE = 64
QK_DIM = QK_NOPE + QK_ROPE
V_DIM = 128
KV_RANK = 512
CHUNK = 64
ROPE_THETA = 10000.0
PEER_HEADS = 8
PEER_NKEYS = 128
PEER_TOPK = 16
PEER_DKEY = 256
PEER_EXPERTS = PEER_NKEYS * PEER_NKEYS
RMS_EPS = 1e-6
IN_SIZES = (D_CONV, D_CONV, D_CONV, N_HEADS * QK_DIM, KV_RANK, QK_ROPE, D_MODEL, D_MODEL)

BF16 = jnp.bfloat16
F32 = jnp.float32
VMEM_LIMIT_BYTES = 56 * 1024 * 1024
MASK_VALUE = -1e30
NT_DIMS = (((1,), (1,)), ((), ()))
TN_DIMS = (((0,), (0,)), ((), ()))

_PAIRS = tuple((p, q) for p in range(PEER_TOPK) for q in range(PEER_TOPK)
               if (p + 1) * (q + 1) <= PEER_TOPK)


def _pick(n, prefs):
    for p in prefs:
        if n % p == 0:
            return p
    return n


def _params(*sem):
    return pltpu.CompilerParams(dimension_semantics=sem, vmem_limit_bytes=VMEM_LIMIT_BYTES)


def _rms_kernel(x_ref, g_ref, o_ref):
    x = x_ref[...]
    ms = jnp.mean(x * x, axis=-1, keepdims=True)
    o_ref[...] = (x * lax.rsqrt(ms + RMS_EPS) * g_ref[...]).astype(o_ref.dtype)


def _rmsnorm(x, g, out_dtype):
    t, d = x.shape
    tm = _pick(t, (640, 512, 256, 128))
    return pl.pallas_call(
        _rms_kernel,
        grid=(t // tm,),
        in_specs=[pl.BlockSpec((tm, d), lambda i: (i, 0)),
                  pl.BlockSpec((1, d), lambda i: (0, 0))],
        out_specs=pl.BlockSpec((tm, d), lambda i: (i, 0)),
        out_shape=jax.ShapeDtypeStruct((t, d), out_dtype),
        compiler_params=_params("parallel"),
        name="rmsnorm",
    )(x, g.reshape(1, d))


def _add_rms_kernel(a_ref, b_ref, g_ref, s_ref, o_ref):
    x = a_ref[...] + b_ref[...]
    s_ref[...] = x
    ms = jnp.mean(x * x, axis=-1, keepdims=True)
    o_ref[...] = (x * lax.rsqrt(ms + RMS_EPS) * g_ref[...]).astype(o_ref.dtype)


def _add_rmsnorm(a, b, g, out_dtype):
    t, d = a.shape
    tm = _pick(t, (640, 512, 256, 128))
    row = pl.BlockSpec((tm, d), lambda i: (i, 0))
    return pl.pallas_call(
        _add_rms_kernel,
        grid=(t // tm,),
        in_specs=[row, row, pl.BlockSpec((1, d), lambda i: (0, 0))],
        out_specs=[row, row],
        out_shape=[jax.ShapeDtypeStruct((t, d), F32), jax.ShapeDtypeStruct((t, d), out_dtype)],
        compiler_params=_params("parallel"),
        name="add_rmsnorm",
    )(a, b, g.reshape(1, d))


def _mm_kernel(x_ref, w_ref, o_ref):
    x = x_ref[...].astype(BF16)
    o_ref[...] = jnp.dot(x, w_ref[...], preferred_element_type=F32).astype(o_ref.dtype)


def _mm_res_kernel(x_ref, w_ref, r_ref, o_ref):
    x = x_ref[...].astype(BF16)
    o_ref[...] = (r_ref[...] + jnp.dot(x, w_ref[...], preferred_element_type=F32)).astype(o_ref.dtype)


def _matmul(x, w, out_dtype, residual=None, tm_prefs=(1280, 1024, 768, 512, 256, 128),
            tn_prefs=(1024, 512, 256, 128), name="matmul"):
    t, k = x.shape
    n = w.shape[1]
    tm = _pick(t, tm_prefs)
    tn = _pick(n, tn_prefs)
    in_specs = [pl.BlockSpec((tm, k), lambda i, j: (i, 0)),
                pl.BlockSpec((k, tn), lambda i, j: (0, j))]
    args = [x, w]
    kern = _mm_kernel
    if residual is not None:
        in_specs.append(pl.BlockSpec((tm, tn), lambda i, j: (i, j)))
        args.append(residual)
        kern = _mm_res_kernel
    return pl.pallas_call(
        kern,
        grid=(t // tm, n // tn),
        in_specs=in_specs,
        out_specs=pl.BlockSpec((tm, tn), lambda i, j: (i, j)),
        out_shape=jax.ShapeDtypeStruct((t, n), out_dtype),
        compiler_params=_params("parallel", "arbitrary"),
        name=name,
    )(*args)


def _kv_proj(c, w_kv):
    r, k = c.shape
    hw = QK_NOPE + V_DIM
    tm = _pick(r, (2064, 2048, 1024, 768, 512, 256, 128, 32))
    return pl.pallas_call(
        _mm_kernel,
        grid=(r // tm, N_HEADS),
        in_specs=[pl.BlockSpec((tm, k), lambda i, j: (i, 0)),
                  pl.BlockSpec((k, hw), lambda i, j: (0, j))],
        out_specs=pl.BlockSpec((None, tm, hw), lambda i, j: (j, i, 0)),
        out_shape=jax.ShapeDtypeStruct((N_HEADS, r, hw), BF16),
        compiler_params=_params("parallel", "arbitrary"),
        name="kv_proj",
    )(c, w_kv)


def _kv_proj_t_kernel(c_ref, wk_ref, wvt_ref, k_ref, vt_ref):
    c = c_ref[...].astype(BF16)
    k = jnp.dot(c, wk_ref[...], preferred_element_type=F32)
    vt = lax.dot_general(wvt_ref[...], c, NT_DIMS, preferred_element_type=F32)
    for h in range(N_HEADS):
        k_ref[h] = k[:, h * QK_NOPE:(h + 1) * QK_NOPE].astype(k_ref.dtype)
        vt_ref[h] = vt[h * V_DIM:(h + 1) * V_DIM, :].astype(vt_ref.dtype)


def _kv_proj_t(c, wk, wvt, tk):
    r, kd = c.shape
    return pl.pallas_call(
        _kv_proj_t_kernel,
        grid=(r // tk,),
        in_specs=[pl.BlockSpec((tk, kd), lambda i: (i, 0)),
                  pl.BlockSpec((kd, N_HEADS * QK_NOPE), lambda i: (0, 0)),
                  pl.BlockSpec((N_HEADS * V_DIM, kd), lambda i: (0, 0))],
        out_specs=[pl.BlockSpec((N_HEADS, tk, QK_NOPE), lambda i: (0, i, 0)),
                   pl.BlockSpec((N_HEADS, None, V_DIM, tk), lambda i: (0, i, 0, 0))],
        out_shape=[jax.ShapeDtypeStruct((N_HEADS, r, QK_NOPE), BF16),
                   jax.ShapeDtypeStruct((N_HEADS, r // tk, V_DIM, tk), BF16)],
        compiler_params=_params("parallel"),
        name="kv_proj_t",
    )(c, wk, wvt)


def _post_small_kernel(z_ref, g_ref, cos_ref, sin_ref, c_ref, kr_ref):
    z = z_ref[...]
    ckv = z[:, :KV_RANK]
    ms = jnp.mean(ckv * ckv, axis=-1, keepdims=True)
    c_ref[...] = ckv * lax.rsqrt(ms + RMS_EPS) * g_ref[...]
    kr = z[:, KV_RANK:KV_RANK + QK_ROPE]
    kr_sw = z[:, KV_RANK + QK_ROPE:KV_RANK + 2 * QK_ROPE]
    kr_ref[...] = kr * cos_ref[...] + kr_sw * sin_ref[...]


def _post_small(z_small, g_kv, cos64, sin64):
    t, w = z_small.shape
    tm = _pick(t, (640, 512, 256, 128))
    return pl.pallas_call(
        _post_small_kernel,
        grid=(t // tm,),
        in_specs=[pl.BlockSpec((tm, w), lambda i: (i, 0)),
                  pl.BlockSpec((1, KV_RANK), lambda i: (0, 0)),
                  pl.BlockSpec((tm, QK_ROPE), lambda i: (i, 0)),
                  pl.BlockSpec((tm, QK_ROPE), lambda i: (i, 0))],
        out_specs=[pl.BlockSpec((tm, KV_RANK), lambda i: (i, 0)),
                   pl.BlockSpec((tm, QK_ROPE), lambda i: (i, 0))],
        out_shape=[jax.ShapeDtypeStruct((t, KV_RANK), F32),
                   jax.ShapeDtypeStruct((t, QK_ROPE), F32)],
        compiler_params=_params("parallel"),
        name="latent_post",
    )(z_small, g_kv.reshape(1, KV_RANK), cos64, sin64)


def _q_prep_kernel(qn_ref, qr_ref, qsw_ref, cos_ref, sin_ref, o_ref):
    scale = QK_DIM ** -0.5 * math.log2(math.e)
    cos = jnp.tile(cos_ref[...], (1, N_HEADS))
    sin = jnp.tile(sin_ref[...], (1, N_HEADS))
    qr = (qr_ref[...].astype(F32) * cos + qsw_ref[...].astype(F32) * sin) * scale
    qn = qn_ref[...].astype(F32) * scale
    for h in range(N_HEADS):
        o_ref[h, :, 0:QK_NOPE] = qn[:, h * QK_NOPE:(h + 1) * QK_NOPE].astype(o_ref.dtype)
        o_ref[h, :, QK_NOPE:QK_DIM] = qr[:, h * QK_ROPE:(h + 1) * QK_ROPE].astype(o_ref.dtype)


def _q_prep(z_big, cos64, sin64, row0, rows):
    tm = _pick(math.gcd(rows, row0) if row0 else rows, (256, 128, 32))
    b0 = row0 // tm
    nope_blk = 3 * D_CONV // (N_HEADS * QK_NOPE)
    rope_blk = (3 * D_CONV + N_HEADS * QK_NOPE) // (N_HEADS * QK_ROPE)
    return pl.pallas_call(
        _q_prep_kernel,
        grid=(rows // tm,),
        in_specs=[pl.BlockSpec((tm, N_HEADS * QK_NOPE), lambda i: (b0 + i, nope_blk)),
                  pl.BlockSpec((tm, N_HEADS * QK_ROPE), lambda i: (b0 + i, rope_blk)),
                  pl.BlockSpec((tm, N_HEADS * QK_ROPE), lambda i: (b0 + i, rope_blk + 1)),
                  pl.BlockSpec((tm, QK_ROPE), lambda i: (b0 + i, 0)),
                  pl.BlockSpec((tm, QK_ROPE), lambda i: (b0 + i, 0))],
        out_specs=pl.BlockSpec((N_HEADS, tm, QK_DIM), lambda i: (0, i, 0)),
        out_shape=jax.ShapeDtypeStruct((N_HEADS, rows, QK_DIM), BF16),
        compiler_params=_params("parallel"),
        name="q_prep",
    )(z_big, z_big, z_big, cos64, sin64)


def _conv_kernel(state_ref, pc_ref, px_ref, b_ref, c_ref, x_ref, w_ref, o_ref, last_ref, *, tm, halo):
    i = pl.program_id(1)
    u = c_ref[...].astype(F32) * x_ref[...].astype(F32)
    prev = pc_ref[...].astype(F32) * px_ref[...].astype(F32)
    st = state_ref[...]
    first = i == 0
    um1 = jnp.where(first, st[7:8, :], prev[halo - 1:halo, :])
    um2 = jnp.where(first, st[6:7, :], prev[halo - 2:halo - 1, :])
    row = lax.broadcasted_iota(jnp.int32, u.shape, 0)
    s1 = jnp.where(row == 0, um1, pltpu.roll(u, 1, 0))
    s2 = jnp.where(row == 0, um2, jnp.where(row == 1, um1, pltpu.roll(u, 2, 0)))
    w = w_ref[...]
    y = w[0:1, :] * s2 + w[1:2, :] * s1 + w[2:3, :] * u
    o_ref[...] = (b_ref[...].astype(F32) * y).astype(o_ref.dtype)
    last_ref[...] = u[tm - 8:tm, :]


def _conv_gate(z_big, state, conv_w, row0, nseq, seq_len):
    tm = _pick(seq_len, (512, 256, 128, 32))
    halo = 16
    nb = seq_len // tm
    b0 = row0 // tm
    h0 = row0 // halo
    per = tm // halo
    d = D_CONV

    def hmap(col):
        return lambda s, i: (jnp.maximum(h0 + (s * nb + i) * per - 1, 0), col)

    def bmap(col):
        return lambda s, i: (b0 + s * nb + i, col)

    return pl.pallas_call(
        functools.partial(_conv_kernel, tm=tm, halo=halo),
        grid=(nseq, nb),
        in_specs=[pl.BlockSpec((None, 8, d), lambda s, i: (s, 0, 0)),
                  pl.BlockSpec((halo, d), hmap(1)),
                  pl.BlockSpec((halo, d), hmap(2)),
                  pl.BlockSpec((tm, d), bmap(0)),
                  pl.BlockSpec((tm, d), bmap(1)),
                  pl.BlockSpec((tm, d), bmap(2)),
                  pl.BlockSpec((8, d), lambda s, i: (0, 0))],
        out_specs=[pl.BlockSpec((tm, d), lambda s, i: (s * nb + i, 0)),
                   pl.BlockSpec((None, 8, d), lambda s, i: (s * nb + i, 0, 0))],
        out_shape=[jax.ShapeDtypeStruct((nseq * seq_len, d), BF16),
                   jax.ShapeDtypeStruct((nseq * nb, 8, d), F32)],
        compiler_params=_params("parallel", "arbitrary"),
        name="conv_gate",
    )(state, z_big, z_big, z_big, z_big, z_big, jnp.pad(conv_w, ((0, 5), (0, 0))))


def _attn_kernel(q_ref, k_ref, v_ref, kr_ref, o_ref, *, tq, tk, sk, q_pos0, q_block_stride):
    i = pl.program_id(1)
    q = q_ref[...]
    base = q_pos0 + i * q_block_stride
    chunk_lo = base // CHUNK
    chunk_hi = (base + tq - 1) // CHUNK
    k_end = jnp.minimum((chunk_hi + 1) * CHUNK, sk)
    n_kv = (k_end + tk - 1) // tk
    n_full = jnp.minimum(((chunk_lo + 1) * CHUNK) // tk, n_kv)
    q_pos = base + lax.broadcasted_iota(jnp.int32, (tq, 1), 0)
    row_end = (q_pos // CHUNK + 1) * CHUNK

    def step(j, carry, masked):
        m, l, acc = carry
        ks = pl.ds(pl.multiple_of(j * tk, tk), tk)
        k = jnp.concatenate([k_ref[ks, :], kr_ref[ks, :]], axis=1)
        s = lax.dot_general(q, k, NT_DIMS, preferred_element_type=F32)
        if masked:
            k_pos = j * tk + lax.broadcasted_iota(jnp.int32, (tq, tk), 1)
            s = jnp.where(k_pos < row_end, s, MASK_VALUE)
        m_new = jnp.maximum(m, jnp.max(s, axis=1, keepdims=True))
        p = jnp.exp2(s - m_new)
        alpha = jnp.exp2(m - m_new)
        l = alpha * l + jnp.sum(p, axis=1, keepdims=True)
        acc = alpha * acc + jnp.dot(p.astype(BF16), v_ref[ks, :], preferred_element_type=F32)
        return m_new, l, acc

    init = (jnp.full((tq, 1), MASK_VALUE, F32), jnp.zeros((tq, 1), F32), jnp.zeros((tq, V_DIM), F32))
    carry = lax.fori_loop(0, n_full, functools.partial(step, masked=False), init)
    m, l, acc = lax.fori_loop(n_full, n_kv, functools.partial(step, masked=True), carry)
    o_ref[...] = (acc / l).astype(o_ref.dtype)


def _attention(q, kv, kr, nseq, sq, sk, q_pos0, per_block_positions):
    if per_block_positions:
        tq = _pick(sq, (256, 128, 32))
        stride = tq
    else:
        tq = sq
        stride = 0
    nqb = sq // tq
    tk = _pick(sk, (512, 256, 128))
    return pl.pallas_call(
        functools.partial(_attn_kernel, tq=tq, tk=tk, sk=sk, q_pos0=q_pos0, q_block_stride=stride),
        grid=(N_HEADS, nseq * nqb),
        in_specs=[pl.BlockSpec((None, tq, QK_DIM), lambda h, i: (h, i, 0)),
                  pl.BlockSpec((None, sk, QK_NOPE), lambda h, i: (h, i // nqb, 0)),
                  pl.BlockSpec((None, sk, V_DIM), lambda h, i: (h, i // nqb, 1)),
                  pl.BlockSpec((sk, QK_ROPE), lambda h, i: (i // nqb, 0))],
        out_specs=pl.BlockSpec((tq, V_DIM), lambda h, i: (i, h)),
        out_shape=jax.ShapeDtypeStruct((nseq * sq, N_HEADS * V_DIM), BF16),
        compiler_params=_params("parallel", "arbitrary"),
        name="attention",
    )(q, kv, kv, kr)


def _attn_t_kernel(q_ref, k_ref, kr_ref, vt_ref, o_ref, m_ref, l_ref, acc_ref, s0_ref, s1_ref, *, tb):
    i = pl.program_id(1)
    q = q_ref[...]
    q_pos = i * tb + lax.broadcasted_iota(jnp.int32, (1, tb), 1)
    col_end = (q_pos // CHUNK + 1) * CHUNK

    m_ref[...] = jnp.full(m_ref.shape, MASK_VALUE, F32)
    l_ref[...] = jnp.zeros(l_ref.shape, F32)
    acc_ref[...] = jnp.zeros(acc_ref.shape, F32)

    def produce(s_ref, j):
        ks = pl.ds(pl.multiple_of(j * tb, tb), tb)
        k = jnp.concatenate([k_ref[ks, :], kr_ref[ks, :]], axis=1)
        s_ref[...] = lax.dot_general(k, q, NT_DIMS, preferred_element_type=F32)

    def consume(s_ref, j, masked):
        s = s_ref[...]
        if masked:
            k_pos = j * tb + lax.broadcasted_iota(jnp.int32, (tb, 1), 0)
            s = jnp.where(k_pos < col_end, s, MASK_VALUE)
        m_old = m_ref[...]
        m_new = jnp.maximum(m_old, jnp.max(s, axis=0, keepdims=True))
        p = jnp.exp2(s - m_new)
        alpha = jnp.exp2(m_old - m_new)
        m_ref[...] = m_new
        l_ref[...] = alpha * l_ref[...] + jnp.sum(p, axis=0, keepdims=True)
        acc_ref[...] = alpha * acc_ref[...] + jnp.dot(vt_ref[j], p.astype(BF16), preferred_element_type=F32)

    produce(s0_ref, 0)

    def pair(p, carry):
        j = 2 * p
        produce(s1_ref, j + 1)
        consume(s0_ref, j, False)
        produce(s0_ref, j + 2)
        consume(s1_ref, j + 1, False)
        return carry

    lax.fori_loop(0, i // 2, pair, 0)

    @pl.when(i % 2 == 0)
    def _():
        consume(s0_ref, i, True)

    @pl.when(i % 2 == 1)
    def _():
        produce(s1_ref, i)
        consume(s0_ref, i - 1, False)
        consume(s1_ref, i, True)

    o_ref[...] = (acc_ref[...] / l_ref[...]).T.astype(o_ref.dtype)


def _attention_t(q, k, kr, vt, seq, tk):
    tq = tk
    assert tk % CHUNK == 0 and seq % tk == 0
    return pl.pallas_call(
        functools.partial(_attn_t_kernel, tb=tk),
        grid=(N_HEADS, seq // tq),
        in_specs=[pl.BlockSpec((None, tq, QK_DIM), lambda h, i: (h, i, 0)),
                  pl.BlockSpec((None, seq, QK_NOPE), lambda h, i: (h, 0, 0)),
                  pl.BlockSpec((seq, QK_ROPE), lambda h, i: (0, 0)),
                  pl.BlockSpec((None, seq // tk, V_DIM, tk), lambda h, i: (h, 0, 0, 0))],
        out_specs=pl.BlockSpec((tq, V_DIM), lambda h, i: (i, h)),
        out_shape=jax.ShapeDtypeStruct((seq, N_HEADS * V_DIM), BF16),
        scratch_shapes=[pltpu.VMEM((1, tq), F32), pltpu.VMEM((1, tq), F32), pltpu.VMEM((V_DIM, tq), F32),
                        pltpu.VMEM((tk, tq), F32), pltpu.VMEM((tk, tq), F32)],
        compiler_params=_params("parallel", "arbitrary"),
        name="attention_t",
    )(q, k, kr, vt)


def _merge_kernel(a_ref, o_ref, wa_ref, wb_ref, ga_ref, gb_ref, m_ref):
    ya = jnp.dot(a_ref[...], wa_ref[...], preferred_element_type=F32)
    yb = jnp.dot(o_ref[...], wb_ref[...], preferred_element_type=F32)
    m = jax.nn.sigmoid(ga_ref[...].astype(F32)) * ya + jax.nn.sigmoid(gb_ref[...].astype(F32)) * yb
    m_ref[...] = m.astype(m_ref.dtype)


def _merge(ya_in, o, wa, wb, z_big):
    t, k = ya_in.shape
    n = wa.shape[1]
    tm = _pick(t, (1280, 1024, 512, 256, 128))
    tn = _pick(n, (512, 256, 128))
    ga_blk = (3 * D_CONV + N_HEADS * (QK_NOPE + 2 * QK_ROPE)) // tn
    gb_blk = ga_blk + D_MODEL // tn
    return pl.pallas_call(
        _merge_kernel,
        grid=(t // tm, n // tn),
        in_specs=[pl.BlockSpec((tm, k), lambda i, j: (i, 0)),
                  pl.BlockSpec((tm, k), lambda i, j: (i, 0)),
                  pl.BlockSpec((k, tn), lambda i, j: (0, j)),
                  pl.BlockSpec((k, tn), lambda i, j: (0, j)),
                  pl.BlockSpec((tm, tn), lambda i, j: (i, ga_blk + j)),
                  pl.BlockSpec((tm, tn), lambda i, j: (i, gb_blk + j))],
        out_specs=pl.BlockSpec((tm, tn), lambda i, j: (i, j)),
        out_shape=jax.ShapeDtypeStruct((t, n), BF16),
        compiler_params=_params("parallel", "arbitrary"),
        name="merge",
    )(ya_in, o, wa, wb, z_big, z_big)


def _extract_top(work_ref, rank_ref, val_ref, idx_ref, *, track_rank):
    nk = PEER_NKEYS
    tm = work_ref.shape[1]
    neg = jnp.full((8, tm), -jnp.inf, F32)

    def round_(p, carry):
        idx_prev, pf = carry
        parts = [neg, neg, neg, neg]
        for k in range(nk):
            rows = slice(8 * k, 8 * k + 8)
            hit = idx_prev == float(k)
            w = jnp.where(hit, -jnp.inf, work_ref[rows, :])
            work_ref[rows, :] = w
            if track_rank:
                rank_ref[rows, :] = jnp.where(hit, pf - 1.0, rank_ref[rows, :])
            parts[k % 4] = jnp.maximum(parts[k % 4], w)
        m = jnp.maximum(jnp.maximum(parts[0], parts[1]), jnp.maximum(parts[2], parts[3]))
        big = jnp.full((8, tm), float(nk), F32)
        iparts = [big, big, big, big]
        for k in range(nk):
            w = work_ref[8 * k:8 * k + 8, :]
            iparts[k % 4] = jnp.minimum(iparts[k % 4], jnp.where(w == m, float(k), float(nk)))
        idx = jnp.minimum(jnp.minimum(iparts[0], iparts[1]), jnp.minimum(iparts[2], iparts[3]))
        val_ref[p] = m
        idx_ref[p] = idx
        return idx, pf + 1.0

    idx_last, _ = lax.fori_loop(0, PEER_TOPK, round_,
                                (jnp.full((8, tm), -1.0, F32), jnp.zeros((8, tm), F32)))
    if track_rank:
        for k in range(nk):
            rows = slice(8 * k, 8 * k + 8)
            rank_ref[rows, :] = jnp.where(idx_last == float(k), float(PEER_TOPK - 1), rank_ref[rows, :])


def _peer_select_kernel(qp_ref, k1_ref, k2_ref, k2h_ref, perm_ref,
                        r2_ref, e2_ref, n1_ref, w1_ref,
                        work_ref, rank_ref, a_ref, ia_ref, b_ref, ib_ref):
    nk = PEER_NKEYS
    qp = qp_ref[...]
    tm = qp.shape[0]
    s2h = lax.dot_general(k2h_ref[...], qp, NT_DIMS, preferred_element_type=F32)
    for h in range(PEER_HEADS):
        blk = s2h[h * nk:(h + 1) * nk, :]
        e2_ref[h * nk:(h + 1) * nk, :] = jnp.exp(blk - jnp.max(blk, axis=0, keepdims=True))

    work_ref[...] = lax.dot_general(k1_ref[...], qp, NT_DIMS, preferred_element_type=F32)
    _extract_top(work_ref, rank_ref, a_ref, ia_ref, track_rank=False)
    work_ref[...] = lax.dot_general(k2_ref[...], qp, NT_DIMS, preferred_element_type=F32)
    rank_ref[...] = jnp.full(rank_ref.shape, float(PEER_TOPK), F32)
    _extract_top(work_ref, rank_ref, b_ref, ib_ref, track_rank=True)
    r2_ref[...] = jnp.dot(perm_ref[...], rank_ref[...].astype(BF16), preferred_element_type=F32)

    a = [a_ref[p] for p in range(PEER_TOPK)]
    b = [b_ref[q] for q in range(PEER_TOPK)]
    cand = [a[p] + b[q] for (p, q) in _PAIRS]
    npair = len(_PAIRS)
    beaten = [jnp.zeros((8, tm), F32) for _ in range(npair)]
    for x in range(npair):
        px, qx = _PAIRS[x]
        for y in range(x + 1, npair):
            py, qy = _PAIRS[y]
            if px <= py and qx <= qy:
                beaten[y] = beaten[y] + 1.0
            else:
                gt = jnp.where(cand[y] > cand[x], 1.0, 0.0)
                beaten[x] = beaten[x] + gt
                beaten[y] = beaten[y] + (1.0 - gt)
    sel = [jnp.where(bt < float(PEER_TOPK), 1.0, 0.0) for bt in beaten]
    ea = [jnp.exp(a[p] - a[0]) for p in range(PEER_TOPK)]
    eb = [jnp.exp(b[q] - b[0]) for q in range(PEER_TOPK)]
    z = jnp.zeros((8, tm), F32)
    cnt = [jnp.zeros((8, tm), F32) for _ in range(PEER_TOPK)]
    for x, (p, q) in enumerate(_PAIRS):
        z = z + sel[x] * (ea[p] * eb[q])
        cnt[p] = cnt[p] + sel[x]
    inv_z = 1.0 / z
    w1 = [ea[p] * inv_z for p in range(PEER_TOPK)]
    ia = [ia_ref[p] for p in range(PEER_TOPK)]
    zero = jnp.zeros((8, tm), F32)
    for k in range(nk):
        n1k = zero
        w1k = zero
        for p in range(PEER_TOPK):
            hit = ia[p] == float(k)
            n1k = jnp.where(hit, cnt[p], n1k)
            w1k = jnp.where(hit, w1[p], w1k)
        n1_ref[8 * k:8 * k + 8, :] = n1k
        w1_ref[8 * k:8 * k + 8, :] = w1k


def _peer_select(qp, k1, k2, k2h, perm):
    t, d = qp.shape
    tm = _pick(t, (256, 128))
    rows = PEER_NKEYS * PEER_HEADS
    full = lambda i: (0, 0)
    out = jax.ShapeDtypeStruct((rows, t), F32)
    ospec = pl.BlockSpec((rows, tm), lambda i: (0, i))
    return pl.pallas_call(
        _peer_select_kernel,
        grid=(t // tm,),
        in_specs=[pl.BlockSpec((tm, d), lambda i: (i, 0)),
                  pl.BlockSpec((rows, d), full),
                  pl.BlockSpec((rows, d), full),
                  pl.BlockSpec((rows, d), full),
                  pl.BlockSpec((rows, rows), full)],
        out_specs=[ospec, ospec, ospec, ospec],
        out_shape=[out, out, out, out],
        scratch_shapes=[pltpu.VMEM((rows, tm), F32), pltpu.VMEM((rows, tm), F32),
                        pltpu.VMEM((PEER_TOPK, 8, tm), F32), pltpu.VMEM((PEER_TOPK, 8, tm), F32),
                        pltpu.VMEM((PEER_TOPK, 8, tm), F32), pltpu.VMEM((PEER_TOPK, 8, tm), F32)],
        compiler_params=_params("parallel"),
        name="peer_select",
    )(qp, k1, k2, k2h, perm)


def _gelu_tanh(x):
    c = math.sqrt(2.0 / math.pi)
    return 0.5 * x * (1.0 + jnp.tanh(c * (x + 0.044715 * (x * x * x))))


def _peer_main_kernel(hn_ref, u_ref, v_ref, r2_ref, e2_ref, n1_ref, w1_ref, o_ref,
                      act_ref, pt_ref, *, tc):
    nk = PEER_NKEYS
    te, tm = act_ref.shape
    ei = pl.program_id(1)

    @pl.when(ei == 0)
    def _():
        o_ref[...] = jnp.zeros(o_ref.shape, o_ref.dtype)

    act_ref[...] = _gelu_tanh(lax.dot_general(u_ref[...], hn_ref[...], NT_DIMS, preferred_element_type=F32))
    for il in range(te // nk):
        rows = slice(il * nk, (il + 1) * nk)
        for c in range(tm // tc):
            cols = slice(c * tc, (c + 1) * tc)
            gate = jnp.zeros((nk, tc), F32)
            for h in range(PEER_HEADS):
                hr = slice(h * nk, (h + 1) * nk)
                r = il * PEER_HEADS + h
                keep = r2_ref[hr, cols] < n1_ref[r:r + 1, cols]
                gate = gate + jnp.where(keep, e2_ref[hr, cols], 0.0) * w1_ref[r:r + 1, cols]
            pt_ref[rows, cols] = (gate * act_ref[rows, cols]).astype(pt_ref.dtype)
    o_ref[...] += lax.dot_general(pt_ref[...], v_ref[...], TN_DIMS, preferred_element_type=F32)


def _peer_main(hn, u, v, r2, e2, n1, w1):
    t, d = hn.shape
    e = u.shape[0]
    tm = _pick(t, (640, 512, 256, 128))
    te = 512
    tc = 128
    rows = PEER_NKEYS * PEER_HEADS
    sub = te // PEER_NKEYS * PEER_HEADS
    return pl.pallas_call(
        functools.partial(_peer_main_kernel, tc=tc),
        grid=(t // tm, e // te),
        in_specs=[pl.BlockSpec((tm, d), lambda i, j: (i, 0)),
                  pl.BlockSpec((te, d), lambda i, j: (j, 0)),
                  pl.BlockSpec((te, d), lambda i, j: (j, 0)),
                  pl.BlockSpec((rows, tm), lambda i, j: (0, i)),
                  pl.BlockSpec((rows, tm), lambda i, j: (0, i)),
                  pl.BlockSpec((sub, tm), lambda i, j: (j, i)),
                  pl.BlockSpec((sub, tm), lambda i, j: (j, i))],
        out_specs=pl.BlockSpec((tm, d), lambda i, j: (i, 0)),
        out_shape=jax.ShapeDtypeStruct((t, d), F32),
        scratch_shapes=[pltpu.VMEM((te, tm), F32), pltpu.VMEM((te, tm), BF16)],
        compiler_params=_params("parallel", "arbitrary"),
        name="peer_main",
    )(hn, u, v, r2, e2, n1, w1)


def _ple_kernel(xn_ref, p_ref, wg_ref, wp_ref, h_ref, o_ref):
    gate = jnp.dot(xn_ref[...], wg_ref[...], preferred_element_type=F32)
    pe = jnp.dot(p_ref[...].astype(BF16), wp_ref[...], preferred_element_type=F32)
    o_ref[...] = h_ref[...] + pe * jax.nn.sigmoid(gate)


def _ple(xn, p, wg, wp, h):
    t, d = xn.shape
    pd = p.shape[1]
    tm = _pick(t, (1280, 1024, 512, 256, 128))
    tn = _pick(d, (512, 256, 128))
    return pl.pallas_call(
        _ple_kernel,
        grid=(t // tm, d // tn),
        in_specs=[pl.BlockSpec((tm, d), lambda i, j: (i, 0)),
                  pl.BlockSpec((tm, pd), lambda i, j: (i, 0)),
                  pl.BlockSpec((d, tn), lambda i, j: (0, j)),
                  pl.BlockSpec((pd, tn), lambda i, j: (0, j)),
                  pl.BlockSpec((tm, tn), lambda i, j: (i, j))],
        out_specs=pl.BlockSpec((tm, tn), lambda i, j: (i, j)),
        out_shape=jax.ShapeDtypeStruct((t, d), F32),
        compiler_params=_params("parallel", "arbitrary"),
        name="ple",
    )(xn, p, wg, wp, h)


def _swap_halves(w):
    half = w.shape[-1] // 2
    return jnp.concatenate([w[..., half:], w[..., :half]], axis=-1)


def _split_w_in(w_in):
    d = w_in.shape[0]
    splits = [int(s) for s in np.cumsum(IN_SIZES)[:-1]]
    wb, wc, wx, wq, wckv, wkr, wga, wgb = jnp.split(w_in, splits, axis=1)
    wq = wq.reshape(d, N_HEADS, QK_DIM)
    wq_n = wq[:, :, :QK_NOPE].reshape(d, N_HEADS * QK_NOPE)
    wq_r = wq[:, :, QK_NOPE:]
    w_big = jnp.concatenate(
        [wb, wc, wx, wq_n, wq_r.reshape(d, -1), _swap_halves(wq_r).reshape(d, -1), wga, wgb], axis=1)
    w_small = jnp.concatenate([wckv, wkr, _swap_halves(wkr)], axis=1)
    return w_big.astype(BF16), w_small.astype(BF16)


def _rope_tables(pos):
    inv = 1.0 / (ROPE_THETA ** (jnp.arange(0, QK_ROPE, 2, dtype=F32) / QK_ROPE))
    ang = pos.astype(F32)[:, None] * inv[None, :]
    cos, sin = jnp.cos(ang), jnp.sin(ang)
    return jnp.concatenate([cos, cos], axis=1), jnp.concatenate([-sin, sin], axis=1)


def _key_matrices(sub_keys):
    nk, hd, half = PEER_NKEYS, PEER_HEADS, PEER_DKEY // 2
    eye = jnp.eye(hd, dtype=F32)

    def build(c, head_major):
        sel = jnp.zeros((2,), F32).at[c].set(1.0)
        m = sub_keys[c][:, None, None, None, :] * eye[None, :, :, None, None] * sel[None, None, None, :, None]
        if head_major:
            m = jnp.transpose(m, (1, 0, 2, 3, 4))
        return m.reshape(nk * hd, hd * PEER_DKEY).astype(BF16)

    r = np.arange(nk * hd)
    perm = np.zeros((nk * hd, nk * hd), np.float32)
    perm[(r % hd) * nk + r // hd, r] = 1.0
    return build(0, False), build(1, False), build(1, True), jnp.asarray(perm, BF16)


def kernel(x_prompt, x_sample, cache_conv, cache_ckv, cache_krope, p_prompt, p_sample, g_mix, w_in, conv_w, g_kv, w_kv_b, w_a_out, w_b_out, w_o, g_ffn, w_pq, sub_keys, u_tab, v_tab, g_ple, w_ple_gate, w_ple, g_final):
    assert x_prompt.shape[0] == 1 and w_in.shape[0] == 1
    seq = x_prompt.shape[1]
    nb, dseq = x_sample.shape[0], x_sample.shape[1]
    past = cache_ckv.shape[2]
    d = D_MODEL
    ns = nb * dseq
    t = seq + ns

    x_all = jnp.concatenate([x_prompt.reshape(seq, d), x_sample.reshape(ns, d)], axis=0)
    p_all = jnp.concatenate([p_prompt[0].reshape(seq, -1), p_sample[0].reshape(ns, -1)], axis=0)
    pos = jnp.concatenate([jnp.arange(seq), jnp.tile(past + jnp.arange(dseq), nb)])
    cos64, sin64 = _rope_tables(pos)
    w_big, w_small = _split_w_in(w_in[0])

    xn = _rmsnorm(x_all, g_mix[0], BF16)
    z_big = _matmul(xn, w_big, BF16, name="in_proj")
    z_small = _matmul(xn, w_small, F32, tn_prefs=(w_small.shape[1],), name="in_proj_small")
    c_new, kr_new = _post_small(z_small, g_kv[0], cos64, sin64)

    zero_state = jnp.zeros((1, 8, D_CONV), F32)
    ya_p, last_p = _conv_gate(z_big, zero_state, conv_w[0], 0, 1, seq)
    state_s = jnp.pad(cache_conv[0], ((0, 0), (6, 0), (0, 0)))
    ya_s, last_s = _conv_gate(z_big, state_s, conv_w[0], seq, nb, dseq)
    ya_in = jnp.concatenate([ya_p, ya_s], axis=0)
    conv_p = last_p[-1, 6:8, :].reshape(1, 1, 2, D_CONV)
    if dseq >= 2:
        conv_s = last_s[:, 6:8, :].reshape(1, nb, 2, D_CONV)
    else:
        raise NotImplementedError("sample blocks shorter than the convolution state")

    w_kv = w_kv_b[0].astype(BF16)
    q_p = _q_prep(z_big, cos64, sin64, 0, seq)
    q_s = _q_prep(z_big, cos64, sin64, seq, ns)
    tk = _pick(seq, (512, 256, 128))
    w_kv_h = w_kv.reshape(KV_RANK, N_HEADS, QK_NOPE + V_DIM)
    wk = w_kv_h[:, :, :QK_NOPE].reshape(KV_RANK, N_HEADS * QK_NOPE)
    wvt = w_kv_h[:, :, QK_NOPE:].reshape(KV_RANK, N_HEADS * V_DIM).T
    k_p, vt_p = _kv_proj_t(c_new[:seq], wk, wvt, tk)
    o_p = _attention_t(q_p, k_p, kr_new[:seq].astype(BF16), vt_p, seq, tk)
    c_all = jnp.concatenate([cache_ckv[0], c_new[seq:].reshape(nb, dseq, KV_RANK)], axis=1)
    kr_all = jnp.concatenate([cache_krope[0], kr_new[seq:].reshape(nb, dseq, QK_ROPE)], axis=1)
    sk = past + dseq
    kv_s = _kv_proj(c_all.reshape(nb * sk, KV_RANK), w_kv)
    o_s = _attention(q_s, kv_s, kr_all.reshape(nb * sk, QK_ROPE).astype(BF16), nb, dseq, sk, past, False)
    o_all = jnp.concatenate([o_p, o_s], axis=0)

    m = _merge(ya_in, o_all, w_a_out[0].astype(BF16), w_b_out[0].astype(BF16), z_big)
    h1 = _matmul(m, w_o[0].astype(BF16), F32, residual=x_all, name="out_proj")

    hn = _rmsnorm(h1, g_ffn[0], BF16)
    qp = _matmul(hn, w_pq[0].astype(BF16), BF16, name="peer_query")
    k1, k2, k2h, perm = _key_matrices(sub_keys[0])
    r2, e2, n1, w1 = _peer_select(qp, k1, k2, k2h, perm)
    peer = _peer_main(hn, u_tab[0].astype(BF16), v_tab[0].astype(BF16), r2, e2, n1, w1)

    h2, x2n = _add_rmsnorm(h1, peer, g_ple[0], BF16)
    h3 = _ple(x2n, p_all, w_ple_gate[0].astype(BF16), w_ple[0].astype(BF16), h2)
    y = _rmsnorm(h3, g_final, F32)

    y_prompt = y[:seq].reshape(1, seq, d)
    y_sample = y[seq:].reshape(nb, dseq, d)
    ckv_p = c_new[:seq].reshape(1, 1, seq, KV_RANK)
    kr_p = kr_new[:seq].reshape(1, 1, seq, QK_ROPE)
    ckv_s = c_new[seq:].reshape(1, nb, dseq, KV_RANK)
    kr_s = kr_new[seq:].reshape(1, nb, dseq, QK_ROPE)
    return (y_prompt, y_sample, conv_p, ckv_p, kr_p, conv_s, ckv_s, kr_s)
```

```python
import functools
import math

import numpy as np
import jax
import jax.numpy as jnp
from jax import lax
from jax.experimental import pallas as pl
from jax.experimental.pallas import tpu as pltpu

D_MODEL = 2048
D_CONV = 2048
N_HEADS = 16
QK_NOPE = 128
QK_ROPE = 64
QK_DIM = QK_NOPE + QK_ROPE
V_DIM = 128
KV_RANK = 512
CHUNK = 64
ROPE_THETA = 10000.0
PEER_HEADS = 8
PEER_NKEYS = 128
PEER_TOPK = 16
PEER_DKEY = 256
PEER_EXPERTS = PEER_NKEYS * PEER_NKEYS
RMS_EPS = 1e-6
IN_SIZES = (D_CONV, D_CONV, D_CONV, N_HEADS * QK_DIM, KV_RANK, QK_ROPE, D_MODEL, D_MODEL)

BF16 = jnp.bfloat16
F32 = jnp.float32
VMEM_LIMIT_BYTES = 56 * 1024 * 1024
MASK_VALUE = -1e30
NT_DIMS = (((1,), (1,)), ((), ()))
TN_DIMS = (((0,), (0,)), ((), ()))

_PAIRS = tuple((p, q) for p in range(PEER_TOPK) for q in range(PEER_TOPK)
               if (p + 1) * (q + 1) <= PEER_TOPK)


def _pick(n, prefs):
    for p in prefs:
        if n % p == 0:
            return p
    return n


def _params(*sem):
    return pltpu.CompilerParams(dimension_semantics=sem, vmem_limit_bytes=VMEM_LIMIT_BYTES)


def _rms_kernel(x_ref, g_ref, o_ref):
    x = x_ref[...]
    ms = jnp.mean(x * x, axis=-1, keepdims=True)
    o_ref[...] = (x * lax.rsqrt(ms + RMS_EPS) * g_ref[...]).astype(o_ref.dtype)


def _rmsnorm(x, g, out_dtype):
    t, d = x.shape
    tm = _pick(t, (640, 512, 256, 128))
    return pl.pallas_call(
        _rms_kernel,
        grid=(t // tm,),
        in_specs=[pl.BlockSpec((tm, d), lambda i: (i, 0)),
                  pl.BlockSpec((1, d), lambda i: (0, 0))],
        out_specs=pl.BlockSpec((tm, d), lambda i: (i, 0)),
        out_shape=jax.ShapeDtypeStruct((t, d), out_dtype),
        compiler_params=_params("parallel"),
        name="rmsnorm",
    )(x, g.reshape(1, d))


def _add_rms_kernel(a_ref, b_ref, g_ref, s_ref, o_ref):
    x = a_ref[...] + b_ref[...]
    s_ref[...] = x
    ms = jnp.mean(x * x, axis=-1, keepdims=True)
    o_ref[...] = (x * lax.rsqrt(ms + RMS_EPS) * g_ref[...]).astype(o_ref.dtype)


def _add_rmsnorm(a, b, g, out_dtype):
    t, d = a.shape
    tm = _pick(t, (640, 512, 256, 128))
    row = pl.BlockSpec((tm, d), lambda i: (i, 0))
    return pl.pallas_call(
        _add_rms_kernel,
        grid=(t // tm,),
        in_specs=[row, row, pl.BlockSpec((1, d), lambda i: (0, 0))],
        out_specs=[row, row],
        out_shape=[jax.ShapeDtypeStruct((t, d), F32), jax.ShapeDtypeStruct((t, d), out_dtype)],
        compiler_params=_params("parallel"),
        name="add_rmsnorm",
    )(a, b, g.reshape(1, d))


def _mm_kernel(x_ref, w_ref, o_ref):
    x = x_ref[...].astype(BF16)
    o_ref[...] = jnp.dot(x, w_ref[...], preferred_element_type=F32).astype(o_ref.dtype)


def _mm_res_kernel(x_ref, w_ref, r_ref, o_ref):
    x = x_ref[...].astype(BF16)
    o_ref[...] = (r_ref[...] + jnp.dot(x, w_ref[...], preferred_element_type=F32)).astype(o_ref.dtype)


def _matmul(x, w, out_dtype, residual=None, tm_prefs=(1280, 1024, 768, 512, 256, 128),
            tn_prefs=(1024, 512, 256, 128), name="matmul"):
    t, k = x.shape
    n = w.shape[1]
    tm = _pick(t, tm_prefs)
    tn = _pick(n, tn_prefs)
    in_specs = [pl.BlockSpec((tm, k), lambda i, j: (i, 0)),
                pl.BlockSpec((k, tn), lambda i, j: (0, j))]
    args = [x, w]
    kern = _mm_kernel
    if residual is not None:
        in_specs.append(pl.BlockSpec((tm, tn), lambda i, j: (i, j)))
        args.append(residual)
        kern = _mm_res_kernel
    return pl.pallas_call(
        kern,
        grid=(t // tm, n // tn),
        in_specs=in_specs,
        out_specs=pl.BlockSpec((tm, tn), lambda i, j: (i, j)),
        out_shape=jax.ShapeDtypeStruct((t, n), out_dtype),
        compiler_params=_params("parallel", "arbitrary"),
        name=name,
    )(*args)


def _kv_proj(c, w_kv):
    r, k = c.shape
    hw = QK_NOPE + V_DIM
    tm = _pick(r, (2064, 2048, 1024, 768, 512, 256, 128, 32))
    return pl.pallas_call(
        _mm_kernel,
        grid=(r // tm, N_HEADS),
        in_specs=[pl.BlockSpec((tm, k), lambda i, j: (i, 0)),
                  pl.BlockSpec((k, hw), lambda i, j: (0, j))],
        out_specs=pl.BlockSpec((None, tm, hw), lambda i, j: (j, i, 0)),
        out_shape=jax.ShapeDtypeStruct((N_HEADS, r, hw), BF16),
        compiler_params=_params("parallel", "arbitrary"),
        name="kv_proj",
    )(c, w_kv)


def _kv_proj_t_kernel(c_ref, wk_ref, wvt_ref, k_ref, vt_ref):
    c = c_ref[...].astype(BF16)
    k = jnp.dot(c, wk_ref[...], preferred_element_type=F32)
    vt = lax.dot_general(wvt_ref[...], c, NT_DIMS, preferred_element_type=F32)
    for h in range(N_HEADS):
        k_ref[h] = k[:, h * QK_NOPE:(h + 1) * QK_NOPE].astype(k_ref.dtype)
        vt_ref[h] = vt[h * V_DIM:(h + 1) * V_DIM, :].astype(vt_ref.dtype)


def _kv_proj_t(c, wk, wvt, tk):
    r, kd = c.shape
    return pl.pallas_call(
        _kv_proj_t_kernel,
        grid=(r // tk,),
        in_specs=[pl.BlockSpec((tk, kd), lambda i: (i, 0)),
                  pl.BlockSpec((kd, N_HEADS * QK_NOPE), lambda i: (0, 0)),
                  pl.BlockSpec((N_HEADS * V_DIM, kd), lambda i: (0, 0))],
        out_specs=[pl.BlockSpec((N_HEADS, tk, QK_NOPE), lambda i: (0, i, 0)),
                   pl.BlockSpec((N_HEADS, None, V_DIM, tk), lambda i: (0, i, 0, 0))],
        out_shape=[jax.ShapeDtypeStruct((N_HEADS, r, QK_NOPE), BF16),
                   jax.ShapeDtypeStruct((N_HEADS, r // tk, V_DIM, tk), BF16)],
        compiler_params=_params("parallel"),
        name="kv_proj_t",
    )(c, wk, wvt)


def _post_small_kernel(z_ref, g_ref, cos_ref, sin_ref, c_ref, kr_ref):
    z = z_ref[...]
    ckv = z[:, :KV_RANK]
    ms = jnp.mean(ckv * ckv, axis=-1, keepdims=True)
    c_ref[...] = ckv * lax.rsqrt(ms + RMS_EPS) * g_ref[...]
    kr = z[:, KV_RANK:KV_RANK + QK_ROPE]
    kr_sw = z[:, KV_RANK + QK_ROPE:KV_RANK + 2 * QK_ROPE]
    kr_ref[...] = kr * cos_ref[...] + kr_sw * sin_ref[...]


def _post_small(z_small, g_kv, cos64, sin64):
    t, w = z_small.shape
    tm = _pick(t, (640, 512, 256, 128))
    return pl.pallas_call(
        _post_small_kernel,
        grid=(t // tm,),
        in_specs=[pl.BlockSpec((tm, w), lambda i: (i, 0)),
                  pl.BlockSpec((1, KV_RANK), lambda i: (0, 0)),
                  pl.BlockSpec((tm, QK_ROPE), lambda i: (i, 0)),
                  pl.BlockSpec((tm, QK_ROPE), lambda i: (i, 0))],
        out_specs=[pl.BlockSpec((tm, KV_RANK), lambda i: (i, 0)),
                   pl.BlockSpec((tm, QK_ROPE), lambda i: (i, 0))],
        out_shape=[jax.ShapeDtypeStruct((t, KV_RANK), F32),
                   jax.ShapeDtypeStruct((t, QK_ROPE), F32)],
        compiler_params=_params("parallel"),
        name="latent_post",
    )(z_small, g_kv.reshape(1, KV_RANK), cos64, sin64)


def _q_prep_kernel(qn_ref, qr_ref, qsw_ref, cos_ref, sin_ref, o_ref):
    scale = QK_DIM ** -0.5 * math.log2(math.e)
    cos = jnp.tile(cos_ref[...], (1, N_HEADS))
    sin = jnp.tile(sin_ref[...], (1, N_HEADS))
    qr = (qr_ref[...].astype(F32) * cos + qsw_ref[...].astype(F32) * sin) * scale
    qn = qn_ref[...].astype(F32) * scale
    for h in range(N_HEADS):
        o_ref[h, :, 0:QK_NOPE] = qn[:, h * QK_NOPE:(h + 1) * QK_NOPE].astype(o_ref.dtype)
        o_ref[h, :, QK_NOPE:QK_DIM] = qr[:, h * QK_ROPE:(h + 1) * QK_ROPE].astype(o_ref.dtype)


def _q_prep(z_big, cos64, sin64, row0, rows):
    tm = _pick(math.gcd(rows, row0) if row0 else rows, (256, 128, 32))
    b0 = row0 // tm
    nope_blk = 3 * D_CONV // (N_HEADS * QK_NOPE)
    rope_blk = (3 * D_CONV + N_HEADS * QK_NOPE) // (N_HEADS * QK_ROPE)
    return pl.pallas_call(
        _q_prep_kernel,
        grid=(rows // tm,),
        in_specs=[pl.BlockSpec((tm, N_HEADS * QK_NOPE), lambda i: (b0 + i, nope_blk)),
                  pl.BlockSpec((tm, N_HEADS * QK_ROPE), lambda i: (b0 + i, rope_blk)),
                  pl.BlockSpec((tm, N_HEADS * QK_ROPE), lambda i: (b0 + i, rope_blk + 1)),
                  pl.BlockSpec((tm, QK_ROPE), lambda i: (b0 + i, 0)),
                  pl.BlockSpec((tm, QK_ROPE), lambda i: (b0 + i, 0))],
        out_specs=pl.BlockSpec((N_HEADS, tm, QK_DIM), lambda i: (0, i, 0)),
        out_shape=jax.ShapeDtypeStruct((N_HEADS, rows, QK_DIM), BF16),
        compiler_params=_params("parallel"),
        name="q_prep",
    )(z_big, z_big, z_big, cos64, sin64)


def _conv_kernel(state_ref, pc_ref, px_ref, b_ref, c_ref, x_ref, w_ref, o_ref, last_ref, *, tm, halo):
    i = pl.program_id(1)
    u = c_ref[...].astype(F32) * x_ref[...].astype(F32)
    prev = pc_ref[...].astype(F32) * px_ref[...].astype(F32)
    st = state_ref[...]
    first = i == 0
    um1 = jnp.where(first, st[7:8, :], prev[halo - 1:halo, :])
    um2 = jnp.where(first, st[6:7, :], prev[halo - 2:halo - 1, :])
    row = lax.broadcasted_iota(jnp.int32, u.shape, 0)
    s1 = jnp.where(row == 0, um1, pltpu.roll(u, 1, 0))
    s2 = jnp.where(row == 0, um2, jnp.where(row == 1, um1, pltpu.roll(u, 2, 0)))
    w = w_ref[...]
    y = w[0:1, :] * s2 + w[1:2, :] * s1 + w[2:3, :] * u
    o_ref[...] = (b_ref[...].astype(F32) * y).astype(o_ref.dtype)
    last_ref[...] = u[tm - 8:tm, :]


def _conv_gate(z_big, state, conv_w, row0, nseq, seq_len):
    tm = _pick(seq_len, (512, 256, 128, 32))
    halo = 16
    nb = seq_len // tm
    b0 = row0 // tm
    h0 = row0 // halo
    per = tm // halo
    d = D_CONV

    def hmap(col):
        return lambda s, i: (jnp.maximum(h0 + (s * nb + i) * per - 1, 0), col)

    def bmap(col):
        return lambda s, i: (b0 + s * nb + i, col)

    return pl.pallas_call(
        functools.partial(_conv_kernel, tm=tm, halo=halo),
        grid=(nseq, nb),
        in_specs=[pl.BlockSpec((None, 8, d), lambda s, i: (s, 0, 0)),
                  pl.BlockSpec((halo, d), hmap(1)),
                  pl.BlockSpec((halo, d), hmap(2)),
                  pl.BlockSpec((tm, d), bmap(0)),
                  pl.BlockSpec((tm, d), bmap(1)),
                  pl.BlockSpec((tm, d), bmap(2)),
                  pl.BlockSpec((8, d), lambda s, i: (0, 0))],
        out_specs=[pl.BlockSpec((tm, d), lambda s, i: (s * nb + i, 0)),
                   pl.BlockSpec((None, 8, d), lambda s, i: (s * nb + i, 0, 0))],
        out_shape=[jax.ShapeDtypeStruct((nseq * seq_len, d), BF16),
                   jax.ShapeDtypeStruct((nseq * nb, 8, d), F32)],
        compiler_params=_params("parallel", "arbitrary"),
        name="conv_gate",
    )(state, z_big, z_big, z_big, z_big, z_big, jnp.pad(conv_w, ((0, 5), (0, 0))))


def _attn_kernel(q_ref, k_ref, v_ref, kr_ref, o_ref, *, tq, tk, sk, q_pos0, q_block_stride):
    i = pl.program_id(1)
    q = q_ref[...]
    base = q_pos0 + i * q_block_stride
    chunk_lo = base // CHUNK
    chunk_hi = (base + tq - 1) // CHUNK
    k_end = jnp.minimum((chunk_hi + 1) * CHUNK, sk)
    n_kv = (k_end + tk - 1) // tk
    n_full = jnp.minimum(((chunk_lo + 1) * CHUNK) // tk, n_kv)
    q_pos = base + lax.broadcasted_iota(jnp.int32, (tq, 1), 0)
    row_end = (q_pos // CHUNK + 1) * CHUNK

    def step(j, carry, masked):
        m, l, acc = carry
        ks = pl.ds(pl.multiple_of(j * tk, tk), tk)
        k = jnp.concatenate([k_ref[ks, :], kr_ref[ks, :]], axis=1)
        s = lax.dot_general(q, k, NT_DIMS, preferred_element_type=F32)
        if masked:
            k_pos = j * tk + lax.broadcasted_iota(jnp.int32, (tq, tk), 1)
            s = jnp.where(k_pos < row_end, s, MASK_VALUE)
        m_new = jnp.maximum(m, jnp.max(s, axis=1, keepdims=True))
        p = jnp.exp2(s - m_new)
        alpha = jnp.exp2(m - m_new)
        l = alpha * l + jnp.sum(p, axis=1, keepdims=True)
        acc = alpha * acc + jnp.dot(p.astype(BF16), v_ref[ks, :], preferred_element_type=F32)
        return m_new, l, acc

    init = (jnp.full((tq, 1), MASK_VALUE, F32), jnp.zeros((tq, 1), F32), jnp.zeros((tq, V_DIM), F32))
    carry = lax.fori_loop(0, n_full, functools.partial(step, masked=False), init)
    m, l, acc = lax.fori_loop(n_full, n_kv, functools.partial(step, masked=True), carry)
    o_ref[...] = (acc / l).astype(o_ref.dtype)


def _attention(q, kv, kr, nseq, sq, sk, q_pos0, per_block_positions):
    if per_block_positions:
        tq = _pick(sq, (256, 128, 32))
        stride = tq
    else:
        tq = sq
        stride = 0
    nqb = sq // tq
    tk = _pick(sk, (512, 256, 128))
    return pl.pallas_call(
        functools.partial(_attn_kernel, tq=tq, tk=tk, sk=sk, q_pos0=q_pos0, q_block_stride=stride),
        grid=(N_HEADS, nseq * nqb),
        in_specs=[pl.BlockSpec((None, tq, QK_DIM), lambda h, i: (h, i, 0)),
                  pl.BlockSpec((None, sk, QK_NOPE), lambda h, i: (h, i // nqb, 0)),
                  pl.BlockSpec((None, sk, V_DIM), lambda h, i: (h, i // nqb, 1)),
                  pl.BlockSpec((sk, QK_ROPE), lambda h, i: (i // nqb, 0))],
        out_specs=pl.BlockSpec((tq, V_DIM), lambda h, i: (i, h)),
        out_shape=jax.ShapeDtypeStruct((nseq * sq, N_HEADS * V_DIM), BF16),
        compiler_params=_params("parallel", "arbitrary"),
        name="attention",
    )(q, kv, kv, kr)


def _attn_t_kernel(q_ref, k_ref, kr_ref, vt_ref, o_ref, m_ref, l_ref, acc_ref, s0_ref, s1_ref, *, tb):
    i = pl.program_id(1)
    q = q_ref[...]
    q_pos = i * tb + lax.broadcasted_iota(jnp.int32, (1, tb), 1)
    col_end = (q_pos // CHUNK + 1) * CHUNK

    m_ref[...] = jnp.full(m_ref.shape, MASK_VALUE, F32)
    l_ref[...] = jnp.zeros(l_ref.shape, F32)
    acc_ref[...] = jnp.zeros(acc_ref.shape, F32)

    def produce(s_ref, j):
        ks = pl.ds(pl.multiple_of(j * tb, tb), tb)
        k = jnp.concatenate([k_ref[ks, :], kr_ref[ks, :]], axis=1)
        s_ref[...] = lax.dot_general(k, q, NT_DIMS, preferred_element_type=F32)

    def consume(s_ref, j, masked):
        s = s_ref[...]
        if masked:
            k_pos = j * tb + lax.broadcasted_iota(jnp.int32, (tb, 1), 0)
            s = jnp.where(k_pos < col_end, s, MASK_VALUE)
        m_old = m_ref[...]
        m_new = jnp.maximum(m_old, jnp.max(s, axis=0, keepdims=True))
        p = jnp.exp2(s - m_new)
        alpha = jnp.exp2(m_old - m_new)
        m_ref[...] = m_new
        l_ref[...] = alpha * l_ref[...] + jnp.sum(p, axis=0, keepdims=True)
        acc_ref[...] = alpha * acc_ref[...] + jnp.dot(vt_ref[j], p.astype(BF16), preferred_element_type=F32)

    produce(s0_ref, 0)

    def pair(p, carry):
        j = 2 * p
        produce(s1_ref, j + 1)
        consume(s0_ref, j, False)
        produce(s0_ref, j + 2)
        consume(s1_ref, j + 1, False)
        return carry

    lax.fori_loop(0, i // 2, pair, 0)

    @pl.when(i % 2 == 0)
    def _():
        consume(s0_ref, i, True)

    @pl.when(i % 2 == 1)
    def _():
        produce(s1_ref, i)
        consume(s0_ref, i - 1, False)
        consume(s1_ref, i, True)

    o_ref[...] = (acc_ref[...] / l_ref[...]).T.astype(o_ref.dtype)


def _attention_t(q, k, kr, vt, seq, tk):
    tq = tk
    assert tk % CHUNK == 0 and seq % tk == 0
    return pl.pallas_call(
        functools.partial(_attn_t_kernel, tb=tk),
        grid=(N_HEADS, seq // tq),
        in_specs=[pl.BlockSpec((None, tq, QK_DIM), lambda h, i: (h, i, 0)),
                  pl.BlockSpec((None, seq, QK_NOPE), lambda h, i: (h, 0, 0)),
                  pl.BlockSpec((seq, QK_ROPE), lambda h, i: (0, 0)),
                  pl.BlockSpec((None, seq // tk, V_DIM, tk), lambda h, i: (h, 0, 0, 0))],
        out_specs=pl.BlockSpec((tq, V_DIM), lambda h, i: (i, h)),
        out_shape=jax.ShapeDtypeStruct((seq, N_HEADS * V_DIM), BF16),
        scratch_shapes=[pltpu.VMEM((1, tq), F32), pltpu.VMEM((1, tq), F32), pltpu.VMEM((V_DIM, tq), F32),
                        pltpu.VMEM((tk, tq), F32), pltpu.VMEM((tk, tq), F32)],
        compiler_params=_params("parallel", "arbitrary"),
        name="attention_t",
    )(q, k, kr, vt)


def _merge_kernel(a_ref, o_ref, wa_ref, wb_ref, ga_ref, gb_ref, m_ref):
    ya = jnp.dot(a_ref[...], wa_ref[...], preferred_element_type=F32)
    yb = jnp.dot(o_ref[...], wb_ref[...], preferred_element_type=F32)
    m = jax.nn.sigmoid(ga_ref[...].astype(F32)) * ya + jax.nn.sigmoid(gb_ref[...].astype(F32)) * yb
    m_ref[...] = m.astype(m_ref.dtype)


def _merge(ya_in, o, wa, wb, z_big):
    t, k = ya_in.shape
    n = wa.shape[1]
    tm = _pick(t, (1280, 1024, 512, 256, 128))
    tn = _pick(n, (512, 256, 128))
    ga_blk = (3 * D_CONV + N_HEADS * (QK_NOPE + 2 * QK_ROPE)) // tn
    gb_blk = ga_blk + D_MODEL // tn
    return pl.pallas_call(
        _merge_kernel,
        grid=(t // tm, n // tn),
        in_specs=[pl.BlockSpec((tm, k), lambda i, j: (i, 0)),
                  pl.BlockSpec((tm, k), lambda i, j: (i, 0)),
                  pl.BlockSpec((k, tn), lambda i, j: (0, j)),
                  pl.BlockSpec((k, tn), lambda i, j: (0, j)),
                  pl.BlockSpec((tm, tn), lambda i, j: (i, ga_blk + j)),
                  pl.BlockSpec((tm, tn), lambda i, j: (i, gb_blk + j))],
        out_specs=pl.BlockSpec((tm, tn), lambda i, j: (i, j)),
        out_shape=jax.ShapeDtypeStruct((t, n), BF16),
        compiler_params=_params("parallel", "arbitrary"),
        name="merge",
    )(ya_in, o, wa, wb, z_big, z_big)


def _extract_top(work_ref, rank_ref, val_ref, idx_ref, *, track_rank):
    nk = PEER_NKEYS
    tm = work_ref.shape[1]
    neg = jnp.full((8, tm), -jnp.inf, F32)

    def round_(p, carry):
        idx_prev, pf = carry
        parts = [neg, neg, neg, neg]
        for k in range(nk):
            rows = slice(8 * k, 8 * k + 8)
            hit = idx_prev == float(k)
            w = jnp.where(hit, -jnp.inf, work_ref[rows, :])
            work_ref[rows, :] = w
            if track_rank:
                rank_ref[rows, :] = jnp.where(hit, pf - 1.0, rank_ref[rows, :])
            parts[k % 4] = jnp.maximum(parts[k % 4], w)
        m = jnp.maximum(jnp.maximum(parts[0], parts[1]), jnp.maximum(parts[2], parts[3]))
        big = jnp.full((8, tm), float(nk), F32)
        iparts = [big, big, big, big]
        for k in range(nk):
            w = work_ref[8 * k:8 * k + 8, :]
            iparts[k % 4] = jnp.minimum(iparts[k % 4], jnp.where(w == m, float(k), float(nk)))
        idx = jnp.minimum(jnp.minimum(iparts[0], iparts[1]), jnp.minimum(iparts[2], iparts[3]))
        val_ref[p] = m
        idx_ref[p] = idx
        return idx, pf + 1.0

    idx_last, _ = lax.fori_loop(0, PEER_TOPK, round_,
                                (jnp.full((8, tm), -1.0, F32), jnp.zeros((8, tm), F32)))
    if track_rank:
        for k in range(nk):
            rows = slice(8 * k, 8 * k + 8)
            rank_ref[rows, :] = jnp.where(idx_last == float(k), float(PEER_TOPK - 1), rank_ref[rows, :])


def _peer_select_kernel(qp_ref, k1_ref, k2_ref, k2h_ref, perm_ref,
                        r2_ref, e2_ref, n1_ref, w1_ref,
                        work_ref, rank_ref, a_ref, ia_ref, b_ref, ib_ref):
    nk = PEER_NKEYS
    qp = qp_ref[...]
    tm = qp.shape[0]
    s2h = lax.dot_general(k2h_ref[...], qp, NT_DIMS, preferred_element_type=F32)
    for h in range(PEER_HEADS):
        blk = s2h[h * nk:(h + 1) * nk, :]
        e2_ref[h * nk:(h + 1) * nk, :] = jnp.exp(blk - jnp.max(blk, axis=0, keepdims=True))

    work_ref[...] = lax.dot_general(k1_ref[...], qp, NT_DIMS, preferred_element_type=F32)
    _extract_top(work_ref, rank_ref, a_ref, ia_ref, track_rank=False)
    work_ref[...] = lax.dot_general(k2_ref[...], qp, NT_DIMS, preferred_element_type=F32)
    rank_ref[...] = jnp.full(rank_ref.shape, float(PEER_TOPK), F32)
    _extract_top(work_ref, rank_ref, b_ref, ib_ref, track_rank=True)
    r2_ref[...] = jnp.dot(perm_ref[...], rank_ref[...].astype(BF16), preferred_element_type=F32)

    a = [a_ref[p] for p in range(PEER_TOPK)]
    b = [b_ref[q] for q in range(PEER_TOPK)]
    cand = [a[p] + b[q] for (p, q) in _PAIRS]
    npair = len(_PAIRS)
    beaten = [jnp.zeros((8, tm), F32) for _ in range(npair)]
    for x in range(npair):
        px, qx = _PAIRS[x]
        for y in range(x + 1, npair):
            py, qy = _PAIRS[y]
            if px <= py and qx <= qy:
                beaten[y] = beaten[y] + 1.0
            else:
                gt = jnp.where(cand[y] > cand[x], 1.0, 0.0)
                beaten[x] = beaten[x] + gt
                beaten[y] = beaten[y] + (1.0 - gt)
    sel = [jnp.where(bt < float(PEER_TOPK), 1.0, 0.0) for bt in beaten]
    ea = [jnp.exp(a[p] - a[0]) for p in range(PEER_TOPK)]
    eb = [jnp.exp(b[q] - b[0]) for q in range(PEER_TOPK)]
    z = jnp.zeros((8, tm), F32)
    cnt = [jnp.zeros((8, tm), F32) for _ in range(PEER_TOPK)]
    for x, (p, q) in enumerate(_PAIRS):
        z = z + sel[x] * (ea[p] * eb[q])
        cnt[p] = cnt[p] + sel[x]
    inv_z = 1.0 / z
    w1 = [ea[p] * inv_z for p in range(PEER_TOPK)]
    ia = [ia_ref[p] for p in range(PEER_TOPK)]
    zero = jnp.zeros((8, tm), F32)
    for k in range(nk):
        n1k = zero
        w1k = zero
        for p in range(PEER_TOPK):
            hit = ia[p] == float(k)
            n1k = jnp.where(hit, cnt[p], n1k)
            w1k = jnp.where(hit, w1[p], w1k)
        n1_ref[8 * k:8 * k + 8, :] = n1k
        w1_ref[8 * k:8 * k + 8, :] = w1k


def _peer_select(qp, k1, k2, k2h, perm):
    t, d = qp.shape
    tm = _pick(t, (256, 128))
    rows = PEER_NKEYS * PEER_HEADS
    full = lambda i: (0, 0)
    out = jax.ShapeDtypeStruct((rows, t), F32)
    ospec = pl.BlockSpec((rows, tm), lambda i: (0, i))
    return pl.pallas_call(
        _peer_select_kernel,
        grid=(t // tm,),
        in_specs=[pl.BlockSpec((tm, d), lambda i: (i, 0)),
                  pl.BlockSpec((rows, d), full),
                  pl.BlockSpec((rows, d), full),
                  pl.BlockSpec((rows, d), full),
                  pl.BlockSpec((rows, rows), full)],
        out_specs=[ospec, ospec, ospec, ospec],
        out_shape=[out, out, out, out],
        scratch_shapes=[pltpu.VMEM((rows, tm), F32), pltpu.VMEM((rows, tm), F32),
                        pltpu.VMEM((PEER_TOPK, 8, tm), F32), pltpu.VMEM((PEER_TOPK, 8, tm), F32),
                        pltpu.VMEM((PEER_TOPK, 8, tm), F32), pltpu.VMEM((PEER_TOPK, 8, tm), F32)],
        compiler_params=_params("parallel"),
        name="peer_select",
    )(qp, k1, k2, k2h, perm)


def _gelu_tanh(x):
    c = math.sqrt(2.0 / math.pi)
    return 0.5 * x * (1.0 + jnp.tanh(c * (x + 0.044715 * (x * x * x))))


def _peer_main_kernel(hn_ref, u_ref, va_ref, vb_ref, r2_ref, e2_ref, n1_ref, w1_ref,
                      o_ref, ga_ref, gb_ref, pa_ref, pb_ref, act_ref, *, tc):
    nk = PEER_NKEYS
    te, tm = ga_ref.shape
    j = pl.program_id(1)
    last = pl.num_programs(1) - 1
    hn = hn_ref[...]

    nw = 256
    d = o_ref.shape[1]

    n_il = te // nk
    jr = 64

    def gate_tile(g_ref, nref, wref, half, c, jh):
        cols = slice(c * tc, (c + 1) * tc)
        accs = [jnp.zeros((jr, tc), F32) for _ in range(n_il)]
        for h in range(PEER_HEADS):
            hr = slice(h * nk + jh * jr, h * nk + (jh + 1) * jr)
            r2 = r2_ref[hr, cols]
            e2 = e2_ref[hr, cols]
            for il in range(n_il):
                r = (half * n_il + il) * PEER_HEADS + h
                keep = r2 < nref[r:r + 1, cols]
                accs[il] = accs[il] + jnp.where(keep, e2, 0.0) * wref[r:r + 1, cols]
        for il in range(n_il):
            g_ref[il * nk + jh * jr:il * nk + (jh + 1) * jr, cols] = accs[il]

    def gate(g_ref, nref, wref, half):
        for c in range(tm // tc):
            for jh in range(nk // jr):
                gate_tile(g_ref, nref, wref, half, c, jh)

    @pl.when(j == 0)
    def _():
        o_ref[...] = jnp.zeros(o_ref.shape, o_ref.dtype)
        pb_ref[...] = jnp.zeros(pb_ref.shape, pb_ref.dtype)

    def act_piece(half, n):
        u = u_ref[half * te + n * nw:half * te + (n + 1) * nw, :]
        return _gelu_tanh(lax.dot_general(hn, u, NT_DIMS, preferred_element_type=F32))

    def out_piece(p_ref, v_ref, n):
        cols = slice(n * nw, (n + 1) * nw)
        o_ref[:, cols] += jnp.dot(p_ref[...], v_ref[:, cols], preferred_element_type=F32)

    def sub_block(half, g_ref, p_new_ref, p_old_ref, v_old_ref, next_gate):
        for n in range(te // nw):
            act_ref[:, n * nw:(n + 1) * nw] = act_piece(half, n)
        for n in range(d // nw):
            out_piece(p_old_ref, v_old_ref, n)
        gate(*next_gate)
        for il in range(n_il):
            ecols = slice(il * nk, (il + 1) * nk)
            for c in range(tm // tc):
                rows = slice(c * tc, (c + 1) * tc)
                g = g_ref[ecols, rows].T
                p_new_ref[rows, ecols] = (g * act_ref[rows, ecols]).astype(p_new_ref.dtype)

    @pl.when(j < last)
    def _():
        sub_block(0, ga_ref, pa_ref, pb_ref, vb_ref, (ga_ref, n1_ref, w1_ref, 0))
        sub_block(1, gb_ref, pb_ref, pa_ref, va_ref, (gb_ref, n1_ref, w1_ref, 1))

    @pl.when(j == last)
    def _():
        for n in range(d // nw):
            out_piece(pb_ref, vb_ref, n)


def _peer_main(hn, u, v, r2, e2, n1, w1):
    t, d = hn.shape
    e = u.shape[0]
    tm = _pick(t, (640, 512, 256, 128))
    te = 512
    tc = 128
    rows = PEER_NKEYS * PEER_HEADS
    sub = 2 * te // PEER_NKEYS * PEER_HEADS
    nj = e // (2 * te)
    cur = lambda j: jnp.minimum(j, nj - 1)
    return pl.pallas_call(
        functools.partial(_peer_main_kernel, tc=tc),
        grid=(t // tm, nj + 1),
        in_specs=[pl.BlockSpec((tm, d), lambda i, j: (i, 0)),
                  pl.BlockSpec((2 * te, d), lambda i, j: (cur(j), 0)),
                  pl.BlockSpec((te, d), lambda i, j: (2 * cur(j), 0)),
                  pl.BlockSpec((te, d), lambda i, j: (jnp.maximum(2 * j - 1, 0), 0)),
                  pl.BlockSpec((rows, tm), lambda i, j: (0, i)),
                  pl.BlockSpec((rows, tm), lambda i, j: (0, i)),
                  pl.BlockSpec((sub, tm), lambda i, j: (cur(j), i)),
                  pl.BlockSpec((sub, tm), lambda i, j: (cur(j), i))],
        out_specs=pl.BlockSpec((tm, d), lambda i, j: (i, 0)),
        out_shape=jax.ShapeDtypeStruct((t, d), F32),
        scratch_shapes=[pltpu.VMEM((te, tm), F32), pltpu.VMEM((te, tm), F32),
                        pltpu.VMEM((tm, te), BF16), pltpu.VMEM((tm, te), BF16),
                        pltpu.VMEM((tm, te), F32)],
        compiler_params=_params("parallel", "arbitrary"),
        name="peer_main",
    )(hn, u, v, v, r2, e2, n1, w1)


def _ple_kernel(xn_ref, p_ref, wg_ref, wp_ref, h_ref, o_ref):
    gate = jnp.dot(xn_ref[...], wg_ref[...], preferred_element_type=F32)
    pe = jnp.dot(p_ref[...].astype(BF16), wp_ref[...], preferred_element_type=F32)
    o_ref[...] = h_ref[...] + pe * jax.nn.sigmoid(gate)


def _ple(xn, p, wg, wp, h):
    t, d = xn.shape
    pd = p.shape[1]
    tm = _pick(t, (1280, 1024, 512, 256, 128))
    tn = _pick(d, (512, 256, 128))
    return pl.pallas_call(
        _ple_kernel,
        grid=(t // tm, d // tn),
        in_specs=[pl.BlockSpec((tm, d), lambda i, j: (i, 0)),
                  pl.BlockSpec((tm, pd), lambda i, j: (i, 0)),
                  pl.BlockSpec((d, tn), lambda i, j: (0, j)),
                  pl.BlockSpec((pd, tn), lambda i, j: (0, j)),
                  pl.BlockSpec((tm, tn), lambda i, j: (i, j))],
        out_specs=pl.BlockSpec((tm, tn), lambda i, j: (i, j)),
        out_shape=jax.ShapeDtypeStruct((t, d), F32),
        compiler_params=_params("parallel", "arbitrary"),
        name="ple",
    )(xn, p, wg, wp, h)


def _swap_halves(w):
    half = w.shape[-1] // 2
    return jnp.concatenate([w[..., half:], w[..., :half]], axis=-1)


def _split_w_in(w_in):
    d = w_in.shape[0]
    splits = [int(s) for s in np.cumsum(IN_SIZES)[:-1]]
    wb, wc, wx, wq, wckv, wkr, wga, wgb = jnp.split(w_in, splits, axis=1)
    wq = wq.reshape(d, N_HEADS, QK_DIM)
    wq_n = wq[:, :, :QK_NOPE].reshape(d, N_HEADS * QK_NOPE)
    wq_r = wq[:, :, QK_NOPE:]
    w_big = jnp.concatenate(
        [wb, wc, wx, wq_n, wq_r.reshape(d, -1), _swap_halves(wq_r).reshape(d, -1), wga, wgb], axis=1)
    w_small = jnp.concatenate([wckv, wkr, _swap_halves(wkr)], axis=1)
    return w_big.astype(BF16), w_small.astype(BF16)


def _rope_tables(pos):
    inv = 1.0 / (ROPE_THETA ** (jnp.arange(0, QK_ROPE, 2, dtype=F32) / QK_ROPE))
    ang = pos.astype(F32)[:, None] * inv[None, :]
    cos, sin = jnp.cos(ang), jnp.sin(ang)
    return jnp.concatenate([cos, cos], axis=1), jnp.concatenate([-sin, sin], axis=1)


def _key_matrices(sub_keys):
    nk, hd, half = PEER_NKEYS, PEER_HEADS, PEER_DKEY // 2
    eye = jnp.eye(hd, dtype=F32)

    def build(c, head_major):
        sel = jnp.zeros((2,), F32).at[c].set(1.0)
        m = sub_keys[c][:, None, None, None, :] * eye[None, :, :, None, None] * sel[None, None, None, :, None]
        if head_major:
            m = jnp.transpose(m, (1, 0, 2, 3, 4))
        return m.reshape(nk * hd, hd * PEER_DKEY).astype(BF16)

    r = np.arange(nk * hd)
    perm = np.zeros((nk * hd, nk * hd), np.float32)
    perm[(r % hd) * nk + r // hd, r] = 1.0
    return build(0, False), build(1, False), build(1, True), jnp.asarray(perm, BF16)


def kernel(x_prompt, x_sample, cache_conv, cache_ckv, cache_krope, p_prompt, p_sample, g_mix, w_in, conv_w, g_kv, w_kv_b, w_a_out, w_b_out, w_o, g_ffn, w_pq, sub_keys, u_tab, v_tab, g_ple, w_ple_gate, w_ple, g_final):
    assert x_prompt.shape[0] == 1 and w_in.shape[0] == 1
    seq = x_prompt.shape[1]
    nb, dseq = x_sample.shape[0], x_sample.shape[1]
    past = cache_ckv.shape[2]
    d = D_MODEL
    ns = nb * dseq
    t = seq + ns

    x_all = jnp.concatenate([x_prompt.reshape(seq, d), x_sample.reshape(ns, d)], axis=0)
    p_all = jnp.concatenate([p_prompt[0].reshape(seq, -1), p_sample[0].reshape(ns, -1)], axis=0)
    pos = jnp.concatenate([jnp.arange(seq), jnp.tile(past + jnp.arange(dseq), nb)])
    cos64, sin64 = _rope_tables(pos)
    w_big, w_small = _split_w_in(w_in[0])

    xn = _rmsnorm(x_all, g_mix[0], BF16)
    z_big = _matmul(xn, w_big, BF16, name="in_proj")
    z_small = _matmul(xn, w_small, F32, tn_prefs=(w_small.shape[1],), name="in_proj_small")
    c_new, kr_new = _post_small(z_small, g_kv[0], cos64, sin64)

    zero_state = jnp.zeros((1, 8, D_CONV), F32)
    ya_p, last_p = _conv_gate(z_big, zero_state, conv_w[0], 0, 1, seq)
    state_s = jnp.pad(cache_conv[0], ((0, 0), (6, 0), (0, 0)))
    ya_s, last_s = _conv_gate(z_big, state_s, conv_w[0], seq, nb, dseq)
    ya_in = jnp.concatenate([ya_p, ya_s], axis=0)
    conv_p = last_p[-1, 6:8, :].reshape(1, 1, 2, D_CONV)
    if dseq >= 2:
        conv_s = last_s[:, 6:8, :].reshape(1, nb, 2, D_CONV)
    else:
        raise NotImplementedError("sample blocks shorter than the convolution state")

    w_kv = w_kv_b[0].astype(BF16)
    q_p = _q_prep(z_big, cos64, sin64, 0, seq)
    q_s = _q_prep(z_big, cos64, sin64, seq, ns)
    tk = _pick(seq, (512, 256, 128))
    w_kv_h = w_kv.reshape(KV_RANK, N_HEADS, QK_NOPE + V_DIM)
    wk = w_kv_h[:, :, :QK_NOPE].reshape(KV_RANK, N_HEADS * QK_NOPE)
    wvt = w_kv_h[:, :, QK_NOPE:].reshape(KV_RANK, N_HEADS * V_DIM).T
    k_p, vt_p = _kv_proj_t(c_new[:seq], wk, wvt, tk)
    o_p = _attention_t(q_p, k_p, kr_new[:seq].astype(BF16), vt_p, seq, tk)
    c_all = jnp.concatenate([cache_ckv[0], c_new[seq:].reshape(nb, dseq, KV_RANK)], axis=1)
    kr_all = jnp.concatenate([cache_krope[0], kr_new[seq:].reshape(nb, dseq, QK_ROPE)], axis=1)
    sk = past + dseq
    kv_s = _kv_proj(c_all.reshape(nb * sk, KV_RANK), w_kv)
    o_s = _attention(q_s, kv_s, kr_all.reshape(nb * sk, QK_ROPE).astype(BF16), nb, dseq, sk, past, False)
    o_all = jnp.concatenate([o_p, o_s], axis=0)

    m = _merge(ya_in, o_all, w_a_out[0].astype(BF16), w_b_out[0].astype(BF16), z_big)
    h1 = _matmul(m, w_o[0].astype(BF16), F32, residual=x_all, name="out_proj")

    hn = _rmsnorm(h1, g_ffn[0], BF16)
    qp = _matmul(hn, w_pq[0].astype(BF16), BF16, name="peer_query")
    k1, k2, k2h, perm = _key_matrices(sub_keys[0])
    r2, e2, n1, w1 = _peer_select(qp, k1, k2, k2h, perm)
    peer = _peer_main(hn, u_tab[0].astype(BF16), v_tab[0].astype(BF16), r2, e2, n1, w1)

    h2, x2n = _add_rmsnorm(h1, peer, g_ple[0], BF16)
    h3 = _ple(x2n, p_all, w_ple_gate[0].astype(BF16), w_ple[0].astype(BF16), h2)
    y = _rmsnorm(h3, g_final, F32)

    y_prompt = y[:seq].reshape(1, seq, d)
    y_sample = y[seq:].reshape(nb, dseq, d)
    ckv_p = c_new[:seq].reshape(1, 1, seq, KV_RANK)
    kr_p = kr_new[:seq].reshape(1, 1, seq, QK_ROPE)
    ckv_s = c_new[seq:].reshape(1, nb, dseq, KV_RANK)
    kr_s = kr_new[seq:].reshape(1, nb, dseq, QK_ROPE)
    return (y_prompt, y_sample, conv_p, ckv_p, kr_p, conv_s, ckv_s, kr_s)
```

```python
import functools
import math

import numpy as np
import jax
import jax.numpy as jnp
from jax import lax
from jax.experimental import pallas as pl
from jax.experimental.pallas import tpu as pltpu

D_MODEL = 2048
D_CONV = 2048
N_HEADS = 16
QK_NOPE = 128
QK_ROPE = 64
QK_DIM = QK_NOPE + QK_ROPE
V_DIM = 128
KV_RANK = 512
CHUNK = 64
ROPE_THETA = 10000.0
PEER_HEADS = 8
PEER_NKEYS = 128
PEER_TOPK = 16
PEER_DKEY = 256
PEER_EXPERTS = PEER_NKEYS * PEER_NKEYS
RMS_EPS = 1e-6
IN_SIZES = (D_CONV, D_CONV, D_CONV, N_HEADS * QK_DIM, KV_RANK, QK_ROPE, D_MODEL, D_MODEL)

BF16 = jnp.bfloat16
F32 = jnp.float32
VMEM_LIMIT_BYTES = 56 * 1024 * 1024
MASK_VALUE = -1e30
NT_DIMS = (((1,), (1,)), ((), ()))
TN_DIMS = (((0,), (0,)), ((), ()))

_PAIRS = tuple((p, q) for p in range(PEER_TOPK) for q in range(PEER_TOPK)
               if (p + 1) * (q + 1) <= PEER_TOPK)


def _pick(n, prefs):
    for p in prefs:
        if n % p == 0:
            return p
    return n


def _params(*sem):
    return pltpu.CompilerParams(dimension_semantics=sem, vmem_limit_bytes=VMEM_LIMIT_BYTES)


def _rms_kernel(x_ref, g_ref, o_ref):
    x = x_ref[...]
    ms = jnp.mean(x * x, axis=-1, keepdims=True)
    o_ref[...] = (x * lax.rsqrt(ms + RMS_EPS) * g_ref[...]).astype(o_ref.dtype)


def _fill_call(kern, *, into, in_specs, args, **kwargs):
    if into is None:
        return pl.pallas_call(kern, in_specs=in_specs, **kwargs)(*args)

    def filling(buf_ref, *refs):
        del buf_ref
        kern(*refs)

    return pl.pallas_call(filling, in_specs=[pl.BlockSpec(memory_space=pl.ANY)] + list(in_specs),
                          input_output_aliases={0: 0}, **kwargs)(into, *args)


def _rmsnorm(x, g, out_dtype, in_row0=0, rows=None, out_row0=0, out_rows=None, into=None):
    d = x.shape[1]
    rows = x.shape[0] - in_row0 if rows is None else rows
    out_rows = rows if out_rows is None else out_rows
    tm = _pick(math.gcd(rows, in_row0, out_row0), (1024, 640, 512, 256, 128))
    bi, bo = in_row0 // tm, out_row0 // tm
    return _fill_call(
        _rms_kernel, into=into,
        grid=(rows // tm,),
        in_specs=[pl.BlockSpec((tm, d), lambda i: (bi + i, 0)),
                  pl.BlockSpec((1, d), lambda i: (0, 0))],
        args=(x, g.reshape(1, d)),
        out_specs=pl.BlockSpec((tm, d), lambda i: (bo + i, 0)),
        out_shape=jax.ShapeDtypeStruct((out_rows, d), out_dtype),
        compiler_params=_params("parallel"),
        name="rmsnorm",
    )


def _add_rms_kernel(a_ref, b_ref, g_ref, s_ref, o_ref):
    x = a_ref[...] + b_ref[...]
    s_ref[...] = x
    ms = jnp.mean(x * x, axis=-1, keepdims=True)
    o_ref[...] = (x * lax.rsqrt(ms + RMS_EPS) * g_ref[...]).astype(o_ref.dtype)


def _add_rmsnorm(a, b, g, out_dtype):
    t, d = a.shape
    tm = _pick(t, (640, 512, 256, 128))
    row = pl.BlockSpec((tm, d), lambda i: (i, 0))
    return pl.pallas_call(
        _add_rms_kernel,
        grid=(t // tm,),
        in_specs=[row, row, pl.BlockSpec((1, d), lambda i: (0, 0))],
        out_specs=[row, row],
        out_shape=[jax.ShapeDtypeStruct((t, d), F32), jax.ShapeDtypeStruct((t, d), out_dtype)],
        compiler_params=_params("parallel"),
        name="add_rmsnorm",
    )(a, b, g.reshape(1, d))


def _mm_kernel(x_ref, w_ref, o_ref):
    x = x_ref[...].astype(BF16)
    o_ref[...] = jnp.dot(x, w_ref[...], preferred_element_type=F32).astype(o_ref.dtype)


def _mm_res_kernel(x_ref, w_ref, r_ref, o_ref):
    x = x_ref[...].astype(BF16)
    o_ref[...] = (r_ref[...] + jnp.dot(x, w_ref[...], preferred_element_type=F32)).astype(o_ref.dtype)


def _matmul(x, w, out_dtype, residual=None, tm_prefs=(1280, 1024, 768, 512, 256, 128),
            tn_prefs=(1024, 512, 256, 128), name="matmul", row0=0, rows=None, into=None):
    t, k = x.shape
    n = w.shape[1]
    rows = t - row0 if rows is None else rows
    tm = _pick(math.gcd(rows, row0), tm_prefs)
    tn = _pick(n, tn_prefs)
    b0 = row0 // tm
    in_specs = [pl.BlockSpec((tm, k), lambda i, j: (b0 + i, 0)),
                pl.BlockSpec((k, tn), lambda i, j: (0, j))]
    args = [x, w]
    kern = _mm_kernel
    if residual is not None:
        in_specs.append(pl.BlockSpec((tm, tn), lambda i, j: (i, j)))
        args.append(residual)
        kern = _mm_res_kernel
    return _fill_call(
        kern, into=into,
        grid=(rows // tm, n // tn),
        in_specs=in_specs,
        args=args,
        out_specs=pl.BlockSpec((tm, tn), lambda i, j: (b0 + i, j)),
        out_shape=jax.ShapeDtypeStruct((t, n), out_dtype),
        compiler_params=_params("parallel", "arbitrary"),
        name=name,
    )


def _kv_proj_t_kernel(c_ref, wk_ref, wvt_ref, k_ref, vt_ref):
    c = c_ref[...].astype(BF16)
    k = jnp.dot(c, wk_ref[...], preferred_element_type=F32)
    vt = lax.dot_general(wvt_ref[...], c, NT_DIMS, preferred_element_type=F32)
    for h in range(N_HEADS):
        k_ref[h] = k[:, h * QK_NOPE:(h + 1) * QK_NOPE].astype(k_ref.dtype)
        vt_ref[h] = vt[h * V_DIM:(h + 1) * V_DIM, :].astype(vt_ref.dtype)


def _kv_proj_t(c, wk, wvt, tk):
    r, kd = c.shape
    return pl.pallas_call(
        _kv_proj_t_kernel,
        grid=(r // tk,),
        in_specs=[pl.BlockSpec((tk, kd), lambda i: (i, 0)),
                  pl.BlockSpec((kd, N_HEADS * QK_NOPE), lambda i: (0, 0)),
                  pl.BlockSpec((N_HEADS * V_DIM, kd), lambda i: (0, 0))],
        out_specs=[pl.BlockSpec((N_HEADS, tk, QK_NOPE), lambda i: (0, i, 0)),
                   pl.BlockSpec((N_HEADS, None, V_DIM, tk), lambda i: (0, i, 0, 0))],
        out_shape=[jax.ShapeDtypeStruct((N_HEADS, r, QK_NOPE), BF16),
                   jax.ShapeDtypeStruct((N_HEADS, r // tk, V_DIM, tk), BF16)],
        compiler_params=_params("parallel"),
        name="kv_proj_t",
    )(c, wk, wvt)


def _post_small_kernel(z_ref, g_ref, cos_ref, sin_ref, c_ref, kr_ref):
    z = z_ref[...]
    ckv = z[:, :KV_RANK]
    ms = jnp.mean(ckv * ckv, axis=-1, keepdims=True)
    c_ref[...] = ckv * lax.rsqrt(ms + RMS_EPS) * g_ref[...]
    kr = z[:, KV_RANK:KV_RANK + QK_ROPE]
    kr_sw = z[:, KV_RANK + QK_ROPE:KV_RANK + 2 * QK_ROPE]
    kr_ref[...] = kr * cos_ref[...] + kr_sw * sin_ref[...]


def _post_small(z_small, g_kv, cos64, sin64, row0, rows):
    w = z_small.shape[1]
    tm = _pick(math.gcd(rows, row0), (1024, 512, 256, 128))
    b0 = row0 // tm
    return pl.pallas_call(
        _post_small_kernel,
        grid=(rows // tm,),
        in_specs=[pl.BlockSpec((tm, w), lambda i: (b0 + i, 0)),
                  pl.BlockSpec((1, KV_RANK), lambda i: (0, 0)),
                  pl.BlockSpec((tm, QK_ROPE), lambda i: (b0 + i, 0)),
                  pl.BlockSpec((tm, QK_ROPE), lambda i: (b0 + i, 0))],
        out_specs=[pl.BlockSpec((tm, KV_RANK), lambda i: (i, 0)),
                   pl.BlockSpec((tm, QK_ROPE), lambda i: (i, 0))],
        out_shape=[jax.ShapeDtypeStruct((rows, KV_RANK), F32),
                   jax.ShapeDtypeStruct((rows, QK_ROPE), F32)],
        compiler_params=_params("parallel"),
        name="latent_post",
    )(z_small, g_kv.reshape(1, KV_RANK), cos64, sin64)


def _q_prep_kernel(qn_ref, qr_ref, qsw_ref, cos_ref, sin_ref, o_ref):
    scale = QK_DIM ** -0.5 * math.log2(math.e)
    cos = jnp.tile(cos_ref[...], (1, N_HEADS))
    sin = jnp.tile(sin_ref[...], (1, N_HEADS))
    qr = (qr_ref[...].astype(F32) * cos + qsw_ref[...].astype(F32) * sin) * scale
    qn = qn_ref[...].astype(F32) * scale
    for h in range(N_HEADS):
        o_ref[h, :, 0:QK_NOPE] = qn[:, h * QK_NOPE:(h + 1) * QK_NOPE].astype(o_ref.dtype)
        o_ref[h, :, QK_NOPE:QK_DIM] = qr[:, h * QK_ROPE:(h + 1) * QK_ROPE].astype(o_ref.dtype)


def _q_prep(z_big, cos64, sin64, row0, rows):
    tm = _pick(math.gcd(rows, row0) if row0 else rows, (256, 128, 32))
    b0 = row0 // tm
    nope_blk = 3 * D_CONV // (N_HEADS * QK_NOPE)
    rope_blk = (3 * D_CONV + N_HEADS * QK_NOPE) // (N_HEADS * QK_ROPE)
    return pl.pallas_call(
        _q_prep_kernel,
        grid=(rows // tm,),
        in_specs=[pl.BlockSpec((tm, N_HEADS * QK_NOPE), lambda i: (b0 + i, nope_blk)),
                  pl.BlockSpec((tm, N_HEADS * QK_ROPE), lambda i: (b0 + i, rope_blk)),
                  pl.BlockSpec((tm, N_HEADS * QK_ROPE), lambda i: (b0 + i, rope_blk + 1)),
                  pl.BlockSpec((tm, QK_ROPE), lambda i: (b0 + i, 0)),
                  pl.BlockSpec((tm, QK_ROPE), lambda i: (b0 + i, 0))],
        out_specs=pl.BlockSpec((N_HEADS, tm, QK_DIM), lambda i: (0, i, 0)),
        out_shape=jax.ShapeDtypeStruct((N_HEADS, rows, QK_DIM), BF16),
        compiler_params=_params("parallel"),
        name="q_prep",
    )(z_big, z_big, z_big, cos64, sin64)


def _conv_kernel(state_ref, pc_ref, px_ref, b_ref, c_ref, x_ref, w_ref, o_ref, last_ref, *, tm, halo):
    i = pl.program_id(1)
    u = c_ref[...].astype(F32) * x_ref[...].astype(F32)
    prev = pc_ref[...].astype(F32) * px_ref[...].astype(F32)
    st = state_ref[...]
    first = i == 0
    um1 = jnp.where(first, st[7:8, :], prev[halo - 1:halo, :])
    um2 = jnp.where(first, st[6:7, :], prev[halo - 2:halo - 1, :])
    row = lax.broadcasted_iota(jnp.int32, u.shape, 0)
    s1 = jnp.where(row == 0, um1, pltpu.roll(u, 1, 0))
    s2 = jnp.where(row == 0, um2, jnp.where(row == 1, um1, pltpu.roll(u, 2, 0)))
    w = w_ref[...]
    y = w[0:1, :] * s2 + w[1:2, :] * s1 + w[2:3, :] * u
    o_ref[...] = (b_ref[...].astype(F32) * y).astype(o_ref.dtype)
    last_ref[...] = u[tm - 8:tm, :]


def _conv_gate(z_big, state, conv_w, row0, nseq, seq_len, into=None):
    tm = _pick(seq_len, (512, 256, 128, 32))
    halo = 16
    nb = seq_len // tm
    b0 = row0 // tm
    h0 = row0 // halo
    per = tm // halo
    d = D_CONV

    def hmap(col):
        return lambda s, i: (jnp.maximum(h0 + (s * nb + i) * per - 1, 0), col)

    def bmap(col):
        return lambda s, i: (b0 + s * nb + i, col)

    return _fill_call(
        functools.partial(_conv_kernel, tm=tm, halo=halo), into=into,
        grid=(nseq, nb),
        in_specs=[pl.BlockSpec((None, 8, d), lambda s, i: (s, 0, 0)),
                  pl.BlockSpec((halo, d), hmap(1)),
                  pl.BlockSpec((halo, d), hmap(2)),
                  pl.BlockSpec((tm, d), bmap(0)),
                  pl.BlockSpec((tm, d), bmap(1)),
                  pl.BlockSpec((tm, d), bmap(2)),
                  pl.BlockSpec((8, d), lambda s, i: (0, 0))],
        args=(state, z_big, z_big, z_big, z_big, z_big, jnp.pad(conv_w, ((0, 5), (0, 0)))),
        out_specs=[pl.BlockSpec((tm, d), bmap(0)),
                   pl.BlockSpec((None, 8, d), lambda s, i: (s * nb + i, 0, 0))],
        out_shape=[jax.ShapeDtypeStruct((z_big.shape[0], d), BF16),
                   jax.ShapeDtypeStruct((nseq * nb, 8, d), F32)],
        compiler_params=_params("parallel", "arbitrary"),
        name="conv_gate",
    )


def _attn_latent_kernel(q_ref, wk_ref, wv_ref, cp_ref, krp_ref, cn_ref, krn_ref, o_ref, qa_ref, acc_ref,
                        *, past, tk):
    nh, dseq, _ = q_ref.shape
    rows = nh * dseq
    for h in range(nh):
        qh = q_ref[h]
        qa = lax.dot_general(qh[:, :QK_NOPE], wk_ref[h], NT_DIMS, preferred_element_type=F32)
        qa_ref[h * dseq:(h + 1) * dseq, 0:KV_RANK] = qa.astype(qa_ref.dtype)
        qa_ref[h * dseq:(h + 1) * dseq, KV_RANK:KV_RANK + QK_ROPE] = qh[:, QK_NOPE:QK_DIM]
    q = qa_ref[...]
    acc_ref[...] = jnp.zeros(acc_ref.shape, F32)

    def update(carry, s, c):
        m, l = carry
        m_new = jnp.maximum(m, jnp.max(s, axis=1, keepdims=True))
        p = jnp.exp2(s - m_new)
        alpha = jnp.exp2(m - m_new)
        acc_ref[...] = alpha * acc_ref[...] + jnp.dot(p.astype(BF16), c, preferred_element_type=F32)
        return m_new, alpha * l + jnp.sum(p, axis=1, keepdims=True)

    def past_step(j, carry):
        ks = pl.ds(pl.multiple_of(j * tk, tk), tk)
        c = cp_ref[ks, :].astype(BF16)
        k = jnp.concatenate([c, krp_ref[ks, :].astype(BF16)], axis=1)
        return update(carry, lax.dot_general(q, k, NT_DIMS, preferred_element_type=F32), c)

    carry = (jnp.full((rows, 1), MASK_VALUE, F32), jnp.zeros((rows, 1), F32))
    carry = lax.fori_loop(0, past // tk, past_step, carry)

    cn = cn_ref[...].astype(BF16)
    kn = jnp.concatenate([cn, krn_ref[...].astype(BF16)], axis=1)
    s = lax.dot_general(q, kn, NT_DIMS, preferred_element_type=F32)
    q_pos = past + lax.broadcasted_iota(jnp.int32, (rows, 1), 0) % dseq
    k_pos = past + lax.broadcasted_iota(jnp.int32, (1, dseq), 1)
    s = jnp.where(k_pos // CHUNK <= q_pos // CHUNK, s, MASK_VALUE)
    m, l = update(carry, s, cn)

    lat = (acc_ref[...] / l).astype(BF16)
    for h in range(nh):
        o = jnp.dot(lat[h * dseq:(h + 1) * dseq, :], wv_ref[h], preferred_element_type=F32)
        o_ref[:, h * V_DIM:(h + 1) * V_DIM] = o.astype(o_ref.dtype)


def _attention_latent(q, wk, wv, c_past, kr_past, c_new, kr_new, row0, into):
    nb, past, _ = c_past.shape
    dseq = c_new.shape[1]
    tk = _pick(past, (512, 256, 128))
    assert row0 % dseq == 0
    b0 = row0 // dseq
    per_b = lambda w: pl.BlockSpec((None,) + w, lambda b: (b, 0, 0))
    whole = lambda a: pl.BlockSpec(a.shape, lambda b: (0, 0, 0))
    return _fill_call(
        functools.partial(_attn_latent_kernel, past=past, tk=tk), into=into,
        grid=(nb,),
        in_specs=[pl.BlockSpec((N_HEADS, dseq, QK_DIM), lambda b: (0, b, 0)),
                  whole(wk), whole(wv),
                  per_b((past, KV_RANK)), per_b((past, QK_ROPE)),
                  per_b((dseq, KV_RANK)), per_b((dseq, QK_ROPE))],
        args=(q, wk, wv, c_past, kr_past, c_new, kr_new),
        out_specs=pl.BlockSpec((dseq, N_HEADS * V_DIM), lambda b: (b0 + b, 0)),
        out_shape=jax.ShapeDtypeStruct(into.shape, into.dtype),
        scratch_shapes=[pltpu.VMEM((N_HEADS * dseq, KV_RANK + QK_ROPE), BF16),
                        pltpu.VMEM((N_HEADS * dseq, KV_RANK), F32)],
        compiler_params=_params("parallel"),
        name="attention_latent",
    )


def _attn_t_kernel(q_ref, k_ref, kr_ref, vt_ref, o_ref, m_ref, l_ref, acc_ref, s0_ref, s1_ref, *, tb):
    i = pl.program_id(1)
    q = q_ref[...]
    q_pos = i * tb + lax.broadcasted_iota(jnp.int32, (1, tb), 1)
    col_end = (q_pos // CHUNK + 1) * CHUNK

    m_ref[...] = jnp.full(m_ref.shape, MASK_VALUE, F32)
    l_ref[...] = jnp.zeros(l_ref.shape, F32)
    acc_ref[...] = jnp.zeros(acc_ref.shape, F32)

    def produce(s_ref, j):
        ks = pl.ds(pl.multiple_of(j * tb, tb), tb)
        k = jnp.concatenate([k_ref[ks, :], kr_ref[ks, :]], axis=1)
        s_ref[...] = lax.dot_general(k, q, NT_DIMS, preferred_element_type=F32)

    def consume(s_ref, j, masked):
        s = s_ref[...]
        if masked:
            k_pos = j * tb + lax.broadcasted_iota(jnp.int32, (tb, 1), 0)
            s = jnp.where(k_pos < col_end, s, MASK_VALUE)
        m_old = m_ref[...]
        m_new = jnp.maximum(m_old, jnp.max(s, axis=0, keepdims=True))
        p = jnp.exp2(s - m_new)
        alpha = jnp.exp2(m_old - m_new)
        m_ref[...] = m_new
        l_ref[...] = alpha * l_ref[...] + jnp.sum(p, axis=0, keepdims=True)
        acc_ref[...] = alpha * acc_ref[...] + jnp.dot(vt_ref[j], p.astype(BF16), preferred_element_type=F32)

    produce(s0_ref, 0)

    bufs = (s0_ref, s1_ref)
    unroll = 4

    def group(g, carry):
        j0 = unroll * g
        for t in range(unroll):
            produce(bufs[(t + 1) % 2], j0 + t + 1)
            consume(bufs[t % 2], j0 + t, False)
        return carry

    lax.fori_loop(0, i // unroll, group, 0)
    j0 = i // unroll * unroll

    for rem in range(unroll):
        @pl.when(i - j0 == rem)
        def _(rem=rem):
            for t in range(rem + 1):
                if t < rem:
                    produce(bufs[(t + 1) % 2], j0 + t + 1)
                consume(bufs[t % 2], j0 + t, t == rem)

    o_ref[...] = (acc_ref[...] / l_ref[...]).T.astype(o_ref.dtype)


def _attention_t(q, k, kr, vt, seq, tk, out_rows):
    tq = tk
    assert tk % CHUNK == 0 and seq % tk == 0
    return pl.pallas_call(
        functools.partial(_attn_t_kernel, tb=tk),
        grid=(N_HEADS, seq // tq),
        in_specs=[pl.BlockSpec((None, tq, QK_DIM), lambda h, i: (h, i, 0)),
                  pl.BlockSpec((None, seq, QK_NOPE), lambda h, i: (h, 0, 0)),
                  pl.BlockSpec((seq, QK_ROPE), lambda h, i: (0, 0)),
                  pl.BlockSpec((None, seq // tk, V_DIM, tk), lambda h, i: (h, 0, 0, 0))],
        out_specs=pl.BlockSpec((tq, V_DIM), lambda h, i: (i, h)),
        out_shape=jax.ShapeDtypeStruct((out_rows, N_HEADS * V_DIM), BF16),
        scratch_shapes=[pltpu.VMEM((1, tq), F32), pltpu.VMEM((1, tq), F32), pltpu.VMEM((V_DIM, tq), F32),
                        pltpu.VMEM((tk, tq), F32), pltpu.VMEM((tk, tq), F32)],
        compiler_params=_params("parallel", "arbitrary"),
        name="attention_t",
    )(q, k, kr, vt)


def _merge_kernel(a_ref, o_ref, wa_ref, wb_ref, ga_ref, gb_ref, m_ref):
    ya = jnp.dot(a_ref[...], wa_ref[...], preferred_element_type=F32)
    yb = jnp.dot(o_ref[...], wb_ref[...], preferred_element_type=F32)
    m = jax.nn.sigmoid(ga_ref[...].astype(F32)) * ya + jax.nn.sigmoid(gb_ref[...].astype(F32)) * yb
    m_ref[...] = m.astype(m_ref.dtype)


def _merge(ya_in, o, wa, wb, z_big):
    t, k = ya_in.shape
    n = wa.shape[1]
    tm = _pick(t, (1280, 1024, 512, 256, 128))
    tn = _pick(n, (512, 256, 128))
    ga_blk = (3 * D_CONV + N_HEADS * (QK_NOPE + 2 * QK_ROPE)) // tn
    gb_blk = ga_blk + D_MODEL // tn
    return pl.pallas_call(
        _merge_kernel,
        grid=(t // tm, n // tn),
        in_specs=[pl.BlockSpec((tm, k), lambda i, j: (i, 0)),
                  pl.BlockSpec((tm, k), lambda i, j: (i, 0)),
                  pl.BlockSpec((k, tn), lambda i, j: (0, j)),
                  pl.BlockSpec((k, tn), lambda i, j: (0, j)),
                  pl.BlockSpec((tm, tn), lambda i, j: (i, ga_blk + j)),
                  pl.BlockSpec((tm, tn), lambda i, j: (i, gb_blk + j))],
        out_specs=pl.BlockSpec((tm, tn), lambda i, j: (i, j)),
        out_shape=jax.ShapeDtypeStruct((t, n), BF16),
        compiler_params=_params("parallel", "arbitrary"),
        name="merge",
    )(ya_in, o, wa, wb, z_big, z_big)


def _extract_top(work_ref, rank_ref, val_ref, idx_ref, *, track_rank):
    nk = PEER_NKEYS
    tm = work_ref.shape[1]
    neg = jnp.full((8, tm), -jnp.inf, F32)

    def round_(p, carry):
        idx_prev, pf = carry
        parts = [neg, neg, neg, neg]
        for k in range(nk):
            rows = slice(8 * k, 8 * k + 8)
            hit = idx_prev == float(k)
            w = jnp.where(hit, -jnp.inf, work_ref[rows, :])
            work_ref[rows, :] = w
            if track_rank:
                rank_ref[rows, :] = jnp.where(hit, pf - 1.0, rank_ref[rows, :])
            parts[k % 4] = jnp.maximum(parts[k % 4], w)
        m = jnp.maximum(jnp.maximum(parts[0], parts[1]), jnp.maximum(parts[2], parts[3]))
        big = jnp.full((8, tm), float(nk), F32)
        iparts = [big, big, big, big]
        for k in range(nk):
            w = work_ref[8 * k:8 * k + 8, :]
            iparts[k % 4] = jnp.minimum(iparts[k % 4], jnp.where(w == m, float(k), float(nk)))
        idx = jnp.minimum(jnp.minimum(iparts[0], iparts[1]), jnp.minimum(iparts[2], iparts[3]))
        val_ref[p] = m
        idx_ref[p] = idx
        return idx, pf + 1.0

    idx_last, _ = lax.fori_loop(0, PEER_TOPK, round_,
                                (jnp.full((8, tm), -1.0, F32), jnp.zeros((8, tm), F32)))
    if track_rank:
        for k in range(nk):
            rows = slice(8 * k, 8 * k + 8)
            rank_ref[rows, :] = jnp.where(idx_last == float(k), float(PEER_TOPK - 1), rank_ref[rows, :])


def _peer_select_kernel(qp_ref, k1_ref, k2_ref, k2h_ref, perm_ref,
                        r2_ref, e2_ref, n1_ref, w1_ref,
                        work_ref, rank_ref, a_ref, ia_ref, b_ref, ib_ref):
    nk = PEER_NKEYS
    qp = qp_ref[...]
    tm = qp.shape[0]
    s2h = lax.dot_general(k2h_ref[...], qp, NT_DIMS, preferred_element_type=F32)
    for h in range(PEER_HEADS):
        blk = s2h[h * nk:(h + 1) * nk, :]
        e2_ref[h * nk:(h + 1) * nk, :] = jnp.exp(blk - jnp.max(blk, axis=0, keepdims=True))

    work_ref[...] = lax.dot_general(k1_ref[...], qp, NT_DIMS, preferred_element_type=F32)
    _extract_top(work_ref, rank_ref, a_ref, ia_ref, track_rank=False)
    work_ref[...] = lax.dot_general(k2_ref[...], qp, NT_DIMS, preferred_element_type=F32)
    rank_ref[...] = jnp.full(rank_ref.shape, float(PEER_TOPK), F32)
    _extract_top(work_ref, rank_ref, b_ref, ib_ref, track_rank=True)
    r2_ref[...] = jnp.dot(perm_ref[...], rank_ref[...].astype(BF16), preferred_element_type=F32)

    a = [a_ref[p] for p in range(PEER_TOPK)]
    b = [b_ref[q] for q in range(PEER_TOPK)]
    cand = [a[p] + b[q] for (p, q) in _PAIRS]
    npair = len(_PAIRS)
    beaten = [jnp.zeros((8, tm), F32) for _ in range(npair)]
    for x in range(npair):
        px, qx = _PAIRS[x]
        for y in range(x + 1, npair):
            py, qy = _PAIRS[y]
            if px <= py and qx <= qy:
                beaten[y] = beaten[y] + 1.0
            else:
                gt = jnp.where(cand[y] > cand[x], 1.0, 0.0)
                beaten[x] = beaten[x] + gt
                beaten[y] = beaten[y] + (1.0 - gt)
    sel = [jnp.where(bt < float(PEER_TOPK), 1.0, 0.0) for bt in beaten]
    ea = [jnp.exp(a[p] - a[0]) for p in range(PEER_TOPK)]
    eb = [jnp.exp(b[q] - b[0]) for q in range(PEER_TOPK)]
    z = jnp.zeros((8, tm), F32)
    cnt = [jnp.zeros((8, tm), F32) for _ in range(PEER_TOPK)]
    for x, (p, q) in enumerate(_PAIRS):
        z = z + sel[x] * (ea[p] * eb[q])
        cnt[p] = cnt[p] + sel[x]
    inv_z = 1.0 / z
    w1 = [ea[p] * inv_z for p in range(PEER_TOPK)]
    ia = [ia_ref[p] for p in range(PEER_TOPK)]
    zero = jnp.zeros((8, tm), F32)
    for k in range(nk):
        n1k = zero
        w1k = zero
        for p in range(PEER_TOPK):
            hit = ia[p] == float(k)
            n1k = jnp.where(hit, cnt[p], n1k)
            w1k = jnp.where(hit, w1[p], w1k)
        n1_ref[8 * k:8 * k + 8, :] = n1k
        w1_ref[8 * k:8 * k + 8, :] = w1k


def _peer_select(qp, k1, k2, k2h, perm):
    t, d = qp.shape
    tm = _pick(t, (256, 128))
    rows = PEER_NKEYS * PEER_HEADS
    full = lambda i: (0, 0)
    out = jax.ShapeDtypeStruct((rows, t), F32)
    ospec = pl.BlockSpec((rows, tm), lambda i: (0, i))
    return pl.pallas_call(
        _peer_select_kernel,
        grid=(t // tm,),
        in_specs=[pl.BlockSpec((tm, d), lambda i: (i, 0)),
                  pl.BlockSpec((rows, d), full),
                  pl.BlockSpec((rows, d), full),
                  pl.BlockSpec((rows, d), full),
                  pl.BlockSpec((rows, rows), full)],
        out_specs=[ospec, ospec, ospec, ospec],
        out_shape=[out, out, out, out],
        scratch_shapes=[pltpu.VMEM((rows, tm), F32), pltpu.VMEM((rows, tm), F32),
                        pltpu.VMEM((PEER_TOPK, 8, tm), F32), pltpu.VMEM((PEER_TOPK, 8, tm), F32),
                        pltpu.VMEM((PEER_TOPK, 8, tm), F32), pltpu.VMEM((PEER_TOPK, 8, tm), F32)],
        compiler_params=_params("parallel"),
        name="peer_select",
    )(qp, k1, k2, k2h, perm)


def _gelu_tanh(x):
    c = math.sqrt(2.0 / math.pi)
    return 0.5 * x * (1.0 + jnp.tanh(c * (x + 0.044715 * (x * x * x))))


def _peer_main_kernel(hn_ref, u_ref, va_ref, vb_ref, r2_ref, e2_ref, n1_ref, w1_ref,
                      o_ref, ga_ref, gb_ref, pa_ref, pb_ref, act_ref, *, tc):
    nk = PEER_NKEYS
    te, tm = ga_ref.shape
    j = pl.program_id(1)
    last = pl.num_programs(1) - 1
    hn = hn_ref[...]

    nw = 256
    d = o_ref.shape[1]

    n_il = te // nk
    jr = 64

    def gate_tile(g_ref, nref, wref, half, c, jh):
        cols = slice(c * tc, (c + 1) * tc)
        accs = [jnp.zeros((jr, tc), F32) for _ in range(n_il)]
        for h in range(PEER_HEADS):
            hr = slice(h * nk + jh * jr, h * nk + (jh + 1) * jr)
            r2 = r2_ref[hr, cols]
            e2 = e2_ref[hr, cols]
            for il in range(n_il):
                r = (half * n_il + il) * PEER_HEADS + h
                keep = r2 < nref[r:r + 1, cols]
                accs[il] = accs[il] + jnp.where(keep, e2, 0.0) * wref[r:r + 1, cols]
        for il in range(n_il):
            g_ref[il * nk + jh * jr:il * nk + (jh + 1) * jr, cols] = accs[il]

    def gate(g_ref, nref, wref, half):
        for c in range(tm // tc):
            for jh in range(nk // jr):
                gate_tile(g_ref, nref, wref, half, c, jh)

    @pl.when(j == 0)
    def _():
        o_ref[...] = jnp.zeros(o_ref.shape, o_ref.dtype)
        pb_ref[...] = jnp.zeros(pb_ref.shape, pb_ref.dtype)

    def act_piece(half, n):
        u = u_ref[half * te + n * nw:half * te + (n + 1) * nw, :]
        return _gelu_tanh(lax.dot_general(hn, u, NT_DIMS, preferred_element_type=F32))

    def out_piece(p_ref, v_ref, n):
        cols = slice(n * nw, (n + 1) * nw)
        o_ref[:, cols] += jnp.dot(p_ref[...], v_ref[:, cols], preferred_element_type=F32)

    def sub_block(half, g_ref, p_new_ref, p_old_ref, v_old_ref, next_gate):
        for n in range(te // nw):
            act_ref[:, n * nw:(n + 1) * nw] = act_piece(half, n)
        for n in range(d // nw):
            out_piece(p_old_ref, v_old_ref, n)
        gate(*next_gate)
        for il in range(n_il):
            ecols = slice(il * nk, (il + 1) * nk)
            for c in range(tm // tc):
                rows = slice(c * tc, (c + 1) * tc)
                g = g_ref[ecols, rows].T
                p_new_ref[rows, ecols] = (g * act_ref[rows, ecols]).astype(p_new_ref.dtype)

    @pl.when(j < last)
    def _():
        sub_block(0, ga_ref, pa_ref, pb_ref, vb_ref, (ga_ref, n1_ref, w1_ref, 0))
        sub_block(1, gb_ref, pb_ref, pa_ref, va_ref, (gb_ref, n1_ref, w1_ref, 1))

    @pl.when(j == last)
    def _():
        for n in range(d // nw):
            out_piece(pb_ref, vb_ref, n)


def _peer_main(hn, u, v, r2, e2, n1, w1):
    t, d = hn.shape
    e = u.shape[0]
    tm = _pick(t, (640, 512, 256, 128))
    te = 512
    tc = 128
    rows = PEER_NKEYS * PEER_HEADS
    sub = 2 * te // PEER_NKEYS * PEER_HEADS
    nj = e // (2 * te)
    cur = lambda j: jnp.minimum(j, nj - 1)
    return pl.pallas_call(
        functools.partial(_peer_main_kernel, tc=tc),
        grid=(t // tm, nj + 1),
        in_specs=[pl.BlockSpec((tm, d), lambda i, j: (i, 0)),
                  pl.BlockSpec((2 * te, d), lambda i, j: (cur(j), 0)),
                  pl.BlockSpec((te, d), lambda i, j: (2 * cur(j), 0)),
                  pl.BlockSpec((te, d), lambda i, j: (jnp.maximum(2 * j - 1, 0), 0)),
                  pl.BlockSpec((rows, tm), lambda i, j: (0, i)),
                  pl.BlockSpec((rows, tm), lambda i, j: (0, i)),
                  pl.BlockSpec((sub, tm), lambda i, j: (cur(j), i)),
                  pl.BlockSpec((sub, tm), lambda i, j: (cur(j), i))],
        out_specs=pl.BlockSpec((tm, d), lambda i, j: (i, 0)),
        out_shape=jax.ShapeDtypeStruct((t, d), F32),
        scratch_shapes=[pltpu.VMEM((te, tm), F32), pltpu.VMEM((te, tm), F32),
                        pltpu.VMEM((tm, te), BF16), pltpu.VMEM((tm, te), BF16),
                        pltpu.VMEM((tm, te), F32)],
        compiler_params=_params("parallel", "arbitrary"),
        name="peer_main",
    )(hn, u, v, v, r2, e2, n1, w1)


def _ple_kernel(xn_ref, p_ref, wg_ref, wp_ref, h_ref, o_ref):
    gate = jnp.dot(xn_ref[...], wg_ref[...], preferred_element_type=F32)
    pe = jnp.dot(p_ref[...].astype(BF16), wp_ref[...], preferred_element_type=F32)
    o_ref[...] = h_ref[...] + pe * jax.nn.sigmoid(gate)


def _ple(xn, p, wg, wp, h, row0, into=None):
    t, d = xn.shape
    rows, pd = p.shape
    tm = _pick(math.gcd(rows, row0), (1024, 512, 256, 128))
    tn = _pick(d, (512, 256, 128))
    b0 = row0 // tm
    return _fill_call(
        _ple_kernel, into=into,
        grid=(rows // tm, d // tn),
        in_specs=[pl.BlockSpec((tm, d), lambda i, j: (b0 + i, 0)),
                  pl.BlockSpec((tm, pd), lambda i, j: (i, 0)),
                  pl.BlockSpec((d, tn), lambda i, j: (0, j)),
                  pl.BlockSpec((pd, tn), lambda i, j: (0, j)),
                  pl.BlockSpec((tm, tn), lambda i, j: (b0 + i, j))],
        args=(xn, p, wg, wp, h),
        out_specs=pl.BlockSpec((tm, tn), lambda i, j: (b0 + i, j)),
        out_shape=jax.ShapeDtypeStruct((t, d), F32),
        compiler_params=_params("parallel", "arbitrary"),
        name="ple",
    )


def _swap_halves(w):
    half = w.shape[-1] // 2
    return jnp.concatenate([w[..., half:], w[..., :half]], axis=-1)


def _split_w_in(w_in):
    d = w_in.shape[0]
    splits = [int(s) for s in np.cumsum(IN_SIZES)[:-1]]
    wb, wc, wx, wq, wckv, wkr, wga, wgb = jnp.split(w_in, splits, axis=1)
    wq = wq.reshape(d, N_HEADS, QK_DIM)
    wq_n = wq[:, :, :QK_NOPE].reshape(d, N_HEADS * QK_NOPE)
    wq_r = wq[:, :, QK_NOPE:]
    w_big = jnp.concatenate(
        [wb, wc, wx, wq_n, wq_r.reshape(d, -1), _swap_halves(wq_r).reshape(d, -1), wga, wgb], axis=1)
    w_small = jnp.concatenate([wckv, wkr, _swap_halves(wkr)], axis=1)
    return w_big.astype(BF16), w_small.astype(BF16)


def _rope_tables(pos):
    inv = 1.0 / (ROPE_THETA ** (jnp.arange(0, QK_ROPE, 2, dtype=F32) / QK_ROPE))
    ang = pos.astype(F32)[:, None] * inv[None, :]
    cos, sin = jnp.cos(ang), jnp.sin(ang)
    return jnp.concatenate([cos, cos], axis=1), jnp.concatenate([-sin, sin], axis=1)


def _key_matrices(sub_keys):
    nk, hd, half = PEER_NKEYS, PEER_HEADS, PEER_DKEY // 2
    eye = jnp.eye(hd, dtype=F32)

    def build(c, head_major):
        sel = jnp.zeros((2,), F32).at[c].set(1.0)
        m = sub_keys[c][:, None, None, None, :] * eye[None, :, :, None, None] * sel[None, None, None, :, None]
        if head_major:
            m = jnp.transpose(m, (1, 0, 2, 3, 4))
        return m.reshape(nk * hd, hd * PEER_DKEY).astype(BF16)

    r = np.arange(nk * hd)
    perm = np.zeros((nk * hd, nk * hd), np.float32)
    perm[(r % hd) * nk + r // hd, r] = 1.0
    return build(0, False), build(1, False), build(1, True), jnp.asarray(perm, BF16)


def kernel(x_prompt, x_sample, cache_conv, cache_ckv, cache_krope, p_prompt, p_sample, g_mix, w_in, conv_w, g_kv, w_kv_b, w_a_out, w_b_out, w_o, g_ffn, w_pq, sub_keys, u_tab, v_tab, g_ple, w_ple_gate, w_ple, g_final):
    assert x_prompt.shape[0] == 1 and w_in.shape[0] == 1
    seq = x_prompt.shape[1]
    nb, dseq = x_sample.shape[0], x_sample.shape[1]
    past = cache_ckv.shape[2]
    d = D_MODEL
    ns = nb * dseq
    t = seq + ns

    x_p, x_s = x_prompt.reshape(seq, d), x_sample.reshape(ns, d)
    p_p, p_s = p_prompt[0].reshape(seq, -1), p_sample[0].reshape(ns, -1)
    pos = jnp.concatenate([jnp.arange(seq), jnp.tile(past + jnp.arange(dseq), nb)])
    cos64, sin64 = _rope_tables(pos)
    w_big, w_small = _split_w_in(w_in[0])

    xn = _rmsnorm(x_p, g_mix[0], BF16, out_rows=t)
    xn = _rmsnorm(x_s, g_mix[0], BF16, out_row0=seq, out_rows=t, into=xn)
    z_big = _matmul(xn, w_big, BF16, name="in_proj")
    z_small = _matmul(xn, w_small, F32, tn_prefs=(w_small.shape[1],), name="in_proj_small")
    c_p, kr_p = _post_small(z_small, g_kv[0], cos64, sin64, 0, seq)
    c_s, kr_s = _post_small(z_small, g_kv[0], cos64, sin64, seq, ns)

    zero_state = jnp.zeros((1, 8, D_CONV), F32)
    ya_in, last_p = _conv_gate(z_big, zero_state, conv_w[0], 0, 1, seq)
    state_s = jnp.pad(cache_conv[0], ((0, 0), (6, 0), (0, 0)))
    ya_in, last_s = _conv_gate(z_big, state_s, conv_w[0], seq, nb, dseq, into=ya_in)
    conv_p = last_p[-1, 6:8, :].reshape(1, 1, 2, D_CONV)
    if dseq >= 2:
        conv_s = last_s[:, 6:8, :].reshape(1, nb, 2, D_CONV)
    else:
        raise NotImplementedError("sample blocks shorter than the convolution state")

    w_kv_h = w_kv_b[0].astype(BF16).reshape(KV_RANK, N_HEADS, QK_NOPE + V_DIM)
    wk_h = jnp.transpose(w_kv_h[:, :, :QK_NOPE], (1, 0, 2))
    wv_h = jnp.transpose(w_kv_h[:, :, QK_NOPE:], (1, 0, 2))
    wk = w_kv_h[:, :, :QK_NOPE].reshape(KV_RANK, N_HEADS * QK_NOPE)
    wvt = w_kv_h[:, :, QK_NOPE:].reshape(KV_RANK, N_HEADS * V_DIM).T
    q_p = _q_prep(z_big, cos64, sin64, 0, seq)
    q_s = _q_prep(z_big, cos64, sin64, seq, ns)
    tk = _pick(seq, (512, 256, 128))
    k_p, vt_p = _kv_proj_t(c_p, wk, wvt, tk)
    o_all = _attention_t(q_p, k_p, kr_p.astype(BF16), vt_p, seq, tk, t)
    o_all = _attention_latent(q_s, wk_h, wv_h, cache_ckv[0], cache_krope[0],
                              c_s.reshape(nb, dseq, KV_RANK), kr_s.reshape(nb, dseq, QK_ROPE), seq, o_all)

    m = _merge(ya_in, o_all, w_a_out[0].astype(BF16), w_b_out[0].astype(BF16), z_big)
    w_out = w_o[0].astype(BF16)
    h1 = _matmul(m, w_out, F32, residual=x_p, name="out_proj", rows=seq)
    h1 = _matmul(m, w_out, F32, residual=x_s, name="out_proj", row0=seq, into=h1)

    hn = _rmsnorm(h1, g_ffn[0], BF16)
    qp = _matmul(hn, w_pq[0].astype(BF16), BF16, name="peer_query")
    k1, k2, k2h, perm = _key_matrices(sub_keys[0])
    r2, e2, n1, w1 = _peer_select(qp, k1, k2, k2h, perm)
    peer = _peer_main(hn, u_tab[0].astype(BF16), v_tab[0].astype(BF16), r2, e2, n1, w1)

    h2, x2n = _add_rmsnorm(h1, peer, g_ple[0], BF16)
    wg, wp = w_ple_gate[0].astype(BF16), w_ple[0].astype(BF16)
    h3 = _ple(x2n, p_p, wg, wp, h2, 0)
    h3 = _ple(x2n, p_s, wg, wp, h2, seq, into=h3)
    y_prompt = _rmsnorm(h3, g_final, F32, rows=seq).reshape(1, seq, d)
    y_sample = _rmsnorm(h3, g_final, F32, in_row0=seq).reshape(nb, dseq, d)

    ckv_p = c_p.reshape(1, 1, seq, KV_RANK)
    kr_p = kr_p.reshape(1, 1, seq, QK_ROPE)
    ckv_s = c_s.reshape(1, nb, dseq, KV_RANK)
    kr_s = kr_s.reshape(1, nb, dseq, QK_ROPE)
    return (y_prompt, y_sample, conv_p, ckv_p, kr_p, conv_s, ckv_s, kr_s)
```

```python
import functools
import math

import numpy as np
import jax
import jax.numpy as jnp
from jax import lax
from jax.experimental import pallas as pl
from jax.experimental.pallas import tpu as pltpu

D_MODEL = 2048
D_CONV = 2048
N_HEADS = 16
QK_NOPE = 128
QK_ROPE = 64
QK_DIM = QK_NOPE + QK_ROPE
V_DIM = 128
KV_RANK = 512
CHUNK = 64
ROPE_THETA = 10000.0
PEER_HEADS = 8
PEER_NKEYS = 128
PEER_TOPK = 16
PEER_DKEY = 256
PEER_EXPERTS = PEER_NKEYS * PEER_NKEYS
RMS_EPS = 1e-6
IN_SIZES = (D_CONV, D_CONV, D_CONV, N_HEADS * QK_DIM, KV_RANK, QK_ROPE, D_MODEL, D_MODEL)

BF16 = jnp.bfloat16
F32 = jnp.float32
VMEM_LIMIT_BYTES = 56 * 1024 * 1024
MASK_VALUE = -1e30
NT_DIMS = (((1,), (1,)), ((), ()))
TN_DIMS = (((0,), (0,)), ((), ()))

_PAIRS = tuple((p, q) for p in range(PEER_TOPK) for q in range(PEER_TOPK)
               if (p + 1) * (q + 1) <= PEER_TOPK)


def _pick(n, prefs):
    for p in prefs:
        if n % p == 0:
            return p
    return n


def _params(*sem):
    return pltpu.CompilerParams(dimension_semantics=sem, vmem_limit_bytes=VMEM_LIMIT_BYTES)


def _rms_kernel(x_ref, g_ref, o_ref):
    x = x_ref[...]
    ms = jnp.mean(x * x, axis=-1, keepdims=True)
    o_ref[...] = (x * lax.rsqrt(ms + RMS_EPS) * g_ref[...]).astype(o_ref.dtype)


def _fill_call(kern, *, into, in_specs, args, **kwargs):
    if into is None:
        return pl.pallas_call(kern, in_specs=in_specs, **kwargs)(*args)

    def filling(buf_ref, *refs):
        del buf_ref
        kern(*refs)

    return pl.pallas_call(filling, in_specs=[pl.BlockSpec(memory_space=pl.ANY)] + list(in_specs),
                          input_output_aliases={0: 0}, **kwargs)(into, *args)


def _rmsnorm(x, g, out_dtype, in_row0=0, rows=None, out_row0=0, out_rows=None, into=None):
    d = x.shape[1]
    rows = x.shape[0] - in_row0 if rows is None else rows
    out_rows = rows if out_rows is None else out_rows
    tm = _pick(math.gcd(rows, in_row0, out_row0), (1024, 640, 512, 256, 128))
    bi, bo = in_row0 // tm, out_row0 // tm
    return _fill_call(
        _rms_kernel, into=into,
        grid=(rows // tm,),
        in_specs=[pl.BlockSpec((tm, d), lambda i: (bi + i, 0)),
                  pl.BlockSpec((1, d), lambda i: (0, 0))],
        args=(x, g.reshape(1, d)),
        out_specs=pl.BlockSpec((tm, d), lambda i: (bo + i, 0)),
        out_shape=jax.ShapeDtypeStruct((out_rows, d), out_dtype),
        compiler_params=_params("parallel"),
        name="rmsnorm",
    )


def _add_rms_kernel(a_ref, b_ref, g_ref, s_ref, o_ref):
    x = a_ref[...] + b_ref[...]
    s_ref[...] = x
    ms = jnp.mean(x * x, axis=-1, keepdims=True)
    o_ref[...] = (x * lax.rsqrt(ms + RMS_EPS) * g_ref[...]).astype(o_ref.dtype)


def _add_rmsnorm(a, b, g, out_dtype):
    t, d = a.shape
    tm = _pick(t, (640, 512, 256, 128))
    row = pl.BlockSpec((tm, d), lambda i: (i, 0))
    return pl.pallas_call(
        _add_rms_kernel,
        grid=(t // tm,),
        in_specs=[row, row, pl.BlockSpec((1, d), lambda i: (0, 0))],
        out_specs=[row, row],
        out_shape=[jax.ShapeDtypeStruct((t, d), F32), jax.ShapeDtypeStruct((t, d), out_dtype)],
        compiler_params=_params("parallel"),
        name="add_rmsnorm",
    )(a, b, g.reshape(1, d))


def _mm_kernel(x_ref, w_ref, o_ref):
    x = x_ref[...].astype(BF16)
    o_ref[...] = jnp.dot(x, w_ref[...], preferred_element_type=F32).astype(o_ref.dtype)


def _mm_res_kernel(x_ref, w_ref, r_ref, o_ref):
    x = x_ref[...].astype(BF16)
    o_ref[...] = (r_ref[...] + jnp.dot(x, w_ref[...], preferred_element_type=F32)).astype(o_ref.dtype)


def _matmul(x, w, out_dtype, residual=None, tm_prefs=(1280, 1024, 768, 512, 256, 128),
            tn_prefs=(1024, 512, 256, 128), name="matmul", row0=0, rows=None, into=None):
    t, k = x.shape
    n = w.shape[1]
    rows = t - row0 if rows is None else rows
    tm = _pick(math.gcd(rows, row0), tm_prefs)
    tn = _pick(n, tn_prefs)
    b0 = row0 // tm
    in_specs = [pl.BlockSpec((tm, k), lambda i, j: (b0 + i, 0)),
                pl.BlockSpec((k, tn), lambda i, j: (0, j))]
    args = [x, w]
    kern = _mm_kernel
    if residual is not None:
        in_specs.append(pl.BlockSpec((tm, tn), lambda i, j: (i, j)))
        args.append(residual)
        kern = _mm_res_kernel
    return _fill_call(
        kern, into=into,
        grid=(rows // tm, n // tn),
        in_specs=in_specs,
        args=args,
        out_specs=pl.BlockSpec((tm, tn), lambda i, j: (b0 + i, j)),
        out_shape=jax.ShapeDtypeStruct((t, n), out_dtype),
        compiler_params=_params("parallel", "arbitrary"),
        name=name,
    )


def _kv_proj_t_kernel(c_ref, wk_ref, wvt_ref, k_ref, vt_ref):
    c = c_ref[...].astype(BF16)
    k = jnp.dot(c, wk_ref[...], preferred_element_type=F32)
    vt = lax.dot_general(wvt_ref[...], c, NT_DIMS, preferred_element_type=F32)
    for h in range(N_HEADS):
        k_ref[h] = k[:, h * QK_NOPE:(h + 1) * QK_NOPE].astype(k_ref.dtype)
        vt_ref[h] = vt[h * V_DIM:(h + 1) * V_DIM, :].astype(vt_ref.dtype)


def _kv_proj_t(c, wk, wvt, tk):
    r, kd = c.shape
    return pl.pallas_call(
        _kv_proj_t_kernel,
        grid=(r // tk,),
        in_specs=[pl.BlockSpec((tk, kd), lambda i: (i, 0)),
                  pl.BlockSpec((kd, N_HEADS * QK_NOPE), lambda i: (0, 0)),
                  pl.BlockSpec((N_HEADS * V_DIM, kd), lambda i: (0, 0))],
        out_specs=[pl.BlockSpec((N_HEADS, tk, QK_NOPE), lambda i: (0, i, 0)),
                   pl.BlockSpec((N_HEADS, None, V_DIM, tk), lambda i: (0, i, 0, 0))],
        out_shape=[jax.ShapeDtypeStruct((N_HEADS, r, QK_NOPE), BF16),
                   jax.ShapeDtypeStruct((N_HEADS, r // tk, V_DIM, tk), BF16)],
        compiler_params=_params("parallel"),
        name="kv_proj_t",
    )(c, wk, wvt)


def _post_small_kernel(z_ref, g_ref, cos_ref, sin_ref, c_ref, kr_ref):
    z = z_ref[...]
    ckv = z[:, :KV_RANK]
    ms = jnp.mean(ckv * ckv, axis=-1, keepdims=True)
    c_ref[...] = ckv * lax.rsqrt(ms + RMS_EPS) * g_ref[...]
    kr = z[:, KV_RANK:KV_RANK + QK_ROPE]
    kr_sw = z[:, KV_RANK + QK_ROPE:KV_RANK + 2 * QK_ROPE]
    kr_ref[...] = kr * cos_ref[...] + kr_sw * sin_ref[...]


def _post_small(z_small, g_kv, cos64, sin64, row0, rows):
    w = z_small.shape[1]
    tm = _pick(math.gcd(rows, row0), (1024, 512, 256, 128))
    b0 = row0 // tm
    return pl.pallas_call(
        _post_small_kernel,
        grid=(rows // tm,),
        in_specs=[pl.BlockSpec((tm, w), lambda i: (b0 + i, 0)),
                  pl.BlockSpec((1, KV_RANK), lambda i: (0, 0)),
                  pl.BlockSpec((tm, QK_ROPE), lambda i: (b0 + i, 0)),
                  pl.BlockSpec((tm, QK_ROPE), lambda i: (b0 + i, 0))],
        out_specs=[pl.BlockSpec((tm, KV_RANK), lambda i: (i, 0)),
                   pl.BlockSpec((tm, QK_ROPE), lambda i: (i, 0))],
        out_shape=[jax.ShapeDtypeStruct((rows, KV_RANK), F32),
                   jax.ShapeDtypeStruct((rows, QK_ROPE), F32)],
        compiler_params=_params("parallel"),
        name="latent_post",
    )(z_small, g_kv.reshape(1, KV_RANK), cos64, sin64)


def _q_prep_kernel(qn_ref, qr_ref, qsw_ref, cos_ref, sin_ref, o_ref):
    scale = QK_DIM ** -0.5 * math.log2(math.e)
    cos = jnp.tile(cos_ref[...], (1, N_HEADS))
    sin = jnp.tile(sin_ref[...], (1, N_HEADS))
    qr = (qr_ref[...].astype(F32) * cos + qsw_ref[...].astype(F32) * sin) * scale
    qn = qn_ref[...].astype(F32) * scale
    for h in range(N_HEADS):
        o_ref[h, :, 0:QK_NOPE] = qn[:, h * QK_NOPE:(h + 1) * QK_NOPE].astype(o_ref.dtype)
        o_ref[h, :, QK_NOPE:QK_DIM] = qr[:, h * QK_ROPE:(h + 1) * QK_ROPE].astype(o_ref.dtype)


def _q_prep(z_big, cos64, sin64, row0, rows):
    tm = _pick(math.gcd(rows, row0) if row0 else rows, (256, 128, 32))
    b0 = row0 // tm
    nope_blk = 3 * D_CONV // (N_HEADS * QK_NOPE)
    rope_blk = (3 * D_CONV + N_HEADS * QK_NOPE) // (N_HEADS * QK_ROPE)
    return pl.pallas_call(
        _q_prep_kernel,
        grid=(rows // tm,),
        in_specs=[pl.BlockSpec((tm, N_HEADS * QK_NOPE), lambda i: (b0 + i, nope_blk)),
                  pl.BlockSpec((tm, N_HEADS * QK_ROPE), lambda i: (b0 + i, rope_blk)),
                  pl.BlockSpec((tm, N_HEADS * QK_ROPE), lambda i: (b0 + i, rope_blk + 1)),
                  pl.BlockSpec((tm, QK_ROPE), lambda i: (b0 + i, 0)),
                  pl.BlockSpec((tm, QK_ROPE), lambda i: (b0 + i, 0))],
        out_specs=pl.BlockSpec((N_HEADS, tm, QK_DIM), lambda i: (0, i, 0)),
        out_shape=jax.ShapeDtypeStruct((N_HEADS, rows, QK_DIM), BF16),
        compiler_params=_params("parallel"),
        name="q_prep",
    )(z_big, z_big, z_big, cos64, sin64)


def _conv_kernel(state_ref, pc_ref, px_ref, b_ref, c_ref, x_ref, w_ref, o_ref, last_ref, *, tm, halo):
    i = pl.program_id(1)
    u = c_ref[...].astype(F32) * x_ref[...].astype(F32)
    prev = pc_ref[...].astype(F32) * px_ref[...].astype(F32)
    st = state_ref[...]
    first = i == 0
    um1 = jnp.where(first, st[7:8, :], prev[halo - 1:halo, :])
    um2 = jnp.where(first, st[6:7, :], prev[halo - 2:halo - 1, :])
    row = lax.broadcasted_iota(jnp.int32, u.shape, 0)
    s1 = jnp.where(row == 0, um1, pltpu.roll(u, 1, 0))
    s2 = jnp.where(row == 0, um2, jnp.where(row == 1, um1, pltpu.roll(u, 2, 0)))
    w = w_ref[...]
    y = w[0:1, :] * s2 + w[1:2, :] * s1 + w[2:3, :] * u
    o_ref[...] = (b_ref[...].astype(F32) * y).astype(o_ref.dtype)
    last_ref[...] = u[tm - 8:tm, :]


def _conv_gate(z_big, state, conv_w, row0, nseq, seq_len, into=None):
    tm = _pick(seq_len, (512, 256, 128, 32))
    halo = 16
    nb = seq_len // tm
    b0 = row0 // tm
    h0 = row0 // halo
    per = tm // halo
    d = D_CONV

    def hmap(col):
        return lambda s, i: (jnp.maximum(h0 + (s * nb + i) * per - 1, 0), col)

    def bmap(col):
        return lambda s, i: (b0 + s * nb + i, col)

    return _fill_call(
        functools.partial(_conv_kernel, tm=tm, halo=halo), into=into,
        grid=(nseq, nb),
        in_specs=[pl.BlockSpec((None, 8, d), lambda s, i: (s, 0, 0)),
                  pl.BlockSpec((halo, d), hmap(1)),
                  pl.BlockSpec((halo, d), hmap(2)),
                  pl.BlockSpec((tm, d), bmap(0)),
                  pl.BlockSpec((tm, d), bmap(1)),
                  pl.BlockSpec((tm, d), bmap(2)),
                  pl.BlockSpec((8, d), lambda s, i: (0, 0))],
        args=(state, z_big, z_big, z_big, z_big, z_big, jnp.pad(conv_w, ((0, 5), (0, 0)))),
        out_specs=[pl.BlockSpec((tm, d), bmap(0)),
                   pl.BlockSpec((None, 8, d), lambda s, i: (s * nb + i, 0, 0))],
        out_shape=[jax.ShapeDtypeStruct((z_big.shape[0], d), BF16),
                   jax.ShapeDtypeStruct((nseq * nb, 8, d), F32)],
        compiler_params=_params("parallel", "arbitrary"),
        name="conv_gate",
    )


def _attn_latent_kernel(q_ref, wk_ref, wv_ref, cp_ref, krp_ref, cn_ref, krn_ref, o_ref, qa_ref, acc_ref,
                        *, past, tk):
    nh, dseq, _ = q_ref.shape
    rows = nh * dseq
    for h in range(nh):
        qh = q_ref[h]
        qa = lax.dot_general(qh[:, :QK_NOPE], wk_ref[h], NT_DIMS, preferred_element_type=F32)
        qa_ref[h * dseq:(h + 1) * dseq, 0:KV_RANK] = qa.astype(qa_ref.dtype)
        qa_ref[h * dseq:(h + 1) * dseq, KV_RANK:KV_RANK + QK_ROPE] = qh[:, QK_NOPE:QK_DIM]
    q = qa_ref[...]
    acc_ref[...] = jnp.zeros(acc_ref.shape, F32)

    def update(carry, s, c):
        m, l = carry
        m_new = jnp.maximum(m, jnp.max(s, axis=1, keepdims=True))
        p = jnp.exp2(s - m_new)
        alpha = jnp.exp2(m - m_new)
        acc_ref[...] = alpha * acc_ref[...] + jnp.dot(p.astype(BF16), c, preferred_element_type=F32)
        return m_new, alpha * l + jnp.sum(p, axis=1, keepdims=True)

    def past_step(j, carry):
        ks = pl.ds(pl.multiple_of(j * tk, tk), tk)
        c = cp_ref[ks, :].astype(BF16)
        k = jnp.concatenate([c, krp_ref[ks, :].astype(BF16)], axis=1)
        return update(carry, lax.dot_general(q, k, NT_DIMS, preferred_element_type=F32), c)

    carry = (jnp.full((rows, 1), MASK_VALUE, F32), jnp.zeros((rows, 1), F32))
    carry = lax.fori_loop(0, past // tk, past_step, carry)

    cn = cn_ref[...].astype(BF16)
    kn = jnp.concatenate([cn, krn_ref[...].astype(BF16)], axis=1)
    s = lax.dot_general(q, kn, NT_DIMS, preferred_element_type=F32)
    q_pos = past + lax.broadcasted_iota(jnp.int32, (rows, 1), 0) % dseq
    k_pos = past + lax.broadcasted_iota(jnp.int32, (1, dseq), 1)
    s = jnp.where(k_pos // CHUNK <= q_pos // CHUNK, s, MASK_VALUE)
    m, l = update(carry, s, cn)

    lat = (acc_ref[...] / l).astype(BF16)
    for h in range(nh):
        o = jnp.dot(lat[h * dseq:(h + 1) * dseq, :], wv_ref[h], preferred_element_type=F32)
        o_ref[:, h * V_DIM:(h + 1) * V_DIM] = o.astype(o_ref.dtype)


def _attention_latent(q, wk, wv, c_past, kr_past, c_new, kr_new, row0, into):
    nb, past, _ = c_past.shape
    dseq = c_new.shape[1]
    tk = _pick(past, (512, 256, 128))
    assert row0 % dseq == 0
    b0 = row0 // dseq
    per_b = lambda w: pl.BlockSpec((None,) + w, lambda b: (b, 0, 0))
    whole = lambda a: pl.BlockSpec(a.shape, lambda b: (0, 0, 0))
    return _fill_call(
        functools.partial(_attn_latent_kernel, past=past, tk=tk), into=into,
        grid=(nb,),
        in_specs=[pl.BlockSpec((N_HEADS, dseq, QK_DIM), lambda b: (0, b, 0)),
                  whole(wk), whole(wv),
                  per_b((past, KV_RANK)), per_b((past, QK_ROPE)),
                  per_b((dseq, KV_RANK)), per_b((dseq, QK_ROPE))],
        args=(q, wk, wv, c_past, kr_past, c_new, kr_new),
        out_specs=pl.BlockSpec((dseq, N_HEADS * V_DIM), lambda b: (b0 + b, 0)),
        out_shape=jax.ShapeDtypeStruct(into.shape, into.dtype),
        scratch_shapes=[pltpu.VMEM((N_HEADS * dseq, KV_RANK + QK_ROPE), BF16),
                        pltpu.VMEM((N_HEADS * dseq, KV_RANK), F32)],
        compiler_params=_params("parallel"),
        name="attention_latent",
    )


def _attn_t_kernel(q_ref, k_ref, kr_ref, vt_ref, o_ref, m_ref, l_ref, acc_ref, s0_ref, s1_ref, *, tb):
    i = pl.program_id(1)
    q = q_ref[...]
    q_pos = i * tb + lax.broadcasted_iota(jnp.int32, (1, tb), 1)
    col_end = (q_pos // CHUNK + 1) * CHUNK

    m_ref[...] = jnp.full(m_ref.shape, MASK_VALUE, F32)
    l_ref[...] = jnp.zeros(l_ref.shape, F32)
    acc_ref[...] = jnp.zeros(acc_ref.shape, F32)

    def produce(s_ref, j):
        ks = pl.ds(pl.multiple_of(j * tb, tb), tb)
        k = jnp.concatenate([k_ref[ks, :], kr_ref[ks, :]], axis=1)
        s_ref[...] = lax.dot_general(k, q, NT_DIMS, preferred_element_type=F32)

    def consume(s_ref, j, masked):
        s = s_ref[...]
        if masked:
            k_pos = j * tb + lax.broadcasted_iota(jnp.int32, (tb, 1), 0)
            s = jnp.where(k_pos < col_end, s, MASK_VALUE)
        m_old = m_ref[...]
        m_new = jnp.maximum(m_old, jnp.max(s, axis=0, keepdims=True))
        p = jnp.exp2(s - m_new)
        alpha = jnp.exp2(m_old - m_new)
        m_ref[...] = m_new
        l_ref[...] = alpha * l_ref[...] + jnp.sum(p, axis=0, keepdims=True)
        acc_ref[...] = alpha * acc_ref[...] + jnp.dot(vt_ref[j], p.astype(BF16), preferred_element_type=F32)

    produce(s0_ref, 0)

    bufs = (s0_ref, s1_ref)
    unroll = 8

    def group(g, carry):
        j0 = unroll * g
        for t in range(unroll):
            produce(bufs[(t + 1) % 2], j0 + t + 1)
            consume(bufs[t % 2], j0 + t, False)
        return carry

    lax.fori_loop(0, i // unroll, group, 0)
    j0 = i // unroll * unroll

    for rem in range(unroll):
        @pl.when(i - j0 == rem)
        def _(rem=rem):
            for t in range(rem + 1):
                if t < rem:
                    produce(bufs[(t + 1) % 2], j0 + t + 1)
                consume(bufs[t % 2], j0 + t, t == rem)

    o_ref[...] = (acc_ref[...] / l_ref[...]).T.astype(o_ref.dtype)


def _attention_t(q, k, kr, vt, seq, tk, out_rows):
    tq = tk
    assert tk % CHUNK == 0 and seq % tk == 0
    return pl.pallas_call(
        functools.partial(_attn_t_kernel, tb=tk),
        grid=(N_HEADS, seq // tq),
        in_specs=[pl.BlockSpec((None, tq, QK_DIM), lambda h, i: (h, i, 0)),
                  pl.BlockSpec((None, seq, QK_NOPE), lambda h, i: (h, 0, 0)),
                  pl.BlockSpec((seq, QK_ROPE), lambda h, i: (0, 0)),
                  pl.BlockSpec((None, seq // tk, V_DIM, tk), lambda h, i: (h, 0, 0, 0))],
        out_specs=pl.BlockSpec((tq, V_DIM), lambda h, i: (i, h)),
        out_shape=jax.ShapeDtypeStruct((out_rows, N_HEADS * V_DIM), BF16),
        scratch_shapes=[pltpu.VMEM((1, tq), F32), pltpu.VMEM((1, tq), F32), pltpu.VMEM((V_DIM, tq), F32),
                        pltpu.VMEM((tk, tq), F32), pltpu.VMEM((tk, tq), F32)],
        compiler_params=_params("parallel", "arbitrary"),
        name="attention_t",
    )(q, k, kr, vt)


def _merge_kernel(a_ref, o_ref, wa_ref, wb_ref, ga_ref, gb_ref, m_ref):
    ya = jnp.dot(a_ref[...], wa_ref[...], preferred_element_type=F32)
    yb = jnp.dot(o_ref[...], wb_ref[...], preferred_element_type=F32)
    m = jax.nn.sigmoid(ga_ref[...].astype(F32)) * ya + jax.nn.sigmoid(gb_ref[...].astype(F32)) * yb
    m_ref[...] = m.astype(m_ref.dtype)


def _merge(ya_in, o, wa, wb, z_big):
    t, k = ya_in.shape
    n = wa.shape[1]
    tm = _pick(t, (1280, 1024, 512, 256, 128))
    tn = _pick(n, (512, 256, 128))
    ga_blk = (3 * D_CONV + N_HEADS * (QK_NOPE + 2 * QK_ROPE)) // tn
    gb_blk = ga_blk + D_MODEL // tn
    return pl.pallas_call(
        _merge_kernel,
        grid=(t // tm, n // tn),
        in_specs=[pl.BlockSpec((tm, k), lambda i, j: (i, 0)),
                  pl.BlockSpec((tm, k), lambda i, j: (i, 0)),
                  pl.BlockSpec((k, tn), lambda i, j: (0, j)),
                  pl.BlockSpec((k, tn), lambda i, j: (0, j)),
                  pl.BlockSpec((tm, tn), lambda i, j: (i, ga_blk + j)),
                  pl.BlockSpec((tm, tn), lambda i, j: (i, gb_blk + j))],
        out_specs=pl.BlockSpec((tm, tn), lambda i, j: (i, j)),
        out_shape=jax.ShapeDtypeStruct((t, n), BF16),
        compiler_params=_params("parallel", "arbitrary"),
        name="merge",
    )(ya_in, o, wa, wb, z_big, z_big)


def _extract_top(work_ref, rank_ref, val_ref, idx_ref, *, track_rank):
    nk = PEER_NKEYS
    tm = work_ref.shape[1]
    neg = jnp.full((8, tm), -jnp.inf, F32)

    def round_(p, carry):
        idx_prev, pf = carry
        parts = [neg, neg, neg, neg]
        for k in range(nk):
            rows = slice(8 * k, 8 * k + 8)
            hit = idx_prev == float(k)
            w = jnp.where(hit, -jnp.inf, work_ref[rows, :])
            work_ref[rows, :] = w
            if track_rank:
                rank_ref[rows, :] = jnp.where(hit, pf - 1.0, rank_ref[rows, :])
            parts[k % 4] = jnp.maximum(parts[k % 4], w)
        m = jnp.maximum(jnp.maximum(parts[0], parts[1]), jnp.maximum(parts[2], parts[3]))
        big = jnp.full((8, tm), float(nk), F32)
        iparts = [big, big, big, big]
        for k in range(nk):
            w = work_ref[8 * k:8 * k + 8, :]
            iparts[k % 4] = jnp.minimum(iparts[k % 4], jnp.where(w == m, float(k), float(nk)))
        idx = jnp.minimum(jnp.minimum(iparts[0], iparts[1]), jnp.minimum(iparts[2], iparts[3]))
        val_ref[p] = m
        idx_ref[p] = idx
        return idx, pf + 1.0

    idx_last, _ = lax.fori_loop(0, PEER_TOPK, round_,
                                (jnp.full((8, tm), -1.0, F32), jnp.zeros((8, tm), F32)))
    if track_rank:
        for k in range(nk):
            rows = slice(8 * k, 8 * k + 8)
            rank_ref[rows, :] = jnp.where(idx_last == float(k), float(PEER_TOPK - 1), rank_ref[rows, :])


def _peer_select_kernel(qp_ref, k1_ref, k2_ref, k2h_ref, perm_ref,
                        r2_ref, e2_ref, n1_ref, w1_ref,
                        work_ref, rank_ref, a_ref, ia_ref, b_ref, ib_ref):
    nk = PEER_NKEYS
    qp = qp_ref[...]
    tm = qp.shape[0]
    s2h = lax.dot_general(k2h_ref[...], qp, NT_DIMS, preferred_element_type=F32)
    for h in range(PEER_HEADS):
        blk = s2h[h * nk:(h + 1) * nk, :]
        e2_ref[h * nk:(h + 1) * nk, :] = jnp.exp(blk - jnp.max(blk, axis=0, keepdims=True))

    work_ref[...] = lax.dot_general(k1_ref[...], qp, NT_DIMS, preferred_element_type=F32)
    _extract_top(work_ref, rank_ref, a_ref, ia_ref, track_rank=False)
    work_ref[...] = lax.dot_general(k2_ref[...], qp, NT_DIMS, preferred_element_type=F32)
    rank_ref[...] = jnp.full(rank_ref.shape, float(PEER_TOPK), F32)
    _extract_top(work_ref, rank_ref, b_ref, ib_ref, track_rank=True)
    r2_ref[...] = jnp.dot(perm_ref[...], rank_ref[...].astype(BF16), preferred_element_type=F32)

    a = [a_ref[p] for p in range(PEER_TOPK)]
    b = [b_ref[q] for q in range(PEER_TOPK)]
    cand = [a[p] + b[q] for (p, q) in _PAIRS]
    npair = len(_PAIRS)
    beaten = [jnp.zeros((8, tm), F32) for _ in range(npair)]
    for x in range(npair):
        px, qx = _PAIRS[x]
        for y in range(x + 1, npair):
            py, qy = _PAIRS[y]
            if px <= py and qx <= qy:
                beaten[y] = beaten[y] + 1.0
            else:
                gt = jnp.where(cand[y] > cand[x], 1.0, 0.0)
                beaten[x] = beaten[x] + gt
                beaten[y] = beaten[y] + (1.0 - gt)
    sel = [jnp.where(bt < float(PEER_TOPK), 1.0, 0.0) for bt in beaten]
    ea = [jnp.exp(a[p] - a[0]) for p in range(PEER_TOPK)]
    eb = [jnp.exp(b[q] - b[0]) for q in range(PEER_TOPK)]
    z = jnp.zeros((8, tm), F32)
    cnt = [jnp.zeros((8, tm), F32) for _ in range(PEER_TOPK)]
    for x, (p, q) in enumerate(_PAIRS):
        z = z + sel[x] * (ea[p] * eb[q])
        cnt[p] = cnt[p] + sel[x]
    inv_z = 1.0 / z
    w1 = [ea[p] * inv_z for p in range(PEER_TOPK)]
    ia = [ia_ref[p] for p in range(PEER_TOPK)]
    zero = jnp.zeros((8, tm), F32)
    for k in range(nk):
        n1k = zero
        w1k = zero
        for p in range(PEER_TOPK):
            hit = ia[p] == float(k)
            n1k = jnp.where(hit, cnt[p], n1k)
            w1k = jnp.where(hit, w1[p], w1k)
        n1_ref[8 * k:8 * k + 8, :] = n1k
        w1_ref[8 * k:8 * k + 8, :] = w1k


def _peer_select(qp, k1, k2, k2h, perm):
    t, d = qp.shape
    tm = _pick(t, (256, 128))
    rows = PEER_NKEYS * PEER_HEADS
    full = lambda i: (0, 0)
    out = jax.ShapeDtypeStruct((rows, t), F32)
    ospec = pl.BlockSpec((rows, tm), lambda i: (0, i))
    return pl.pallas_call(
        _peer_select_kernel,
        grid=(t // tm,),
        in_specs=[pl.BlockSpec((tm, d), lambda i: (i, 0)),
                  pl.BlockSpec((rows, d), full),
                  pl.BlockSpec((rows, d), full),
                  pl.BlockSpec((rows, d), full),
                  pl.BlockSpec((rows, rows), full)],
        out_specs=[ospec, ospec, ospec, ospec],
        out_shape=[out, out, out, out],
        scratch_shapes=[pltpu.VMEM((rows, tm), F32), pltpu.VMEM((rows, tm), F32),
                        pltpu.VMEM((PEER_TOPK, 8, tm), F32), pltpu.VMEM((PEER_TOPK, 8, tm), F32),
                        pltpu.VMEM((PEER_TOPK, 8, tm), F32), pltpu.VMEM((PEER_TOPK, 8, tm), F32)],
        compiler_params=_params("parallel"),
        name="peer_select",
    )(qp, k1, k2, k2h, perm)


def _gelu_tanh(x):
    c = math.sqrt(2.0 / math.pi)
    return 0.5 * x * (1.0 + jnp.tanh(c * (x + 0.044715 * (x * x * x))))


def _peer_main_kernel(hn_ref, u_ref, va_ref, vb_ref, r2_ref, e2_ref, n1_ref, w1_ref,
                      o_ref, ga_ref, gb_ref, pa_ref, pb_ref, act_ref, *, tc):
    nk = PEER_NKEYS
    te, tm = ga_ref.shape
    j = pl.program_id(1)
    last = pl.num_programs(1) - 1

    hn = hn_ref[...]

    nw = 256
    d = o_ref.shape[1]

    n_il = te // nk
    jr = 64

    def gate_tile(g_ref, nref, wref, half, c, jh):
        cols = slice(c * tc, (c + 1) * tc)
        accs = [jnp.zeros((jr, tc), F32) for _ in range(n_il)]
        for h in range(PEER_HEADS):
            hr = slice(h * nk + jh * jr, h * nk + (jh + 1) * jr)
            r2 = r2_ref[hr, cols]
            e2 = e2_ref[hr, cols]
            for il in range(n_il):
                r = (half * n_il + il) * PEER_HEADS + h
                keep = r2 < nref[r:r + 1, cols]
                accs[il] = accs[il] + jnp.where(keep, e2, 0.0) * wref[r:r + 1, cols]
        for il in range(n_il):
            g_ref[il * nk + jh * jr:il * nk + (jh + 1) * jr, cols] = accs[il]

    def gate(g_ref, nref, wref, half):
        for c in range(tm // tc):
            for jh in range(nk // jr):
                gate_tile(g_ref, nref, wref, half, c, jh)

    @pl.when(j == 0)
    def _():
        o_ref[...] = jnp.zeros(o_ref.shape, o_ref.dtype)
        pb_ref[...] = jnp.zeros(pb_ref.shape, pb_ref.dtype)

    def act_piece(half, n):
        u = u_ref[half * te + n * nw:half * te + (n + 1) * nw, :]
        return _gelu_tanh(lax.dot_general(hn, u, NT_DIMS, preferred_element_type=F32))

    def out_piece(p_ref, v_ref, n):
        cols = slice(n * nw, (n + 1) * nw)
        o_ref[:, cols] += jnp.dot(p_ref[...], v_ref[:, cols], preferred_element_type=F32)

    def sub_block(half, g_ref, p_new_ref, p_old_ref, v_old_ref, next_gate):
        for n in range(te // nw):
            act_ref[:, n * nw:(n + 1) * nw] = act_piece(half, n)
        for n in range(d // nw):
            out_piece(p_old_ref, v_old_ref, n)
        gate(*next_gate)
        for il in range(n_il):
            ecols = slice(il * nk, (il + 1) * nk)
            for c in range(tm // tc):
                rows = slice(c * tc, (c + 1) * tc)
                g = g_ref[ecols, rows].T
                p_new_ref[rows, ecols] = (g * act_ref[rows, ecols]).astype(p_new_ref.dtype)

    @pl.when(j < last)
    def _():
        sub_block(0, ga_ref, pa_ref, pb_ref, vb_ref, (ga_ref, n1_ref, w1_ref, 0))
        sub_block(1, gb_ref, pb_ref, pa_ref, va_ref, (gb_ref, n1_ref, w1_ref, 1))

    @pl.when(j == last)
    def _():
        for n in range(d // nw):
            out_piece(pb_ref, vb_ref, n)


def _peer_main(hn, u, v, r2, e2, n1, w1):
    t, d = hn.shape
    e = u.shape[0]
    tm = _pick(t, (640, 512, 256, 128))
    te = 512
    tc = 128
    rows = PEER_NKEYS * PEER_HEADS
    sub = 2 * te // PEER_NKEYS * PEER_HEADS
    nj = e // (2 * te)
    cur = lambda j: jnp.minimum(j, nj - 1)
    return pl.pallas_call(
        functools.partial(_peer_main_kernel, tc=tc),
        grid=(t // tm, nj + 1),
        in_specs=[pl.BlockSpec((tm, d), lambda i, j: (i, 0)),
                  pl.BlockSpec((2 * te, d), lambda i, j: (cur(j), 0)),
                  pl.BlockSpec((te, d), lambda i, j: (2 * cur(j), 0)),
                  pl.BlockSpec((te, d), lambda i, j: (jnp.maximum(2 * j - 1, 0), 0)),
                  pl.BlockSpec((rows, tm), lambda i, j: (0, i)),
                  pl.BlockSpec((rows, tm), lambda i, j: (0, i)),
                  pl.BlockSpec((sub, tm), lambda i, j: (cur(j), i)),
                  pl.BlockSpec((sub, tm), lambda i, j: (cur(j), i))],
        out_specs=pl.BlockSpec((tm, d), lambda i, j: (i, 0)),
        out_shape=jax.ShapeDtypeStruct((t, d), F32),
        scratch_shapes=[pltpu.VMEM((te, tm), F32), pltpu.VMEM((te, tm), F32),
                        pltpu.VMEM((tm, te), BF16), pltpu.VMEM((tm, te), BF16),
                        pltpu.VMEM((tm, te), F32)],
        compiler_params=_params("parallel", "arbitrary"),
        name="peer_main",
    )(hn, u, v, v, r2, e2, n1, w1)


def _ple_kernel(xn_ref, p_ref, wg_ref, wp_ref, h_ref, o_ref):
    gate = jnp.dot(xn_ref[...], wg_ref[...], preferred_element_type=F32)
    pe = jnp.dot(p_ref[...].astype(BF16), wp_ref[...], preferred_element_type=F32)
    o_ref[...] = h_ref[...] + pe * jax.nn.sigmoid(gate)


def _ple(xn, p, wg, wp, h, row0, into=None):
    t, d = xn.shape
    rows, pd = p.shape
    tm = _pick(math.gcd(rows, row0), (1024, 512, 256, 128))
    tn = _pick(d, (512, 256, 128))
    b0 = row0 // tm
    return _fill_call(
        _ple_kernel, into=into,
        grid=(rows // tm, d // tn),
        in_specs=[pl.BlockSpec((tm, d), lambda i, j: (b0 + i, 0)),
                  pl.BlockSpec((tm, pd), lambda i, j: (i, 0)),
                  pl.BlockSpec((d, tn), lambda i, j: (0, j)),
                  pl.BlockSpec((pd, tn), lambda i, j: (0, j)),
                  pl.BlockSpec((tm, tn), lambda i, j: (b0 + i, j))],
        args=(xn, p, wg, wp, h),
        out_specs=pl.BlockSpec((tm, tn), lambda i, j: (b0 + i, j)),
        out_shape=jax.ShapeDtypeStruct((t, d), F32),
        compiler_params=_params("parallel", "arbitrary"),
        name="ple",
    )


def _split_w_in(w_in):
    off = [0] + [int(s) for s in np.cumsum(IN_SIZES)]
    col = lambda a, b: w_in[:, a:b]
    half = QK_ROPE // 2
    q0 = off[3]
    nope, rope, rope_sw = [], [], []
    for h in range(N_HEADS):
        r0 = q0 + h * QK_DIM + QK_NOPE
        nope.append(col(q0 + h * QK_DIM, r0))
        rope.append(col(r0, r0 + QK_ROPE))
        rope_sw += [col(r0 + half, r0 + QK_ROPE), col(r0, r0 + half)]
    w_big = jnp.concatenate([col(off[0], off[3])] + nope + rope + rope_sw + [col(off[6], off[8])], axis=1)
    k0 = off[5]
    w_small = jnp.concatenate([col(off[4], off[6]), col(k0 + half, k0 + QK_ROPE), col(k0, k0 + half)], axis=1)
    return w_big.astype(BF16), w_small.astype(BF16)


def _rope_tables(pos):
    inv = 1.0 / (ROPE_THETA ** (jnp.arange(0, QK_ROPE, 2, dtype=F32) / QK_ROPE))
    ang = pos.astype(F32)[:, None] * inv[None, :]
    cos, sin = jnp.cos(ang), jnp.sin(ang)
    return jnp.concatenate([cos, cos], axis=1), jnp.concatenate([-sin, sin], axis=1)


def _key_matrices(sub_keys):
    nk, hd, half = PEER_NKEYS, PEER_HEADS, PEER_DKEY // 2
    eye = jnp.eye(hd, dtype=F32)

    def build(c, head_major):
        sel = jnp.zeros((2,), F32).at[c].set(1.0)
        m = sub_keys[c][:, None, None, None, :] * eye[None, :, :, None, None] * sel[None, None, None, :, None]
        if head_major:
            m = jnp.transpose(m, (1, 0, 2, 3, 4))
        return m.reshape(nk * hd, hd * PEER_DKEY).astype(BF16)

    r = np.arange(nk * hd)
    perm = np.zeros((nk * hd, nk * hd), np.float32)
    perm[(r % hd) * nk + r // hd, r] = 1.0
    return build(0, False), build(1, False), build(1, True), jnp.asarray(perm, BF16)


def kernel(x_prompt, x_sample, cache_conv, cache_ckv, cache_krope, p_prompt, p_sample, g_mix, w_in, conv_w, g_kv, w_kv_b, w_a_out, w_b_out, w_o, g_ffn, w_pq, sub_keys, u_tab, v_tab, g_ple, w_ple_gate, w_ple, g_final):
    assert x_prompt.shape[0] == 1 and w_in.shape[0] == 1
    seq = x_prompt.shape[1]
    nb, dseq = x_sample.shape[0], x_sample.shape[1]
    past = cache_ckv.shape[2]
    d = D_MODEL
    ns = nb * dseq
    t = seq + ns

    x_p, x_s = x_prompt.reshape(seq, d), x_sample.reshape(ns, d)
    p_p, p_s = p_prompt[0].reshape(seq, -1), p_sample[0].reshape(ns, -1)
    pos = jnp.concatenate([jnp.arange(seq), jnp.tile(past + jnp.arange(dseq), nb)])
    cos64, sin64 = _rope_tables(pos)
    w_big, w_small = _split_w_in(w_in[0])

    xn = _rmsnorm(x_p, g_mix[0], BF16, out_rows=t)
    xn = _rmsnorm(x_s, g_mix[0], BF16, out_row0=seq, out_rows=t, into=xn)
    z_big = _matmul(xn, w_big, BF16, name="in_proj")
    z_small = _matmul(xn, w_small, F32, tn_prefs=(w_small.shape[1],), name="in_proj_small")
    c_p, kr_p = _post_small(z_small, g_kv[0], cos64, sin64, 0, seq)
    c_s, kr_s = _post_small(z_small, g_kv[0], cos64, sin64, seq, ns)

    zero_state = jnp.zeros((1, 8, D_CONV), F32)
    ya_in, last_p = _conv_gate(z_big, zero_state, conv_w[0], 0, 1, seq)
    state_s = jnp.pad(cache_conv[0], ((0, 0), (6, 0), (0, 0)))
    ya_in, last_s = _conv_gate(z_big, state_s, conv_w[0], seq, nb, dseq, into=ya_in)
    conv_p = last_p[-1, 6:8, :].reshape(1, 1, 2, D_CONV)
    if dseq >= 2:
        conv_s = last_s[:, 6:8, :].reshape(1, nb, 2, D_CONV)
    else:
        raise NotImplementedError("sample blocks shorter than the convolution state")

    w_kv_h = w_kv_b[0].astype(BF16).reshape(KV_RANK, N_HEADS, QK_NOPE + V_DIM)
    wk_h = jnp.transpose(w_kv_h[:, :, :QK_NOPE], (1, 0, 2))
    wv_h = jnp.transpose(w_kv_h[:, :, QK_NOPE:], (1, 0, 2))
    wk = w_kv_h[:, :, :QK_NOPE].reshape(KV_RANK, N_HEADS * QK_NOPE)
    wvt = w_kv_h[:, :, QK_NOPE:].reshape(KV_RANK, N_HEADS * V_DIM).T
    q_p = _q_prep(z_big, cos64, sin64, 0, seq)
    q_s = _q_prep(z_big, cos64, sin64, seq, ns)
    tk = _pick(seq, (512, 256, 128))
    k_p, vt_p = _kv_proj_t(c_p, wk, wvt, tk)
    o_all = _attention_t(q_p, k_p, kr_p.astype(BF16), vt_p, seq, tk, t)
    o_all = _attention_latent(q_s, wk_h, wv_h, cache_ckv[0], cache_krope[0],
                              c_s.reshape(nb, dseq, KV_RANK), kr_s.reshape(nb, dseq, QK_ROPE), seq, o_all)

    m = _merge(ya_in, o_all, w_a_out[0].astype(BF16), w_b_out[0].astype(BF16), z_big)
    w_out = w_o[0].astype(BF16)
    h1 = _matmul(m, w_out, F32, residual=x_p, name="out_proj", rows=seq)
    h1 = _matmul(m, w_out, F32, residual=x_s, name="out_proj", row0=seq, into=h1)

    hn = _rmsnorm(h1, g_ffn[0], BF16)
    qp = _matmul(hn, w_pq[0].astype(BF16), BF16, name="peer_query")
    k1, k2, k2h, perm = _key_matrices(sub_keys[0])
    r2, e2, n1, w1 = _peer_select(qp, k1, k2, k2h, perm)
    peer = _peer_main(hn, u_tab[0].astype(BF16), v_tab[0].astype(BF16), r2, e2, n1, w1)

    h2, x2n = _add_rmsnorm(h1, peer, g_ple[0], BF16)
    wg, wp = w_ple_gate[0].astype(BF16), w_ple[0].astype(BF16)
    h3 = _ple(x2n, p_p, wg, wp, h2, 0)
    h3 = _ple(x2n, p_s, wg, wp, h2, seq, into=h3)
    y_prompt = _rmsnorm(h3, g_final, F32, rows=seq).reshape(1, seq, d)
    y_sample = _rmsnorm(h3, g_final, F32, in_row0=seq).reshape(nb, dseq, d)

    ckv_p = c_p.reshape(1, 1, seq, KV_RANK)
    kr_p = kr_p.reshape(1, 1, seq, QK_ROPE)
    ckv_s = c_s.reshape(1, nb, dseq, KV_RANK)
    kr_s = kr_s.reshape(1, nb, dseq, QK_ROPE)
    return (y_prompt, y_sample, conv_p, ckv_p, kr_p, conv_s, ckv_s, kr_s)
```

```python
import functools
import math

import numpy as np
import jax
import jax.numpy as jnp
from jax import lax
from jax.experimental import pallas as pl
from jax.experimental.pallas import tpu as pltpu

D_MODEL = 2048
D_CONV = 2048
N_HEADS = 16
QK_NOPE = 128
QK_ROPE = 64
QK_DIM = QK_NOPE + QK_ROPE
V_DIM = 128
KV_RANK = 512
CHUNK = 64
ROPE_THETA = 10000.0
PEER_HEADS = 8
PEER_NKEYS = 128
PEER_TOPK = 16
PEER_DKEY = 256
PEER_EXPERTS = PEER_NKEYS * PEER_NKEYS
RMS_EPS = 1e-6
IN_SIZES = (D_CONV, D_CONV, D_CONV, N_HEADS * QK_DIM, KV_RANK, QK_ROPE, D_MODEL, D_MODEL)

BF16 = jnp.bfloat16
F32 = jnp.float32
VMEM_LIMIT_BYTES = 56 * 1024 * 1024
MASK_VALUE = -1e30
NT_DIMS = (((1,), (1,)), ((), ()))
TN_DIMS = (((0,), (0,)), ((), ()))

_PAIRS = tuple((p, q) for p in range(PEER_TOPK) for q in range(PEER_TOPK)
               if (p + 1) * (q + 1) <= PEER_TOPK)


def _pick(n, prefs):
    for p in prefs:
        if n % p == 0:
            return p
    return n


def _params(*sem):
    return pltpu.CompilerParams(dimension_semantics=sem, vmem_limit_bytes=VMEM_LIMIT_BYTES)


def _rms_kernel(x_ref, g_ref, o_ref):
    x = x_ref[...]
    ms = jnp.mean(x * x, axis=-1, keepdims=True)
    o_ref[...] = (x * lax.rsqrt(ms + RMS_EPS) * g_ref[...]).astype(o_ref.dtype)


def _fill_call(kern, *, into, in_specs, args, tail=None, **kwargs):
    if tail is not None:
        axis, n_real = tail
        n_in = len(in_specs)
        body = kern

        def kern(*refs):
            i = pl.program_id(axis)

            @pl.when(i < n_real)
            def _():
                body(*refs)

            @pl.when(i >= n_real)
            def _():
                refs[n_in][...] = jnp.zeros(refs[n_in].shape, refs[n_in].dtype)

    if into is None:
        return pl.pallas_call(kern, in_specs=in_specs, **kwargs)(*args)

    def filling(buf_ref, *refs):
        del buf_ref
        kern(*refs)

    return pl.pallas_call(filling, in_specs=[pl.BlockSpec(memory_space=pl.ANY)] + list(in_specs),
                          input_output_aliases={0: 0}, **kwargs)(into, *args)


def _tail_blocks(into, end_row, total_rows, tm):
    if into is not None or end_row >= total_rows:
        return 0
    return -(-(total_rows - end_row) // tm)


def _rmsnorm(x, g, out_dtype, in_row0=0, rows=None, out_row0=0, out_rows=None, into=None):
    d = x.shape[1]
    rows = x.shape[0] - in_row0 if rows is None else rows
    out_rows = rows if out_rows is None else out_rows
    tm = _pick(math.gcd(rows, in_row0, out_row0), (1024, 640, 512, 256, 128))
    bi, bo = in_row0 // tm, out_row0 // tm
    nb = rows // tm
    n_tail = _tail_blocks(into, out_row0 + rows, out_rows, tm)
    return _fill_call(
        _rms_kernel, into=into, tail=(0, nb) if n_tail else None,
        grid=(nb + n_tail,),
        in_specs=[pl.BlockSpec((tm, d), lambda i: (bi + jnp.minimum(i, nb - 1), 0)),
                  pl.BlockSpec((1, d), lambda i: (0, 0))],
        args=(x, g.reshape(1, d)),
        out_specs=pl.BlockSpec((tm, d), lambda i: (bo + i, 0)),
        out_shape=jax.ShapeDtypeStruct((out_rows, d), out_dtype),
        compiler_params=_params("parallel"),
        name="rmsnorm",
    )


def _add_rms_kernel(a_ref, b_ref, g_ref, s_ref, o_ref):
    x = a_ref[...] + b_ref[...]
    s_ref[...] = x
    ms = jnp.mean(x * x, axis=-1, keepdims=True)
    o_ref[...] = (x * lax.rsqrt(ms + RMS_EPS) * g_ref[...]).astype(o_ref.dtype)


def _add_rmsnorm(a, b, g, out_dtype):
    t, d = a.shape
    tm = _pick(t, (640, 512, 256, 128))
    row = pl.BlockSpec((tm, d), lambda i: (i, 0))
    return pl.pallas_call(
        _add_rms_kernel,
        grid=(t // tm,),
        in_specs=[row, row, pl.BlockSpec((1, d), lambda i: (0, 0))],
        out_specs=[row, row],
        out_shape=[jax.ShapeDtypeStruct((t, d), F32), jax.ShapeDtypeStruct((t, d), out_dtype)],
        compiler_params=_params("parallel"),
        name="add_rmsnorm",
    )(a, b, g.reshape(1, d))


def _mm_kernel(x_ref, w_ref, o_ref):
    x = x_ref[...].astype(BF16)
    o_ref[...] = jnp.dot(x, w_ref[...], preferred_element_type=F32).astype(o_ref.dtype)


def _mm_res_kernel(x_ref, w_ref, r_ref, o_ref):
    x = x_ref[...].astype(BF16)
    o_ref[...] = (r_ref[...] + jnp.dot(x, w_ref[...], preferred_element_type=F32)).astype(o_ref.dtype)


def _matmul(x, w, out_dtype, residual=None, tm_prefs=(1280, 1024, 768, 512, 256, 128),
            tn_prefs=(1024, 512, 256, 128), name="matmul", row0=0, rows=None, into=None):
    t, k = x.shape
    n = w.shape[1]
    rows = t - row0 if rows is None else rows
    tm = _pick(math.gcd(rows, row0), tm_prefs)
    tn = _pick(n, tn_prefs)
    b0 = row0 // tm
    nb = rows // tm
    n_tail = _tail_blocks(into, row0 + rows, t, tm)
    real = lambda i: jnp.minimum(i, nb - 1)
    in_specs = [pl.BlockSpec((tm, k), lambda i, j: (b0 + real(i), 0)),
                pl.BlockSpec((k, tn), lambda i, j: (0, j))]
    args = [x, w]
    kern = _mm_kernel
    if residual is not None:
        in_specs.append(pl.BlockSpec((tm, tn), lambda i, j: (real(i), j)))
        args.append(residual)
        kern = _mm_res_kernel
    return _fill_call(
        kern, into=into, tail=(0, nb) if n_tail else None,
        grid=(nb + n_tail, n // tn),
        in_specs=in_specs,
        args=args,
        out_specs=pl.BlockSpec((tm, tn), lambda i, j: (b0 + i, j)),
        out_shape=jax.ShapeDtypeStruct((t, n), out_dtype),
        compiler_params=_params("parallel", "arbitrary"),
        name=name,
    )


def _kv_proj_t_kernel(c_ref, wk_ref, wvt_ref, k_ref, vt_ref):
    c = c_ref[...].astype(BF16)
    k = jnp.dot(c, wk_ref[...], preferred_element_type=F32)
    vt = lax.dot_general(wvt_ref[...], c, NT_DIMS, preferred_element_type=F32)
    for h in range(N_HEADS):
        k_ref[h] = k[:, h * QK_NOPE:(h + 1) * QK_NOPE].astype(k_ref.dtype)
        vt_ref[h] = vt[h * V_DIM:(h + 1) * V_DIM, :].astype(vt_ref.dtype)


def _kv_proj_t(c, wk, wvt, tk):
    r, kd = c.shape
    return pl.pallas_call(
        _kv_proj_t_kernel,
        grid=(r // tk,),
        in_specs=[pl.BlockSpec((tk, kd), lambda i: (i, 0)),
                  pl.BlockSpec((kd, N_HEADS * QK_NOPE), lambda i: (0, 0)),
                  pl.BlockSpec((N_HEADS * V_DIM, kd), lambda i: (0, 0))],
        out_specs=[pl.BlockSpec((N_HEADS, tk, QK_NOPE), lambda i: (0, i, 0)),
                   pl.BlockSpec((N_HEADS, None, V_DIM, tk), lambda i: (0, i, 0, 0))],
        out_shape=[jax.ShapeDtypeStruct((N_HEADS, r, QK_NOPE), BF16),
                   jax.ShapeDtypeStruct((N_HEADS, r // tk, V_DIM, tk), BF16)],
        compiler_params=_params("parallel"),
        name="kv_proj_t",
    )(c, wk, wvt)


def _post_small_kernel(z_ref, g_ref, cos_ref, sin_ref, c_ref, kr_ref):
    z = z_ref[...]
    ckv = z[:, :KV_RANK]
    ms = jnp.mean(ckv * ckv, axis=-1, keepdims=True)
    c_ref[...] = ckv * lax.rsqrt(ms + RMS_EPS) * g_ref[...]
    kr = z[:, KV_RANK:KV_RANK + QK_ROPE]
    kr_sw = z[:, KV_RANK + QK_ROPE:KV_RANK + 2 * QK_ROPE]
    kr_ref[...] = kr * cos_ref[...] + kr_sw * sin_ref[...]


def _post_small(z_small, g_kv, cos64, sin64, row0, rows):
    w = z_small.shape[1]
    tm = _pick(math.gcd(rows, row0), (1024, 512, 256, 128))
    b0 = row0 // tm
    return pl.pallas_call(
        _post_small_kernel,
        grid=(rows // tm,),
        in_specs=[pl.BlockSpec((tm, w), lambda i: (b0 + i, 0)),
                  pl.BlockSpec((1, KV_RANK), lambda i: (0, 0)),
                  pl.BlockSpec((tm, QK_ROPE), lambda i: (b0 + i, 0)),
                  pl.BlockSpec((tm, QK_ROPE), lambda i: (b0 + i, 0))],
        out_specs=[pl.BlockSpec((tm, KV_RANK), lambda i: (i, 0)),
                   pl.BlockSpec((tm, QK_ROPE), lambda i: (i, 0))],
        out_shape=[jax.ShapeDtypeStruct((rows, KV_RANK), F32),
                   jax.ShapeDtypeStruct((rows, QK_ROPE), F32)],
        compiler_params=_params("parallel"),
        name="latent_post",
    )(z_small, g_kv.reshape(1, KV_RANK), cos64, sin64)


def _q_prep_kernel(qn_ref, qr_ref, qsw_ref, cos_ref, sin_ref, o_ref):
    scale = QK_DIM ** -0.5 * math.log2(math.e)
    cos = jnp.tile(cos_ref[...], (1, N_HEADS))
    sin = jnp.tile(sin_ref[...], (1, N_HEADS))
    qr = (qr_ref[...].astype(F32) * cos + qsw_ref[...].astype(F32) * sin) * scale
    qn = qn_ref[...].astype(F32) * scale
    for h in range(N_HEADS):
        o_ref[h, :, 0:QK_NOPE] = qn[:, h * QK_NOPE:(h + 1) * QK_NOPE].astype(o_ref.dtype)
        o_ref[h, :, QK_NOPE:QK_DIM] = qr[:, h * QK_ROPE:(h + 1) * QK_ROPE].astype(o_ref.dtype)


def _q_prep(z_big, cos64, sin64, row0, rows):
    tm = _pick(math.gcd(rows, row0) if row0 else rows, (256, 128, 32))
    b0 = row0 // tm
    nope_blk = 3 * D_CONV // (N_HEADS * QK_NOPE)
    rope_blk = (3 * D_CONV + N_HEADS * QK_NOPE) // (N_HEADS * QK_ROPE)
    return pl.pallas_call(
        _q_prep_kernel,
        grid=(rows // tm,),
        in_specs=[pl.BlockSpec((tm, N_HEADS * QK_NOPE), lambda i: (b0 + i, nope_blk)),
                  pl.BlockSpec((tm, N_HEADS * QK_ROPE), lambda i: (b0 + i, rope_blk)),
                  pl.BlockSpec((tm, N_HEADS * QK_ROPE), lambda i: (b0 + i, rope_blk + 1)),
                  pl.BlockSpec((tm, QK_ROPE), lambda i: (b0 + i, 0)),
                  pl.BlockSpec((tm, QK_ROPE), lambda i: (b0 + i, 0))],
        out_specs=pl.BlockSpec((N_HEADS, tm, QK_DIM), lambda i: (0, i, 0)),
        out_shape=jax.ShapeDtypeStruct((N_HEADS, rows, QK_DIM), BF16),
        compiler_params=_params("parallel"),
        name="q_prep",
    )(z_big, z_big, z_big, cos64, sin64)


def _conv_kernel(state_ref, pc_ref, px_ref, b_ref, c_ref, x_ref, w_ref, o_ref, last_ref, *, tm, halo):
    i = pl.program_id(1)
    u = c_ref[...].astype(F32) * x_ref[...].astype(F32)
    prev = pc_ref[...].astype(F32) * px_ref[...].astype(F32)
    st = state_ref[...]
    first = i == 0
    um1 = jnp.where(first, st[7:8, :], prev[halo - 1:halo, :])
    um2 = jnp.where(first, st[6:7, :], prev[halo - 2:halo - 1, :])
    row = lax.broadcasted_iota(jnp.int32, u.shape, 0)
    s1 = jnp.where(row == 0, um1, pltpu.roll(u, 1, 0))
    s2 = jnp.where(row == 0, um2, jnp.where(row == 1, um1, pltpu.roll(u, 2, 0)))
    w = w_ref[...]
    y = w[0:1, :] * s2 + w[1:2, :] * s1 + w[2:3, :] * u
    o_ref[...] = (b_ref[...].astype(F32) * y).astype(o_ref.dtype)
    last_ref[...] = u[tm - 8:tm, :]


def _conv_gate(z_big, state, conv_w, row0, nseq, seq_len, into=None):
    tm = _pick(seq_len, (512, 256, 128, 32))
    halo = 16
    nb = seq_len // tm
    b0 = row0 // tm
    h0 = row0 // halo
    per = tm // halo
    d = D_CONV

    n_tail = _tail_blocks(into, row0 + nseq * seq_len, z_big.shape[0], tm)
    assert n_tail == 0 or nseq == 1
    real = lambda i: jnp.minimum(i, nb - 1)

    def hmap(col):
        return lambda s, i: (jnp.maximum(h0 + (s * nb + real(i)) * per - 1, 0), col)

    def bmap(col, clamp=True):
        return lambda s, i: (b0 + s * nb + (real(i) if clamp else i), col)

    return _fill_call(
        functools.partial(_conv_kernel, tm=tm, halo=halo), into=into, tail=(1, nb) if n_tail else None,
        grid=(nseq, nb + n_tail),
        in_specs=[pl.BlockSpec((None, 8, d), lambda s, i: (s, 0, 0)),
                  pl.BlockSpec((halo, d), hmap(1)),
                  pl.BlockSpec((halo, d), hmap(2)),
                  pl.BlockSpec((tm, d), bmap(0)),
                  pl.BlockSpec((tm, d), bmap(1)),
                  pl.BlockSpec((tm, d), bmap(2)),
                  pl.BlockSpec((8, d), lambda s, i: (0, 0))],
        args=(state, z_big, z_big, z_big, z_big, z_big, jnp.pad(conv_w, ((0, 5), (0, 0)))),
        out_specs=[pl.BlockSpec((tm, d), bmap(0, clamp=False)),
                   pl.BlockSpec((None, 8, d), lambda s, i: (s * nb + real(i), 0, 0))],
        out_shape=[jax.ShapeDtypeStruct((z_big.shape[0], d), BF16),
                   jax.ShapeDtypeStruct((nseq * nb, 8, d), F32)],
        compiler_params=_params("parallel", "arbitrary"),
        name="conv_gate",
    )


def _attn_latent_kernel(q_ref, wk_ref, wv_ref, cp_ref, krp_ref, cn_ref, krn_ref, o_ref, qa_ref, acc_ref,
                        *, past, tk):
    nh, dseq, _ = q_ref.shape
    rows = nh * dseq
    for h in range(nh):
        qh = q_ref[h]
        qa = lax.dot_general(qh[:, :QK_NOPE], wk_ref[h], NT_DIMS, preferred_element_type=F32)
        qa_ref[h * dseq:(h + 1) * dseq, 0:KV_RANK] = qa.astype(qa_ref.dtype)
        qa_ref[h * dseq:(h + 1) * dseq, KV_RANK:KV_RANK + QK_ROPE] = qh[:, QK_NOPE:QK_DIM]
    q = qa_ref[...]
    acc_ref[...] = jnp.zeros(acc_ref.shape, F32)

    def update(carry, s, c):
        m, l = carry
        m_new = jnp.maximum(m, jnp.max(s, axis=1, keepdims=True))
        p = jnp.exp2(s - m_new)
        alpha = jnp.exp2(m - m_new)
        acc_ref[...] = alpha * acc_ref[...] + jnp.dot(p.astype(BF16), c, preferred_element_type=F32)
        return m_new, alpha * l + jnp.sum(p, axis=1, keepdims=True)

    def past_step(j, carry):
        ks = pl.ds(pl.multiple_of(j * tk, tk), tk)
        c = cp_ref[ks, :].astype(BF16)
        k = jnp.concatenate([c, krp_ref[ks, :].astype(BF16)], axis=1)
        return update(carry, lax.dot_general(q, k, NT_DIMS, preferred_element_type=F32), c)

    carry = (jnp.full((rows, 1), MASK_VALUE, F32), jnp.zeros((rows, 1), F32))
    carry = lax.fori_loop(0, past // tk, past_step, carry)

    cn = cn_ref[...].astype(BF16)
    kn = jnp.concatenate([cn, krn_ref[...].astype(BF16)], axis=1)
    s = lax.dot_general(q, kn, NT_DIMS, preferred_element_type=F32)
    q_pos = past + lax.broadcasted_iota(jnp.int32, (rows, 1), 0) % dseq
    k_pos = past + lax.broadcasted_iota(jnp.int32, (1, dseq), 1)
    s = jnp.where(k_pos // CHUNK <= q_pos // CHUNK, s, MASK_VALUE)
    m, l = update(carry, s, cn)

    lat = (acc_ref[...] / l).astype(BF16)
    for h in range(nh):
        o = jnp.dot(lat[h * dseq:(h + 1) * dseq, :], wv_ref[h], preferred_element_type=F32)
        o_ref[:, h * V_DIM:(h + 1) * V_DIM] = o.astype(o_ref.dtype)


def _attention_latent(q, wk, wv, c_past, kr_past, c_new, kr_new, row0, into):
    nb, past, _ = c_past.shape
    dseq = c_new.shape[1]
    tk = _pick(past, (512, 256, 128))
    assert row0 % dseq == 0
    b0 = row0 // dseq
    per_b = lambda w: pl.BlockSpec((None,) + w, lambda b: (b, 0, 0))
    whole = lambda a: pl.BlockSpec(a.shape, lambda b: (0, 0, 0))
    return _fill_call(
        functools.partial(_attn_latent_kernel, past=past, tk=tk), into=into,
        grid=(nb,),
        in_specs=[pl.BlockSpec((N_HEADS, dseq, QK_DIM), lambda b: (0, b, 0)),
                  whole(wk), whole(wv),
                  per_b((past, KV_RANK)), per_b((past, QK_ROPE)),
                  per_b((dseq, KV_RANK)), per_b((dseq, QK_ROPE))],
        args=(q, wk, wv, c_past, kr_past, c_new, kr_new),
        out_specs=pl.BlockSpec((dseq, N_HEADS * V_DIM), lambda b: (b0 + b, 0)),
        out_shape=jax.ShapeDtypeStruct(into.shape, into.dtype),
        scratch_shapes=[pltpu.VMEM((N_HEADS * dseq, KV_RANK + QK_ROPE), BF16),
                        pltpu.VMEM((N_HEADS * dseq, KV_RANK), F32)],
        compiler_params=_params("parallel"),
        name="attention_latent",
    )


def _attn_t_kernel(q_ref, k_ref, kr_ref, vt_ref, o_ref, m_ref, l_ref, acc_ref, s0_ref, s1_ref, *, tb):
    i = pl.program_id(1)
    q = q_ref[...]
    q_pos = i * tb + lax.broadcasted_iota(jnp.int32, (1, tb), 1)
    col_end = (q_pos // CHUNK + 1) * CHUNK

    m_ref[...] = jnp.full(m_ref.shape, MASK_VALUE, F32)
    l_ref[...] = jnp.zeros(l_ref.shape, F32)
    acc_ref[...] = jnp.zeros(acc_ref.shape, F32)

    def produce(s_ref, j):
        ks = pl.ds(pl.multiple_of(j * tb, tb), tb)
        k = jnp.concatenate([k_ref[ks, :], kr_ref[ks, :]], axis=1)
        s_ref[...] = lax.dot_general(k, q, NT_DIMS, preferred_element_type=F32)

    def consume(s_ref, j, masked):
        s = s_ref[...]
        if masked:
            k_pos = j * tb + lax.broadcasted_iota(jnp.int32, (tb, 1), 0)
            s = jnp.where(k_pos < col_end, s, MASK_VALUE)
        m_old = m_ref[...]
        m_new = jnp.maximum(m_old, jnp.max(s, axis=0, keepdims=True))
        p = jnp.exp2(s - m_new)
        alpha = jnp.exp2(m_old - m_new)
        m_ref[...] = m_new
        l_ref[...] = alpha * l_ref[...] + jnp.sum(p, axis=0, keepdims=True)
        acc_ref[...] = alpha * acc_ref[...] + jnp.dot(vt_ref[j], p.astype(BF16), preferred_element_type=F32)

    produce(s0_ref, 0)

    bufs = (s0_ref, s1_ref)
    unroll = 8

    def group(g, carry):
        j0 = unroll * g
        for t in range(unroll):
            produce(bufs[(t + 1) % 2], j0 + t + 1)
            consume(bufs[t % 2], j0 + t, False)
        return carry

    lax.fori_loop(0, i // unroll, group, 0)
    j0 = i // unroll * unroll

    for rem in range(unroll):
        @pl.when(i - j0 == rem)
        def _(rem=rem):
            for t in range(rem + 1):
                if t < rem:
                    produce(bufs[(t + 1) % 2], j0 + t + 1)
                consume(bufs[t % 2], j0 + t, t == rem)

    o_ref[...] = (acc_ref[...] / l_ref[...]).T.astype(o_ref.dtype)


def _attention_t(q, k, kr, vt, seq, tk, out_rows):
    tq = tk
    assert tk % CHUNK == 0 and seq % tk == 0
    nq = seq // tq
    n_tail = _tail_blocks(None, seq, out_rows, tq)
    return _fill_call(
        functools.partial(_attn_t_kernel, tb=tk), into=None, tail=(1, nq) if n_tail else None,
        grid=(N_HEADS, nq + n_tail),
        in_specs=[pl.BlockSpec((None, tq, QK_DIM), lambda h, i: (h, jnp.minimum(i, nq - 1), 0)),
                  pl.BlockSpec((None, seq, QK_NOPE), lambda h, i: (h, 0, 0)),
                  pl.BlockSpec((seq, QK_ROPE), lambda h, i: (0, 0)),
                  pl.BlockSpec((None, seq // tk, V_DIM, tk), lambda h, i: (h, 0, 0, 0))],
        args=(q, k, kr, vt),
        out_specs=pl.BlockSpec((tq, V_DIM), lambda h, i: (i, h)),
        out_shape=jax.ShapeDtypeStruct((out_rows, N_HEADS * V_DIM), BF16),
        scratch_shapes=[pltpu.VMEM((1, tq), F32), pltpu.VMEM((1, tq), F32), pltpu.VMEM((V_DIM, tq), F32),
                        pltpu.VMEM((tk, tq), F32), pltpu.VMEM((tk, tq), F32)],
        compiler_params=_params("parallel", "arbitrary"),
        name="attention_t",
    )


def _merge_kernel(a_ref, o_ref, wa_ref, wb_ref, ga_ref, gb_ref, m_ref):
    ya = jnp.dot(a_ref[...], wa_ref[...], preferred_element_type=F32)
    yb = jnp.dot(o_ref[...], wb_ref[...], preferred_element_type=F32)
    m = jax.nn.sigmoid(ga_ref[...].astype(F32)) * ya + jax.nn.sigmoid(gb_ref[...].astype(F32)) * yb
    m_ref[...] = m.astype(m_ref.dtype)


def _merge(ya_in, o, wa, wb, z_big):
    t, k = ya_in.shape
    n = wa.shape[1]
    tm = _pick(t, (1280, 1024, 512, 256, 128))
    tn = _pick(n, (512, 256, 128))
    ga_blk = (3 * D_CONV + N_HEADS * (QK_NOPE + 2 * QK_ROPE)) // tn
    gb_blk = ga_blk + D_MODEL // tn
    return pl.pallas_call(
        _merge_kernel,
        grid=(t // tm, n // tn),
        in_specs=[pl.BlockSpec((tm, k), lambda i, j: (i, 0)),
                  pl.BlockSpec((tm, k), lambda i, j: (i, 0)),
                  pl.BlockSpec((k, tn), lambda i, j: (0, j)),
                  pl.BlockSpec((k, tn), lambda i, j: (0, j)),
                  pl.BlockSpec((tm, tn), lambda i, j: (i, ga_blk + j)),
                  pl.BlockSpec((tm, tn), lambda i, j: (i, gb_blk + j))],
        out_specs=pl.BlockSpec((tm, tn), lambda i, j: (i, j)),
        out_shape=jax.ShapeDtypeStruct((t, n), BF16),
        compiler_params=_params("parallel", "arbitrary"),
        name="merge",
    )(ya_in, o, wa, wb, z_big, z_big)


def _extract_top(work_ref, rank_ref, val_ref, idx_ref, *, track_rank):
    nk = PEER_NKEYS
    tm = work_ref.shape[1]
    neg = jnp.full((8, tm), -jnp.inf, F32)

    def round_(p, carry):
        idx_prev, pf = carry
        parts = [neg, neg, neg, neg]
        for k in range(nk):
            rows = slice(8 * k, 8 * k + 8)
            hit = idx_prev == float(k)
            w = jnp.where(hit, -jnp.inf, work_ref[rows, :])
            work_ref[rows, :] = w
            if track_rank:
                rank_ref[rows, :] = jnp.where(hit, pf - 1.0, rank_ref[rows, :])
            parts[k % 4] = jnp.maximum(parts[k % 4], w)
        m = jnp.maximum(jnp.maximum(parts[0], parts[1]), jnp.maximum(parts[2], parts[3]))
        big = jnp.full((8, tm), float(nk), F32)
        iparts = [big, big, big, big]
        for k in range(nk):
            w = work_ref[8 * k:8 * k + 8, :]
            iparts[k % 4] = jnp.minimum(iparts[k % 4], jnp.where(w == m, float(k), float(nk)))
        idx = jnp.minimum(jnp.minimum(iparts[0], iparts[1]), jnp.minimum(iparts[2], iparts[3]))
        val_ref[p] = m
        idx_ref[p] = idx
        return idx, pf + 1.0

    idx_last, _ = lax.fori_loop(0, PEER_TOPK, round_,
                                (jnp.full((8, tm), -1.0, F32), jnp.zeros((8, tm), F32)))
    if track_rank:
        for k in range(nk):
            rows = slice(8 * k, 8 * k + 8)
            rank_ref[rows, :] = jnp.where(idx_last == float(k), float(PEER_TOPK - 1), rank_ref[rows, :])


def _peer_select_kernel(qp_ref, k1_ref, k2_ref, k2h_ref, perm_ref,
                        r2_ref, e2_ref, n1_ref, w1_ref,
                        work_ref, rank_ref, a_ref, ia_ref, b_ref, ib_ref):
    nk = PEER_NKEYS
    qp = qp_ref[...]
    tm = qp.shape[0]
    s2h = lax.dot_general(k2h_ref[...], qp, NT_DIMS, preferred_element_type=F32)
    for h in range(PEER_HEADS):
        blk = s2h[h * nk:(h + 1) * nk, :]
        e2_ref[h * nk:(h + 1) * nk, :] = jnp.exp(blk - jnp.max(blk, axis=0, keepdims=True))

    work_ref[...] = lax.dot_general(k1_ref[...], qp, NT_DIMS, preferred_element_type=F32)
    _extract_top(work_ref, rank_ref, a_ref, ia_ref, track_rank=False)
    work_ref[...] = lax.dot_general(k2_ref[...], qp, NT_DIMS, preferred_element_type=F32)
    rank_ref[...] = jnp.full(rank_ref.shape, float(PEER_TOPK), F32)
    _extract_top(work_ref, rank_ref, b_ref, ib_ref, track_rank=True)
    r2_ref[...] = jnp.dot(perm_ref[...], rank_ref[...].astype(BF16), preferred_element_type=F32)

    a = [a_ref[p] for p in range(PEER_TOPK)]
    b = [b_ref[q] for q in range(PEER_TOPK)]
    cand = [a[p] + b[q] for (p, q) in _PAIRS]
    npair = len(_PAIRS)
    beaten = [jnp.zeros((8, tm), F32) for _ in range(npair)]
    for x in range(npair):
        px, qx = _PAIRS[x]
        for y in range(x + 1, npair):
            py, qy = _PAIRS[y]
            if px <= py and qx <= qy:
                beaten[y] = beaten[y] + 1.0
            else:
                gt = jnp.where(cand[y] > cand[x], 1.0, 0.0)
                beaten[x] = beaten[x] + gt
                beaten[y] = beaten[y] + (1.0 - gt)
    sel = [jnp.where(bt < float(PEER_TOPK), 1.0, 0.0) for bt in beaten]
    ea = [jnp.exp(a[p] - a[0]) for p in range(PEER_TOPK)]
    eb = [jnp.exp(b[q] - b[0]) for q in range(PEER_TOPK)]
    z = jnp.zeros((8, tm), F32)
    cnt = [jnp.zeros((8, tm), F32) for _ in range(PEER_TOPK)]
    for x, (p, q) in enumerate(_PAIRS):
        z = z + sel[x] * (ea[p] * eb[q])
        cnt[p] = cnt[p] + sel[x]
    inv_z = 1.0 / z
    w1 = [ea[p] * inv_z for p in range(PEER_TOPK)]
    ia = [ia_ref[p] for p in range(PEER_TOPK)]
    zero = jnp.zeros((8, tm), F32)
    for k in range(nk):
        n1k = zero
        w1k = zero
        for p in range(PEER_TOPK):
            hit = ia[p] == float(k)
            n1k = jnp.where(hit, cnt[p], n1k)
            w1k = jnp.where(hit, w1[p], w1k)
        n1_ref[8 * k:8 * k + 8, :] = n1k
        w1_ref[8 * k:8 * k + 8, :] = w1k


def _peer_select(qp, k1, k2, k2h, perm):
    t, d = qp.shape
    tm = _pick(t, (256, 128))
    rows = PEER_NKEYS * PEER_HEADS
    full = lambda i: (0, 0)
    out = jax.ShapeDtypeStruct((rows, t), F32)
    ospec = pl.BlockSpec((rows, tm), lambda i: (0, i))
    return pl.pallas_call(
        _peer_select_kernel,
        grid=(t // tm,),
        in_specs=[pl.BlockSpec((tm, d), lambda i: (i, 0)),
                  pl.BlockSpec((rows, d), full),
                  pl.BlockSpec((rows, d), full),
                  pl.BlockSpec((rows, d), full),
                  pl.BlockSpec((rows, rows), full)],
        out_specs=[ospec, ospec, ospec, ospec],
        out_shape=[out, out, out, out],
        scratch_shapes=[pltpu.VMEM((rows, tm), F32), pltpu.VMEM((rows, tm), F32),
                        pltpu.VMEM((PEER_TOPK, 8, tm), F32), pltpu.VMEM((PEER_TOPK, 8, tm), F32),
                        pltpu.VMEM((PEER_TOPK, 8, tm), F32), pltpu.VMEM((PEER_TOPK, 8, tm), F32)],
        compiler_params=_params("parallel"),
        name="peer_select",
    )(qp, k1, k2, k2h, perm)


def _gelu_tanh(x):
    c = math.sqrt(2.0 / math.pi)
    return 0.5 * x * (1.0 + jnp.tanh(c * (x + 0.044715 * (x * x * x))))


def _peer_main_kernel(hn_ref, u_ref, va_ref, vb_ref, r2_ref, e2_ref, n1_ref, w1_ref,
                      o_ref, ga_ref, gb_ref, pa_ref, pb_ref, act_ref, *, tc):
    nk = PEER_NKEYS
    te, tm = ga_ref.shape
    j = pl.program_id(1)
    last = pl.num_programs(1) - 1

    hn = hn_ref[...]

    nw = 256
    d = o_ref.shape[1]

    n_il = te // nk
    jr = 64

    def gate_tile(g_ref, nref, wref, half, c, jh):
        cols = slice(c * tc, (c + 1) * tc)
        accs = [jnp.zeros((jr, tc), F32) for _ in range(n_il)]
        for h in range(PEER_HEADS):
            hr = slice(h * nk + jh * jr, h * nk + (jh + 1) * jr)
            r2 = r2_ref[hr, cols]
            e2 = e2_ref[hr, cols]
            for il in range(n_il):
                r = (half * n_il + il) * PEER_HEADS + h
                keep = r2 < nref[r:r + 1, cols]
                accs[il] = accs[il] + jnp.where(keep, e2, 0.0) * wref[r:r + 1, cols]
        for il in range(n_il):
            g_ref[il * nk + jh * jr:il * nk + (jh + 1) * jr, cols] = accs[il]

    def gate(g_ref, nref, wref, half):
        for c in range(tm // tc):
            for jh in range(nk // jr):
                gate_tile(g_ref, nref, wref, half, c, jh)

    @pl.when(j == 0)
    def _():
        o_ref[...] = jnp.zeros(o_ref.shape, o_ref.dtype)
        pb_ref[...] = jnp.zeros(pb_ref.shape, pb_ref.dtype)

    def act_piece(half, n):
        u = u_ref[half * te + n * nw:half * te + (n + 1) * nw, :]
        return _gelu_tanh(lax.dot_general(hn, u, NT_DIMS, preferred_element_type=F32))

    def out_piece(p_ref, v_ref, n):
        cols = slice(n * nw, (n + 1) * nw)
        o_ref[:, cols] += jnp.dot(p_ref[...], v_ref[:, cols], preferred_element_type=F32)

    def sub_block(half, g_ref, p_new_ref, p_old_ref, v_old_ref, next_gate):
        for n in range(te // nw):
            act_ref[:, n * nw:(n + 1) * nw] = act_piece(half, n)
        for n in range(d // nw):
            out_piece(p_old_ref, v_old_ref, n)
        gate(*next_gate)
        for il in range(n_il):
            ecols = slice(il * nk, (il + 1) * nk)
            for c in range(tm // tc):
                rows = slice(c * tc, (c + 1) * tc)
                g = g_ref[ecols, rows].T
                p_new_ref[rows, ecols] = (g * act_ref[rows, ecols]).astype(p_new_ref.dtype)

    @pl.when(j < last)
    def _():
        sub_block(0, ga_ref, pa_ref, pb_ref, vb_ref, (ga_ref, n1_ref, w1_ref, 0))
        sub_block(1, gb_ref, pb_ref, pa_ref, va_ref, (gb_ref, n1_ref, w1_ref, 1))

    @pl.when(j == last)
    def _():
        for n in range(d // nw):
            out_piece(pb_ref, vb_ref, n)


def _peer_main(hn, u, v, r2, e2, n1, w1):
    t, d = hn.shape
    e = u.shape[0]
    tm = _pick(t, (640, 512, 256, 128))
    te = 512
    tc = 128
    rows = PEER_NKEYS * PEER_HEADS
    sub = 2 * te // PEER_NKEYS * PEER_HEADS
    nj = e // (2 * te)
    cur = lambda j: jnp.minimum(j, nj - 1)
    return pl.pallas_call(
        functools.partial(_peer_main_kernel, tc=tc),
        grid=(t // tm, nj + 1),
        in_specs=[pl.BlockSpec((tm, d), lambda i, j: (i, 0)),
                  pl.BlockSpec((2 * te, d), lambda i, j: (cur(j), 0)),
                  pl.BlockSpec((te, d), lambda i, j: (2 * cur(j), 0)),
                  pl.BlockSpec((te, d), lambda i, j: (jnp.maximum(2 * j - 1, 0), 0)),
                  pl.BlockSpec((rows, tm), lambda i, j: (0, i)),
                  pl.BlockSpec((rows, tm), lambda i, j: (0, i)),
                  pl.BlockSpec((sub, tm), lambda i, j: (cur(j), i)),
                  pl.BlockSpec((sub, tm), lambda i, j: (cur(j), i))],
        out_specs=pl.BlockSpec((tm, d), lambda i, j: (i, 0)),
        out_shape=jax.ShapeDtypeStruct((t, d), F32),
        scratch_shapes=[pltpu.VMEM((te, tm), F32), pltpu.VMEM((te, tm), F32),
                        pltpu.VMEM((tm, te), BF16), pltpu.VMEM((tm, te), BF16),
                        pltpu.VMEM((tm, te), F32)],
        compiler_params=_params("parallel", "arbitrary"),
        name="peer_main",
    )(hn, u, v, v, r2, e2, n1, w1)


def _ple_kernel(xn_ref, p_ref, wg_ref, wp_ref, h_ref, o_ref):
    gate = jnp.dot(xn_ref[...], wg_ref[...], preferred_element_type=F32)
    pe = jnp.dot(p_ref[...].astype(BF16), wp_ref[...], preferred_element_type=F32)
    o_ref[...] = h_ref[...] + pe * jax.nn.sigmoid(gate)


def _ple(xn, p, wg, wp, h, row0, into=None):
    t, d = xn.shape
    rows, pd = p.shape
    tm = _pick(math.gcd(rows, row0), (1024, 512, 256, 128))
    tn = _pick(d, (512, 256, 128))
    b0 = row0 // tm
    nb = rows // tm
    n_tail = _tail_blocks(into, row0 + rows, t, tm)
    real = lambda i: jnp.minimum(i, nb - 1)
    return _fill_call(
        _ple_kernel, into=into, tail=(0, nb) if n_tail else None,
        grid=(nb + n_tail, d // tn),
        in_specs=[pl.BlockSpec((tm, d), lambda i, j: (b0 + real(i), 0)),
                  pl.BlockSpec((tm, pd), lambda i, j: (real(i), 0)),
                  pl.BlockSpec((d, tn), lambda i, j: (0, j)),
                  pl.BlockSpec((pd, tn), lambda i, j: (0, j)),
                  pl.BlockSpec((tm, tn), lambda i, j: (b0 + real(i), j))],
        args=(xn, p, wg, wp, h),
        out_specs=pl.BlockSpec((tm, tn), lambda i, j: (b0 + i, j)),
        out_shape=jax.ShapeDtypeStruct((t, d), F32),
        compiler_params=_params("parallel", "arbitrary"),
        name="ple",
    )


def _split_w_in(w_in):
    off = [0] + [int(s) for s in np.cumsum(IN_SIZES)]
    w16 = w_in.astype(BF16)
    col = lambda a, b: w16[:, a:b]
    half = QK_ROPE // 2
    q0 = off[3]
    nope, rope, rope_sw = [], [], []
    for h in range(N_HEADS):
        r0 = q0 + h * QK_DIM + QK_NOPE
        nope.append(col(q0 + h * QK_DIM, r0))
        rope.append(col(r0, r0 + QK_ROPE))
        rope_sw += [col(r0 + half, r0 + QK_ROPE), col(r0, r0 + half)]
    w_big = jnp.concatenate([col(off[0], off[3])] + nope + rope + rope_sw + [col(off[6], off[8])], axis=1)
    k0 = off[5]
    w_small = jnp.concatenate([col(off[4], off[6]), col(k0 + half, k0 + QK_ROPE), col(k0, k0 + half)], axis=1)
    return w_big, w_small


def _rope_tables(pos):
    inv = 1.0 / (ROPE_THETA ** (jnp.arange(0, QK_ROPE, 2, dtype=F32) / QK_ROPE))
    ang = pos.astype(F32)[:, None] * inv[None, :]
    cos, sin = jnp.cos(ang), jnp.sin(ang)
    return jnp.concatenate([cos, cos], axis=1), jnp.concatenate([-sin, sin], axis=1)


def _key_matrices(sub_keys):
    nk, hd, half = PEER_NKEYS, PEER_HEADS, PEER_DKEY // 2
    eye = jnp.eye(hd, dtype=F32)

    def build(c, head_major):
        sel = jnp.zeros((2,), F32).at[c].set(1.0)
        m = sub_keys[c][:, None, None, None, :] * eye[None, :, :, None, None] * sel[None, None, None, :, None]
        if head_major:
            m = jnp.transpose(m, (1, 0, 2, 3, 4))
        return m.reshape(nk * hd, hd * PEER_DKEY).astype(BF16)

    r = np.arange(nk * hd)
    perm = np.zeros((nk * hd, nk * hd), np.float32)
    perm[(r % hd) * nk + r // hd, r] = 1.0
    return build(0, False), build(1, False), build(1, True), jnp.asarray(perm, BF16)


def kernel(x_prompt, x_sample, cache_conv, cache_ckv, cache_krope, p_prompt, p_sample, g_mix, w_in, conv_w, g_kv, w_kv_b, w_a_out, w_b_out, w_o, g_ffn, w_pq, sub_keys, u_tab, v_tab, g_ple, w_ple_gate, w_ple, g_final):
    assert x_prompt.shape[0] == 1 and w_in.shape[0] == 1
    seq = x_prompt.shape[1]
    nb, dseq = x_sample.shape[0], x_sample.shape[1]
    past = cache_ckv.shape[2]
    d = D_MODEL
    ns = nb * dseq
    t = seq + ns

    x_p, x_s = x_prompt.reshape(seq, d), x_sample.reshape(ns, d)
    p_p, p_s = p_prompt[0].reshape(seq, -1), p_sample[0].reshape(ns, -1)
    pos = jnp.concatenate([jnp.arange(seq), jnp.tile(past + jnp.arange(dseq), nb)])
    cos64, sin64 = _rope_tables(pos)
    w_big, w_small = _split_w_in(w_in[0])

    xn = _rmsnorm(x_p, g_mix[0], BF16, out_rows=t)
    xn = _rmsnorm(x_s, g_mix[0], BF16, out_row0=seq, out_rows=t, into=xn)
    z_big = _matmul(xn, w_big, BF16, name="in_proj")
    z_small = _matmul(xn, w_small, F32, tn_prefs=(w_small.shape[1],), name="in_proj_small")
    c_p, kr_p = _post_small(z_small, g_kv[0], cos64, sin64, 0, seq)
    c_s, kr_s = _post_small(z_small, g_kv[0], cos64, sin64, seq, ns)

    zero_state = jnp.zeros((1, 8, D_CONV), F32)
    ya_in, last_p = _conv_gate(z_big, zero_state, conv_w[0], 0, 1, seq)
    state_s = jnp.pad(cache_conv[0], ((0, 0), (6, 0), (0, 0)))
    ya_in, last_s = _conv_gate(z_big, state_s, conv_w[0], seq, nb, dseq, into=ya_in)
    conv_p = last_p[-1, 6:8, :].reshape(1, 1, 2, D_CONV)
    if dseq >= 2:
        conv_s = last_s[:, 6:8, :].reshape(1, nb, 2, D_CONV)
    else:
        raise NotImplementedError("sample blocks shorter than the convolution state")

    w_kv_h = w_kv_b[0].astype(BF16).reshape(KV_RANK, N_HEADS, QK_NOPE + V_DIM)
    wk_h = jnp.transpose(w_kv_h[:, :, :QK_NOPE], (1, 0, 2))
    wv_h = jnp.transpose(w_kv_h[:, :, QK_NOPE:], (1, 0, 2))
    wk = w_kv_h[:, :, :QK_NOPE].reshape(KV_RANK, N_HEADS * QK_NOPE)
    wvt = w_kv_h[:, :, QK_NOPE:].reshape(KV_RANK, N_HEADS * V_DIM).T
    q_p = _q_prep(z_big, cos64, sin64, 0, seq)
    q_s = _q_prep(z_big, cos64, sin64, seq, ns)
    tk = _pick(seq, (512, 256, 128))
    k_p, vt_p = _kv_proj_t(c_p, wk, wvt, tk)
    o_all = _attention_t(q_p, k_p, kr_p.astype(BF16), vt_p, seq, tk, t)
    o_all = _attention_latent(q_s, wk_h, wv_h, cache_ckv[0], cache_krope[0],
                              c_s.reshape(nb, dseq, KV_RANK), kr_s.reshape(nb, dseq, QK_ROPE), seq, o_all)

    m = _merge(ya_in, o_all, w_a_out[0].astype(BF16), w_b_out[0].astype(BF16), z_big)
    w_out = w_o[0].astype(BF16)
    h1 = _matmul(m, w_out, F32, residual=x_p, name="out_proj", rows=seq)
    h1 = _matmul(m, w_out, F32, residual=x_s, name="out_proj", row0=seq, into=h1)

    hn = _rmsnorm(h1, g_ffn[0], BF16)
    qp = _matmul(hn, w_pq[0].astype(BF16), BF16, name="peer_query")
    k1, k2, k2h, perm = _key_matrices(sub_keys[0])
    r2, e2, n1, w1 = _peer_select(qp, k1, k2, k2h, perm)
    peer = _peer_main(hn, u_tab[0].astype(BF16), v_tab[0].astype(BF16), r2, e2, n1, w1)

    h2, x2n = _add_rmsnorm(h1, peer, g_ple[0], BF16)
    wg, wp = w_ple_gate[0].astype(BF16), w_ple[0].astype(BF16)
    h3 = _ple(x2n, p_p, wg, wp, h2, 0)
    h3 = _ple(x2n, p_s, wg, wp, h2, seq, into=h3)
    y_prompt = _rmsnorm(h3, g_final, F32, rows=seq).reshape(1, seq, d)
    y_sample = _rmsnorm(h3, g_final, F32, in_row0=seq).reshape(nb, dseq, d)

    ckv_p = c_p.reshape(1, 1, seq, KV_RANK)
    kr_p = kr_p.reshape(1, 1, seq, QK_ROPE)
    ckv_s = c_s.reshape(1, nb, dseq, KV_RANK)
    kr_s = kr_s.reshape(1, nb, dseq, QK_ROPE)
    return (y_prompt, y_sample, conv_p, ckv_p, kr_p, conv_s, ckv_s, kr_s)
```

```python
import functools
import math

import numpy as np
import jax
import jax.numpy as jnp
from jax import lax
from jax.experimental import pallas as pl
from jax.experimental.pallas import tpu as pltpu

D_MODEL = 2048
D_CONV = 2048
N_HEADS = 16
QK_NOPE = 128
QK_ROPE = 64
QK_DIM = QK_NOPE + QK_ROPE
V_DIM = 128
KV_RANK = 512
CHUNK = 64
ROPE_THETA = 10000.0
PEER_HEADS = 8
PEER_NKEYS = 128
PEER_TOPK = 16
PEER_DKEY = 256
PEER_EXPERTS = PEER_NKEYS * PEER_NKEYS
RMS_EPS = 1e-6
IN_SIZES = (D_CONV, D_CONV, D_CONV, N_HEADS * QK_DIM, KV_RANK, QK_ROPE, D_MODEL, D_MODEL)

BF16 = jnp.bfloat16
F32 = jnp.float32
VMEM_LIMIT_BYTES = 56 * 1024 * 1024
MASK_VALUE = -1e30
NT_DIMS = (((1,), (1,)), ((), ()))
TN_DIMS = (((0,), (0,)), ((), ()))

_PAIRS = tuple((p, q) for p in range(PEER_TOPK) for q in range(PEER_TOPK)
               if (p + 1) * (q + 1) <= PEER_TOPK)


def _pick(n, prefs):
    for p in prefs:
        if n % p == 0:
            return p
    return n


def _params(*sem):
    return pltpu.CompilerParams(dimension_semantics=sem, vmem_limit_bytes=VMEM_LIMIT_BYTES)


def _rms_kernel(x_ref, g_ref, o_ref):
    x = x_ref[...]
    ms = jnp.mean(x * x, axis=-1, keepdims=True)
    o_ref[...] = (x * lax.rsqrt(ms + RMS_EPS) * g_ref[...]).astype(o_ref.dtype)


def _fill_call(kern, *, into, in_specs, args, tail=None, n_shared=1, **kwargs):
    if tail is not None:
        axis, n_real = tail
        n_in = len(in_specs)
        body = kern

        def kern(*refs):
            i = pl.program_id(axis)

            @pl.when(i < n_real)
            def _():
                body(*refs)

            @pl.when(i >= n_real)
            def _():
                for o in refs[n_in:n_in + n_shared]:
                    o[...] = jnp.zeros(o.shape, o.dtype)

    if into is None:
        return pl.pallas_call(kern, in_specs=in_specs, **kwargs)(*args)
    bufs = tuple(into) if isinstance(into, (tuple, list)) else (into,)
    assert len(bufs) == n_shared

    def filling(*refs):
        kern(*refs[n_shared:])

    return pl.pallas_call(filling, in_specs=[pl.BlockSpec(memory_space=pl.ANY)] * n_shared + list(in_specs),
                          input_output_aliases={k: k for k in range(n_shared)}, **kwargs)(*bufs, *args)


def _tail_blocks(into, end_row, total_rows, tm):
    if into is not None or end_row >= total_rows:
        return 0
    return -(-(total_rows - end_row) // tm)


def _rmsnorm(x, g, out_dtype, out_row0, out_rows, into=None):
    rows, d = x.shape
    tm = _pick(math.gcd(rows, out_row0), (1024, 640, 512, 256, 128))
    bo = out_row0 // tm
    nb = rows // tm
    n_tail = _tail_blocks(into, out_row0 + rows, out_rows, tm)
    return _fill_call(
        _rms_kernel, into=into, tail=(0, nb) if n_tail else None,
        grid=(nb + n_tail,),
        in_specs=[pl.BlockSpec((tm, d), lambda i: (jnp.minimum(i, nb - 1), 0)),
                  pl.BlockSpec((1, d), lambda i: (0, 0))],
        args=(x, g.reshape(1, d)),
        out_specs=pl.BlockSpec((tm, d), lambda i: (bo + i, 0)),
        out_shape=jax.ShapeDtypeStruct((out_rows, d), out_dtype),
        compiler_params=_params("parallel"),
        name="rmsnorm",
    )


def _add_rms_kernel(a_ref, b_ref, g_ref, s_ref, o_ref):
    x = a_ref[...] + b_ref[...]
    s_ref[...] = x
    ms = jnp.mean(x * x, axis=-1, keepdims=True)
    o_ref[...] = (x * lax.rsqrt(ms + RMS_EPS) * g_ref[...]).astype(o_ref.dtype)


def _add_rmsnorm(a, b, g, out_dtype):
    t, d = a.shape
    tm = _pick(t, (640, 512, 256, 128))
    row = pl.BlockSpec((tm, d), lambda i: (i, 0))
    return pl.pallas_call(
        _add_rms_kernel,
        grid=(t // tm,),
        in_specs=[row, row, pl.BlockSpec((1, d), lambda i: (0, 0))],
        out_specs=[row, row],
        out_shape=[jax.ShapeDtypeStruct((t, d), F32), jax.ShapeDtypeStruct((t, d), out_dtype)],
        compiler_params=_params("parallel"),
        name="add_rmsnorm",
    )(a, b, g.reshape(1, d))


def _mm_kernel(x_ref, w_ref, o_ref):
    o_ref[...] = jnp.dot(x_ref[...], w_ref[...], preferred_element_type=F32).astype(o_ref.dtype)


def _matmul(x, w, out_dtype, tn_prefs=(1024, 512, 256, 128), name="matmul"):
    t, k = x.shape
    n = w.shape[1]
    tm = _pick(t, (1280, 1024, 768, 512, 256, 128))
    tn = _pick(n, tn_prefs)
    return pl.pallas_call(
        _mm_kernel,
        grid=(t // tm, n // tn),
        in_specs=[pl.BlockSpec((tm, k), lambda i, j: (i, 0)),
                  pl.BlockSpec((k, tn), lambda i, j: (0, j))],
        out_specs=pl.BlockSpec((tm, tn), lambda i, j: (i, j)),
        out_shape=jax.ShapeDtypeStruct((t, n), out_dtype),
        compiler_params=_params("parallel", "arbitrary"),
        name=name,
    )(x, w)


def _out_proj_norm_kernel(m_ref, w_ref, x_ref, g_ref, h_ref, hn_ref):
    h = x_ref[...] + jnp.dot(m_ref[...], w_ref[...], preferred_element_type=F32)
    h_ref[...] = h
    ms = jnp.mean(h * h, axis=-1, keepdims=True)
    hn_ref[...] = (h * lax.rsqrt(ms + RMS_EPS) * g_ref[...]).astype(hn_ref.dtype)


def _out_proj_norm(m, w, x_seg, g, row0, into=None):
    t, k = m.shape
    d = w.shape[1]
    rows = x_seg.shape[0]
    tm = _pick(math.gcd(rows, row0), (512, 256, 128))
    b0, nb = row0 // tm, rows // tm
    n_tail = _tail_blocks(into, row0 + rows, t, tm)
    real = lambda i: jnp.minimum(i, nb - 1)
    whole = lambda i: (0, 0)
    row = pl.BlockSpec((tm, d), lambda i: (b0 + i, 0))
    return _fill_call(
        _out_proj_norm_kernel, into=into, tail=(0, nb) if n_tail else None, n_shared=2,
        grid=(nb + n_tail,),
        in_specs=[pl.BlockSpec((tm, k), lambda i: (b0 + real(i), 0)),
                  pl.BlockSpec((k, d), whole),
                  pl.BlockSpec((tm, d), lambda i: (real(i), 0)),
                  pl.BlockSpec((1, d), whole)],
        args=(m, w, x_seg, g.reshape(1, d)),
        out_specs=[row, row],
        out_shape=[jax.ShapeDtypeStruct((t, d), F32), jax.ShapeDtypeStruct((t, d), BF16)],
        compiler_params=_params("parallel"),
        name="out_proj",
    )


def _kv_proj_t_kernel(c_ref, wk_ref, wvt_ref, k_ref, vt_ref):
    c = c_ref[...].astype(BF16)
    k = jnp.dot(c, wk_ref[...], preferred_element_type=F32)
    vt = lax.dot_general(wvt_ref[...], c, NT_DIMS, preferred_element_type=F32)
    for h in range(N_HEADS):
        k_ref[h] = k[:, h * QK_NOPE:(h + 1) * QK_NOPE].astype(k_ref.dtype)
        vt_ref[h] = vt[h * V_DIM:(h + 1) * V_DIM, :].astype(vt_ref.dtype)


def _kv_proj_t(c, wk, wvt, tk):
    r, kd = c.shape
    return pl.pallas_call(
        _kv_proj_t_kernel,
        grid=(r // tk,),
        in_specs=[pl.BlockSpec((tk, kd), lambda i: (i, 0)),
                  pl.BlockSpec((kd, N_HEADS * QK_NOPE), lambda i: (0, 0)),
                  pl.BlockSpec((N_HEADS * V_DIM, kd), lambda i: (0, 0))],
        out_specs=[pl.BlockSpec((N_HEADS, tk, QK_NOPE), lambda i: (0, i, 0)),
                   pl.BlockSpec((N_HEADS, None, V_DIM, tk), lambda i: (0, i, 0, 0))],
        out_shape=[jax.ShapeDtypeStruct((N_HEADS, r, QK_NOPE), BF16),
                   jax.ShapeDtypeStruct((N_HEADS, r // tk, V_DIM, tk), BF16)],
        compiler_params=_params("parallel"),
        name="kv_proj_t",
    )(c, wk, wvt)


def _post_small_kernel(z_ref, g_ref, cos_ref, sin_ref, c_ref, kr_ref):
    z = z_ref[...]
    ckv = z[:, :KV_RANK]
    ms = jnp.mean(ckv * ckv, axis=-1, keepdims=True)
    c_ref[...] = ckv * lax.rsqrt(ms + RMS_EPS) * g_ref[...]
    kr = z[:, KV_RANK:KV_RANK + QK_ROPE]
    kr_sw = z[:, KV_RANK + QK_ROPE:KV_RANK + 2 * QK_ROPE]
    kr_ref[...] = kr * cos_ref[...] + kr_sw * sin_ref[...]


def _post_small(z_small, g_kv, cos64, sin64, row0, rows):
    w = z_small.shape[1]
    tm = _pick(math.gcd(rows, row0), (1024, 512, 256, 128))
    b0 = row0 // tm
    return pl.pallas_call(
        _post_small_kernel,
        grid=(rows // tm,),
        in_specs=[pl.BlockSpec((tm, w), lambda i: (b0 + i, 0)),
                  pl.BlockSpec((1, KV_RANK), lambda i: (0, 0)),
                  pl.BlockSpec((tm, QK_ROPE), lambda i: (b0 + i, 0)),
                  pl.BlockSpec((tm, QK_ROPE), lambda i: (b0 + i, 0))],
        out_specs=[pl.BlockSpec((tm, KV_RANK), lambda i: (i, 0)),
                   pl.BlockSpec((tm, QK_ROPE), lambda i: (i, 0))],
        out_shape=[jax.ShapeDtypeStruct((rows, KV_RANK), F32),
                   jax.ShapeDtypeStruct((rows, QK_ROPE), F32)],
        compiler_params=_params("parallel"),
        name="latent_post",
    )(z_small, g_kv.reshape(1, KV_RANK), cos64, sin64)


def _q_prep_kernel(qn_ref, qr_ref, qsw_ref, cos_ref, sin_ref, o_ref):
    scale = QK_DIM ** -0.5 * math.log2(math.e)
    cos = jnp.tile(cos_ref[...], (1, N_HEADS))
    sin = jnp.tile(sin_ref[...], (1, N_HEADS))
    qr = (qr_ref[...].astype(F32) * cos + qsw_ref[...].astype(F32) * sin) * scale
    qn = qn_ref[...].astype(F32) * scale
    for h in range(N_HEADS):
        o_ref[h, :, 0:QK_NOPE] = qn[:, h * QK_NOPE:(h + 1) * QK_NOPE].astype(o_ref.dtype)
        o_ref[h, :, QK_NOPE:QK_DIM] = qr[:, h * QK_ROPE:(h + 1) * QK_ROPE].astype(o_ref.dtype)


def _q_prep(z_big, cos64, sin64, row0, rows):
    tm = _pick(math.gcd(rows, row0) if row0 else rows, (256, 128, 32))
    b0 = row0 // tm
    nope_blk = 3 * D_CONV // (N_HEADS * QK_NOPE)
    rope_blk = (3 * D_CONV + N_HEADS * QK_NOPE) // (N_HEADS * QK_ROPE)
    return pl.pallas_call(
        _q_prep_kernel,
        grid=(rows // tm,),
        in_specs=[pl.BlockSpec((tm, N_HEADS * QK_NOPE), lambda i: (b0 + i, nope_blk)),
                  pl.BlockSpec((tm, N_HEADS * QK_ROPE), lambda i: (b0 + i, rope_blk)),
                  pl.BlockSpec((tm, N_HEADS * QK_ROPE), lambda i: (b0 + i, rope_blk + 1)),
                  pl.BlockSpec((tm, QK_ROPE), lambda i: (b0 + i, 0)),
                  pl.BlockSpec((tm, QK_ROPE), lambda i: (b0 + i, 0))],
        out_specs=pl.BlockSpec((N_HEADS, tm, QK_DIM), lambda i: (0, i, 0)),
        out_shape=jax.ShapeDtypeStruct((N_HEADS, rows, QK_DIM), BF16),
        compiler_params=_params("parallel"),
        name="q_prep",
    )(z_big, z_big, z_big, cos64, sin64)


def _conv_kernel(state_ref, pc_ref, px_ref, b_ref, c_ref, x_ref, w_ref, o_ref, last_ref, *, tm, halo):
    i = pl.program_id(1)
    u = c_ref[...].astype(F32) * x_ref[...].astype(F32)
    prev = pc_ref[...].astype(F32) * px_ref[...].astype(F32)
    st = state_ref[...]
    first = i == 0
    um1 = jnp.where(first, st[7:8, :], prev[halo - 1:halo, :])
    um2 = jnp.where(first, st[6:7, :], prev[halo - 2:halo - 1, :])
    row = lax.broadcasted_iota(jnp.int32, u.shape, 0)
    s1 = jnp.where(row == 0, um1, pltpu.roll(u, 1, 0))
    s2 = jnp.where(row == 0, um2, jnp.where(row == 1, um1, pltpu.roll(u, 2, 0)))
    w = w_ref[...]
    y = w[0:1, :] * s2 + w[1:2, :] * s1 + w[2:3, :] * u
    o_ref[...] = (b_ref[...].astype(F32) * y).astype(o_ref.dtype)
    last_ref[...] = u[tm - 8:tm, :]


def _conv_gate(z_big, state, conv_w, row0, nseq, seq_len, into=None):
    tm = _pick(seq_len, (512, 256, 128, 32))
    halo = 16
    nb = seq_len // tm
    b0 = row0 // tm
    h0 = row0 // halo
    per = tm // halo
    d = D_CONV

    n_tail = _tail_blocks(into, row0 + nseq * seq_len, z_big.shape[0], tm)
    assert n_tail == 0 or nseq == 1
    real = lambda i: jnp.minimum(i, nb - 1)

    def hmap(col):
        return lambda s, i: (jnp.maximum(h0 + (s * nb + real(i)) * per - 1, 0), col)

    def bmap(col, clamp=True):
        return lambda s, i: (b0 + s * nb + (real(i) if clamp else i), col)

    return _fill_call(
        functools.partial(_conv_kernel, tm=tm, halo=halo), into=into, tail=(1, nb) if n_tail else None,
        grid=(nseq, nb + n_tail),
        in_specs=[pl.BlockSpec((None, 8, d), lambda s, i: (s, 0, 0)),
                  pl.BlockSpec((halo, d), hmap(1)),
                  pl.BlockSpec((halo, d), hmap(2)),
                  pl.BlockSpec((tm, d), bmap(0)),
                  pl.BlockSpec((tm, d), bmap(1)),
                  pl.BlockSpec((tm, d), bmap(2)),
                  pl.BlockSpec((8, d), lambda s, i: (0, 0))],
        args=(state, z_big, z_big, z_big, z_big, z_big, jnp.pad(conv_w, ((0, 5), (0, 0)))),
        out_specs=[pl.BlockSpec((tm, d), bmap(0, clamp=False)),
                   pl.BlockSpec((None, 8, d), lambda s, i: (s * nb + real(i), 0, 0))],
        out_shape=[jax.ShapeDtypeStruct((z_big.shape[0], d), BF16),
                   jax.ShapeDtypeStruct((nseq * nb, 8, d), F32)],
        compiler_params=_params("parallel", "arbitrary"),
        name="conv_gate",
    )


def _attn_latent_kernel(q_ref, wk_ref, wv_ref, cp_ref, krp_ref, cn_ref, krn_ref, o_ref, qa_ref, acc_ref,
                        *, past, tk):
    nh, dseq, _ = q_ref.shape
    rows = nh * dseq
    for h in range(nh):
        qh = q_ref[h]
        qa = lax.dot_general(qh[:, :QK_NOPE], wk_ref[h], NT_DIMS, preferred_element_type=F32)
        qa_ref[h * dseq:(h + 1) * dseq, 0:KV_RANK] = qa.astype(qa_ref.dtype)
        qa_ref[h * dseq:(h + 1) * dseq, KV_RANK:KV_RANK + QK_ROPE] = qh[:, QK_NOPE:QK_DIM]
    q = qa_ref[...]
    acc_ref[...] = jnp.zeros(acc_ref.shape, F32)

    def update(carry, s, c):
        m, l = carry
        m_new = jnp.maximum(m, jnp.max(s, axis=1, keepdims=True))
        p = jnp.exp2(s - m_new)
        alpha = jnp.exp2(m - m_new)
        acc_ref[...] = alpha * acc_ref[...] + jnp.dot(p.astype(BF16), c, preferred_element_type=F32)
        return m_new, alpha * l + jnp.sum(p, axis=1, keepdims=True)

    def past_step(j, carry):
        ks = pl.ds(pl.multiple_of(j * tk, tk), tk)
        c = cp_ref[ks, :].astype(BF16)
        k = jnp.concatenate([c, krp_ref[ks, :].astype(BF16)], axis=1)
        return update(carry, lax.dot_general(q, k, NT_DIMS, preferred_element_type=F32), c)

    carry = (jnp.full((rows, 1), MASK_VALUE, F32), jnp.zeros((rows, 1), F32))
    carry = lax.fori_loop(0, past // tk, past_step, carry)

    cn = cn_ref[...].astype(BF16)
    kn = jnp.concatenate([cn, krn_ref[...].astype(BF16)], axis=1)
    s = lax.dot_general(q, kn, NT_DIMS, preferred_element_type=F32)
    q_pos = past + lax.broadcasted_iota(jnp.int32, (rows, 1), 0) % dseq
    k_pos = past + lax.broadcasted_iota(jnp.int32, (1, dseq), 1)
    s = jnp.where(k_pos // CHUNK <= q_pos // CHUNK, s, MASK_VALUE)
    m, l = update(carry, s, cn)

    lat = (acc_ref[...] / l).astype(BF16)
    for h in range(nh):
        o = jnp.dot(lat[h * dseq:(h + 1) * dseq, :], wv_ref[h], preferred_element_type=F32)
        o_ref[:, h * V_DIM:(h + 1) * V_DIM] = o.astype(o_ref.dtype)


def _attention_latent(q, wk, wv, c_past, kr_past, c_new, kr_new, row0, into):
    nb, past, _ = c_past.shape
    dseq = c_new.shape[1]
    tk = _pick(past, (512, 256, 128))
    assert row0 % dseq == 0
    b0 = row0 // dseq
    per_b = lambda w: pl.BlockSpec((None,) + w, lambda b: (b, 0, 0))
    whole = lambda a: pl.BlockSpec(a.shape, lambda b: (0, 0, 0))
    return _fill_call(
        functools.partial(_attn_latent_kernel, past=past, tk=tk), into=into,
        grid=(nb,),
        in_specs=[pl.BlockSpec((N_HEADS, dseq, QK_DIM), lambda b: (0, b, 0)),
                  whole(wk), whole(wv),
                  per_b((past, KV_RANK)), per_b((past, QK_ROPE)),
                  per_b((dseq, KV_RANK)), per_b((dseq, QK_ROPE))],
        args=(q, wk, wv, c_past, kr_past, c_new, kr_new),
        out_specs=pl.BlockSpec((dseq, N_HEADS * V_DIM), lambda b: (b0 + b, 0)),
        out_shape=jax.ShapeDtypeStruct(into.shape, into.dtype),
        scratch_shapes=[pltpu.VMEM((N_HEADS * dseq, KV_RANK + QK_ROPE), BF16),
                        pltpu.VMEM((N_HEADS * dseq, KV_RANK), F32)],
        compiler_params=_params("parallel"),
        name="attention_latent",
    )


def _attn_t_kernel(q_ref, k_ref, kr_ref, vt_ref, o_ref, m_ref, l_ref, acc_ref, s0_ref, s1_ref, *, tb):
    i = pl.program_id(1)
    q = q_ref[...]
    q_pos = i * tb + lax.broadcasted_iota(jnp.int32, (1, tb), 1)
    col_end = (q_pos // CHUNK + 1) * CHUNK

    m_ref[...] = jnp.full(m_ref.shape, MASK_VALUE, F32)
    l_ref[...] = jnp.zeros(l_ref.shape, F32)
    acc_ref[...] = jnp.zeros(acc_ref.shape, F32)

    def produce(s_ref, j):
        ks = pl.ds(pl.multiple_of(j * tb, tb), tb)
        k = jnp.concatenate([k_ref[ks, :], kr_ref[ks, :]], axis=1)
        s_ref[...] = lax.dot_general(k, q, NT_DIMS, preferred_element_type=F32)

    def consume(s_ref, j, masked):
        s = s_ref[...]
        if masked:
            k_pos = j * tb + lax.broadcasted_iota(jnp.int32, (tb, 1), 0)
            s = jnp.where(k_pos < col_end, s, MASK_VALUE)
        m_old = m_ref[...]
        m_new = jnp.maximum(m_old, jnp.max(s, axis=0, keepdims=True))
        p = jnp.exp2(s - m_new)
        alpha = jnp.exp2(m_old - m_new)
        m_ref[...] = m_new
        l_ref[...] = alpha * l_ref[...] + jnp.sum(p, axis=0, keepdims=True)
        acc_ref[...] = alpha * acc_ref[...] + jnp.dot(vt_ref[j], p.astype(BF16), preferred_element_type=F32)

    produce(s0_ref, 0)

    bufs = (s0_ref, s1_ref)
    unroll = 8

    def group(g, carry):
        j0 = unroll * g
        for t in range(unroll):
            produce(bufs[(t + 1) % 2], j0 + t + 1)
            consume(bufs[t % 2], j0 + t, False)
        return carry

    lax.fori_loop(0, i // unroll, group, 0)
    j0 = i // unroll * unroll

    for rem in range(unroll):
        @pl.when(i - j0 == rem)
        def _(rem=rem):
            for t in range(rem + 1):
                if t < rem:
                    produce(bufs[(t + 1) % 2], j0 + t + 1)
                consume(bufs[t % 2], j0 + t, t == rem)

    o_ref[...] = (acc_ref[...] / l_ref[...]).T.astype(o_ref.dtype)


def _attention_t(q, k, kr, vt, seq, tk, out_rows):
    tq = tk
    assert tk % CHUNK == 0 and seq % tk == 0
    nq = seq // tq
    n_tail = _tail_blocks(None, seq, out_rows, tq)
    return _fill_call(
        functools.partial(_attn_t_kernel, tb=tk), into=None, tail=(1, nq) if n_tail else None,
        grid=(N_HEADS, nq + n_tail),
        in_specs=[pl.BlockSpec((None, tq, QK_DIM), lambda h, i: (h, jnp.minimum(i, nq - 1), 0)),
                  pl.BlockSpec((None, seq, QK_NOPE), lambda h, i: (h, 0, 0)),
                  pl.BlockSpec((seq, QK_ROPE), lambda h, i: (0, 0)),
                  pl.BlockSpec((None, seq // tk, V_DIM, tk), lambda h, i: (h, 0, 0, 0))],
        args=(q, k, kr, vt),
        out_specs=pl.BlockSpec((tq, V_DIM), lambda h, i: (i, h)),
        out_shape=jax.ShapeDtypeStruct((out_rows, N_HEADS * V_DIM), BF16),
        scratch_shapes=[pltpu.VMEM((1, tq), F32), pltpu.VMEM((1, tq), F32), pltpu.VMEM((V_DIM, tq), F32),
                        pltpu.VMEM((tk, tq), F32), pltpu.VMEM((tk, tq), F32)],
        compiler_params=_params("parallel", "arbitrary"),
        name="attention_t",
    )


def _merge_kernel(a_ref, o_ref, wa_ref, wb_ref, ga_ref, gb_ref, m_ref):
    ya = jnp.dot(a_ref[...], wa_ref[...], preferred_element_type=F32)
    yb = jnp.dot(o_ref[...], wb_ref[...], preferred_element_type=F32)
    m = jax.nn.sigmoid(ga_ref[...].astype(F32)) * ya + jax.nn.sigmoid(gb_ref[...].astype(F32)) * yb
    m_ref[...] = m.astype(m_ref.dtype)


def _merge(ya_in, o, wa, wb, z_big):
    t, k = ya_in.shape
    n = wa.shape[1]
    tm = _pick(t, (1280, 1024, 512, 256, 128))
    tn = _pick(n, (512, 256, 128))
    ga_blk = (3 * D_CONV + N_HEADS * (QK_NOPE + 2 * QK_ROPE)) // tn
    gb_blk = ga_blk + D_MODEL // tn
    return pl.pallas_call(
        _merge_kernel,
        grid=(t // tm, n // tn),
        in_specs=[pl.BlockSpec((tm, k), lambda i, j: (i, 0)),
                  pl.BlockSpec((tm, k), lambda i, j: (i, 0)),
                  pl.BlockSpec((k, tn), lambda i, j: (0, j)),
                  pl.BlockSpec((k, tn), lambda i, j: (0, j)),
                  pl.BlockSpec((tm, tn), lambda i, j: (i, ga_blk + j)),
                  pl.BlockSpec((tm, tn), lambda i, j: (i, gb_blk + j))],
        out_specs=pl.BlockSpec((tm, tn), lambda i, j: (i, j)),
        out_shape=jax.ShapeDtypeStruct((t, n), BF16),
        compiler_params=_params("parallel", "arbitrary"),
        name="merge",
    )(ya_in, o, wa, wb, z_big, z_big)


def _extract_top(work_ref, rank_ref, val_ref, idx_ref, *, track_rank):
    nk = PEER_NKEYS
    tm = work_ref.shape[1]
    neg = jnp.full((8, tm), -jnp.inf, F32)

    def round_(p, carry):
        idx_prev, pf = carry
        parts = [neg, neg, neg, neg]
        for k in range(nk):
            rows = slice(8 * k, 8 * k + 8)
            hit = idx_prev == float(k)
            w = jnp.where(hit, -jnp.inf, work_ref[rows, :])
            work_ref[rows, :] = w
            if track_rank:
                rank_ref[rows, :] = jnp.where(hit, pf - 1.0, rank_ref[rows, :])
            parts[k % 4] = jnp.maximum(parts[k % 4], w)
        m = jnp.maximum(jnp.maximum(parts[0], parts[1]), jnp.maximum(parts[2], parts[3]))
        big = jnp.full((8, tm), float(nk), F32)
        iparts = [big, big, big, big]
        for k in range(nk):
            w = work_ref[8 * k:8 * k + 8, :]
            iparts[k % 4] = jnp.minimum(iparts[k % 4], jnp.where(w == m, float(k), float(nk)))
        idx = jnp.minimum(jnp.minimum(iparts[0], iparts[1]), jnp.minimum(iparts[2], iparts[3]))
        val_ref[p] = m
        idx_ref[p] = idx
        return idx, pf + 1.0

    idx_last, _ = lax.fori_loop(0, PEER_TOPK, round_,
                                (jnp.full((8, tm), -1.0, F32), jnp.zeros((8, tm), F32)))
    if track_rank:
        for k in range(nk):
            rows = slice(8 * k, 8 * k + 8)
            rank_ref[rows, :] = jnp.where(idx_last == float(k), float(PEER_TOPK - 1), rank_ref[rows, :])


def _peer_select_kernel(qp_ref, k1_ref, k2_ref, k2h_ref, perm_ref,
                        r2_ref, e2_ref, n1_ref, w1_ref,
                        work_ref, rank_ref, a_ref, ia_ref, b_ref, ib_ref):
    nk = PEER_NKEYS
    qp = qp_ref[...]
    tm = qp.shape[0]
    s2h = lax.dot_general(k2h_ref[...], qp, NT_DIMS, preferred_element_type=F32)
    for h in range(PEER_HEADS):
        blk = s2h[h * nk:(h + 1) * nk, :]
        e2_ref[h * nk:(h + 1) * nk, :] = jnp.exp(blk - jnp.max(blk, axis=0, keepdims=True))

    work_ref[...] = lax.dot_general(k1_ref[...], qp, NT_DIMS, preferred_element_type=F32)
    _extract_top(work_ref, rank_ref, a_ref, ia_ref, track_rank=False)
    work_ref[...] = lax.dot_general(k2_ref[...], qp, NT_DIMS, preferred_element_type=F32)
    rank_ref[...] = jnp.full(rank_ref.shape, float(PEER_TOPK), F32)
    _extract_top(work_ref, rank_ref, b_ref, ib_ref, track_rank=True)
    r2_ref[...] = jnp.dot(perm_ref[...], rank_ref[...].astype(BF16), preferred_element_type=F32)

    a = [a_ref[p] for p in range(PEER_TOPK)]
    b = [b_ref[q] for q in range(PEER_TOPK)]
    cand = [a[p] + b[q] for (p, q) in _PAIRS]
    npair = len(_PAIRS)
    beaten = [jnp.zeros((8, tm), F32) for _ in range(npair)]
    for x in range(npair):
        px, qx = _PAIRS[x]
        for y in range(x + 1, npair):
            py, qy = _PAIRS[y]
            if px <= py and qx <= qy:
                beaten[y] = beaten[y] + 1.0
            else:
                gt = jnp.where(cand[y] > cand[x], 1.0, 0.0)
                beaten[x] = beaten[x] + gt
                beaten[y] = beaten[y] + (1.0 - gt)
    sel = [jnp.where(bt < float(PEER_TOPK), 1.0, 0.0) for bt in beaten]
    ea = [jnp.exp(a[p] - a[0]) for p in range(PEER_TOPK)]
    eb = [jnp.exp(b[q] - b[0]) for q in range(PEER_TOPK)]
    z = jnp.zeros((8, tm), F32)
    cnt = [jnp.zeros((8, tm), F32) for _ in range(PEER_TOPK)]
    for x, (p, q) in enumerate(_PAIRS):
        z = z + sel[x] * (ea[p] * eb[q])
        cnt[p] = cnt[p] + sel[x]
    inv_z = 1.0 / z
    w1 = [ea[p] * inv_z for p in range(PEER_TOPK)]
    ia = [ia_ref[p] for p in range(PEER_TOPK)]
    zero = jnp.zeros((8, tm), F32)
    for k in range(nk):
        n1k = zero
        w1k = zero
        for p in range(PEER_TOPK):
            hit = ia[p] == float(k)
            n1k = jnp.where(hit, cnt[p], n1k)
            w1k = jnp.where(hit, w1[p], w1k)
        n1_ref[8 * k:8 * k + 8, :] = n1k
        w1_ref[8 * k:8 * k + 8, :] = w1k


def _peer_select(qp, k1, k2, k2h, perm):
    t, d = qp.shape
    tm = _pick(t, (256, 128))
    rows = PEER_NKEYS * PEER_HEADS
    full = lambda i: (0, 0)
    out = jax.ShapeDtypeStruct((rows, t), F32)
    ospec = pl.BlockSpec((rows, tm), lambda i: (0, i))
    return pl.pallas_call(
        _peer_select_kernel,
        grid=(t // tm,),
        in_specs=[pl.BlockSpec((tm, d), lambda i: (i, 0)),
                  pl.BlockSpec((rows, d), full),
                  pl.BlockSpec((rows, d), full),
                  pl.BlockSpec((rows, d), full),
                  pl.BlockSpec((rows, rows), full)],
        out_specs=[ospec, ospec, ospec, ospec],
        out_shape=[out, out, out, out],
        scratch_shapes=[pltpu.VMEM((rows, tm), F32), pltpu.VMEM((rows, tm), F32),
                        pltpu.VMEM((PEER_TOPK, 8, tm), F32), pltpu.VMEM((PEER_TOPK, 8, tm), F32),
                        pltpu.VMEM((PEER_TOPK, 8, tm), F32), pltpu.VMEM((PEER_TOPK, 8, tm), F32)],
        compiler_params=_params("parallel"),
        name="peer_select",
    )(qp, k1, k2, k2h, perm)


def _gelu_tanh(x):
    c = math.sqrt(2.0 / math.pi)
    return 0.5 * x * (1.0 + jnp.tanh(c * (x + 0.044715 * (x * x * x))))


def _peer_main_kernel(hn_ref, u_ref, va_ref, vb_ref, r2_ref, e2_ref, n1_ref, w1_ref,
                      o_ref, ga_ref, gb_ref, pa_ref, pb_ref, act_ref, *, tc):
    nk = PEER_NKEYS
    te, tm = ga_ref.shape
    j = pl.program_id(1)
    last = pl.num_programs(1) - 1

    hn = hn_ref[...]

    nw = 256
    d = o_ref.shape[1]

    n_il = te // nk
    jr = 64

    def gate_tile(g_ref, nref, wref, half, c, jh):
        cols = slice(c * tc, (c + 1) * tc)
        accs = [jnp.zeros((jr, tc), F32) for _ in range(n_il)]
        for h in range(PEER_HEADS):
            hr = slice(h * nk + jh * jr, h * nk + (jh + 1) * jr)
            r2 = r2_ref[hr, cols]
            e2 = e2_ref[hr, cols]
            for il in range(n_il):
                r = (half * n_il + il) * PEER_HEADS + h
                keep = r2 < nref[r:r + 1, cols]
                accs[il] = accs[il] + jnp.where(keep, e2, 0.0) * wref[r:r + 1, cols]
        for il in range(n_il):
            g_ref[il * nk + jh * jr:il * nk + (jh + 1) * jr, cols] = accs[il]

    def gate(g_ref, nref, wref, half):
        for c in range(tm // tc):
            for jh in range(nk // jr):
                gate_tile(g_ref, nref, wref, half, c, jh)

    @pl.when(j == 0)
    def _():
        o_ref[...] = jnp.zeros(o_ref.shape, o_ref.dtype)
        pb_ref[...] = jnp.zeros(pb_ref.shape, pb_ref.dtype)

    def act_piece(half, n):
        u = u_ref[half * te + n * nw:half * te + (n + 1) * nw, :]
        return _gelu_tanh(lax.dot_general(hn, u, NT_DIMS, preferred_element_type=F32))

    def out_piece(p_ref, v_ref, n):
        cols = slice(n * nw, (n + 1) * nw)
        o_ref[:, cols] += jnp.dot(p_ref[...], v_ref[:, cols], preferred_element_type=F32)

    def sub_block(half, g_ref, p_new_ref, p_old_ref, v_old_ref, next_gate):
        for n in range(te // nw):
            act_ref[:, n * nw:(n + 1) * nw] = act_piece(half, n)
        for n in range(d // nw):
            out_piece(p_old_ref, v_old_ref, n)
        gate(*next_gate)
        for il in range(n_il):
            ecols = slice(il * nk, (il + 1) * nk)
            for c in range(tm // tc):
                rows = slice(c * tc, (c + 1) * tc)
                g = g_ref[ecols, rows].T
                p_new_ref[rows, ecols] = (g * act_ref[rows, ecols]).astype(p_new_ref.dtype)

    @pl.when(j < last)
    def _():
        sub_block(0, ga_ref, pa_ref, pb_ref, vb_ref, (ga_ref, n1_ref, w1_ref, 0))
        sub_block(1, gb_ref, pb_ref, pa_ref, va_ref, (gb_ref, n1_ref, w1_ref, 1))

    @pl.when(j == last)
    def _():
        for n in range(d // nw):
            out_piece(pb_ref, vb_ref, n)


def _peer_main(hn, u, v, r2, e2, n1, w1):
    t, d = hn.shape
    e = u.shape[0]
    tm = _pick(t, (640, 512, 256, 128))
    te = 512
    tc = 128
    rows = PEER_NKEYS * PEER_HEADS
    sub = 2 * te // PEER_NKEYS * PEER_HEADS
    nj = e // (2 * te)
    cur = lambda j: jnp.minimum(j, nj - 1)
    return pl.pallas_call(
        functools.partial(_peer_main_kernel, tc=tc),
        grid=(t // tm, nj + 1),
        in_specs=[pl.BlockSpec((tm, d), lambda i, j: (i, 0)),
                  pl.BlockSpec((2 * te, d), lambda i, j: (cur(j), 0)),
                  pl.BlockSpec((te, d), lambda i, j: (2 * cur(j), 0)),
                  pl.BlockSpec((te, d), lambda i, j: (jnp.maximum(2 * j - 1, 0), 0)),
                  pl.BlockSpec((rows, tm), lambda i, j: (0, i)),
                  pl.BlockSpec((rows, tm), lambda i, j: (0, i)),
                  pl.BlockSpec((sub, tm), lambda i, j: (cur(j), i)),
                  pl.BlockSpec((sub, tm), lambda i, j: (cur(j), i))],
        out_specs=pl.BlockSpec((tm, d), lambda i, j: (i, 0)),
        out_shape=jax.ShapeDtypeStruct((t, d), F32),
        scratch_shapes=[pltpu.VMEM((te, tm), F32), pltpu.VMEM((te, tm), F32),
                        pltpu.VMEM((tm, te), BF16), pltpu.VMEM((tm, te), BF16),
                        pltpu.VMEM((tm, te), F32)],
        compiler_params=_params("parallel", "arbitrary"),
        name="peer_main",
    )(hn, u, v, v, r2, e2, n1, w1)


def _ple_final_kernel(xn_ref, p_ref, wg_ref, wp_ref, h_ref, g_ref, o_ref):
    gate = jnp.dot(xn_ref[...], wg_ref[...], preferred_element_type=F32)
    pe = jnp.dot(p_ref[...].astype(BF16), wp_ref[...], preferred_element_type=F32)
    x = h_ref[...] + pe * jax.nn.sigmoid(gate)
    ms = jnp.mean(x * x, axis=-1, keepdims=True)
    o_ref[...] = x * lax.rsqrt(ms + RMS_EPS) * g_ref[...]


def _ple_final(xn, p, wg, wp, h, g, row0):
    d = xn.shape[1]
    rows, pd = p.shape
    tm = _pick(math.gcd(rows, row0), (256, 128))
    b0 = row0 // tm
    whole = lambda i: (0, 0)
    return pl.pallas_call(
        _ple_final_kernel,
        grid=(rows // tm,),
        in_specs=[pl.BlockSpec((tm, d), lambda i: (b0 + i, 0)),
                  pl.BlockSpec((tm, pd), lambda i: (i, 0)),
                  pl.BlockSpec((d, d), whole),
                  pl.BlockSpec((pd, d), whole),
                  pl.BlockSpec((tm, d), lambda i: (b0 + i, 0)),
                  pl.BlockSpec((1, d), whole)],
        out_specs=pl.BlockSpec((tm, d), lambda i: (i, 0)),
        out_shape=jax.ShapeDtypeStruct((rows, d), F32),
        compiler_params=_params("parallel"),
        name="ple_final",
    )(xn, p, wg, wp, h, g.reshape(1, d))


def _swap_halves(w):
    half = w.shape[-1] // 2
    return jnp.concatenate([w[..., half:], w[..., :half]], axis=-1)


def _split_w_in(w_in):
    d = w_in.shape[0]
    splits = [int(s) for s in np.cumsum(IN_SIZES)[:-1]]
    wb, wc, wx, wq, wckv, wkr, wga, wgb = jnp.split(w_in, splits, axis=1)
    wq = wq.reshape(d, N_HEADS, QK_DIM)
    wq_n = wq[:, :, :QK_NOPE].reshape(d, N_HEADS * QK_NOPE)
    wq_r = wq[:, :, QK_NOPE:]
    w_big = jnp.concatenate(
        [wb, wc, wx, wq_n, wq_r.reshape(d, -1), _swap_halves(wq_r).reshape(d, -1), wga, wgb], axis=1)
    w_small = jnp.concatenate([wckv, wkr, _swap_halves(wkr)], axis=1)
    return w_big.astype(BF16), w_small.astype(BF16)


def _rope_tables(pos):
    inv = 1.0 / (ROPE_THETA ** (jnp.arange(0, QK_ROPE, 2, dtype=F32) / QK_ROPE))
    ang = pos.astype(F32)[:, None] * inv[None, :]
    cos, sin = jnp.cos(ang), jnp.sin(ang)
    return jnp.concatenate([cos, cos], axis=1), jnp.concatenate([-sin, sin], axis=1)


def _key_matrices(sub_keys):
    nk, hd, half = PEER_NKEYS, PEER_HEADS, PEER_DKEY // 2
    eye = jnp.eye(hd, dtype=F32)

    def build(c, head_major):
        sel = jnp.zeros((2,), F32).at[c].set(1.0)
        m = sub_keys[c][:, None, None, None, :] * eye[None, :, :, None, None] * sel[None, None, None, :, None]
        if head_major:
            m = jnp.transpose(m, (1, 0, 2, 3, 4))
        return m.reshape(nk * hd, hd * PEER_DKEY).astype(BF16)

    r = np.arange(nk * hd)
    perm = np.zeros((nk * hd, nk * hd), np.float32)
    perm[(r % hd) * nk + r // hd, r] = 1.0
    return build(0, False), build(1, False), build(1, True), jnp.asarray(perm, BF16)


def kernel(x_prompt, x_sample, cache_conv, cache_ckv, cache_krope, p_prompt, p_sample, g_mix, w_in, conv_w, g_kv, w_kv_b, w_a_out, w_b_out, w_o, g_ffn, w_pq, sub_keys, u_tab, v_tab, g_ple, w_ple_gate, w_ple, g_final):
    assert x_prompt.shape[0] == 1 and w_in.shape[0] == 1
    seq = x_prompt.shape[1]
    nb, dseq = x_sample.shape[0], x_sample.shape[1]
    past = cache_ckv.shape[2]
    d = D_MODEL
    ns = nb * dseq
    t = seq + ns

    x_p, x_s = x_prompt.reshape(seq, d), x_sample.reshape(ns, d)
    p_p, p_s = p_prompt[0].reshape(seq, -1), p_sample[0].reshape(ns, -1)
    pos = jnp.concatenate([jnp.arange(seq), jnp.tile(past + jnp.arange(dseq), nb)])
    cos64, sin64 = _rope_tables(pos)
    w_big, w_small = _split_w_in(w_in[0])

    xn = _rmsnorm(x_p, g_mix[0], BF16, 0, t)
    xn = _rmsnorm(x_s, g_mix[0], BF16, seq, t, into=xn)
    z_big = _matmul(xn, w_big, BF16, name="in_proj")
    z_small = _matmul(xn, w_small, F32, tn_prefs=(w_small.shape[1],), name="in_proj_small")
    c_p, kr_p = _post_small(z_small, g_kv[0], cos64, sin64, 0, seq)
    c_s, kr_s = _post_small(z_small, g_kv[0], cos64, sin64, seq, ns)

    zero_state = jnp.zeros((1, 8, D_CONV), F32)
    ya_in, last_p = _conv_gate(z_big, zero_state, conv_w[0], 0, 1, seq)
    state_s = jnp.pad(cache_conv[0], ((0, 0), (6, 0), (0, 0)))
    ya_in, last_s = _conv_gate(z_big, state_s, conv_w[0], seq, nb, dseq, into=ya_in)
    conv_p = last_p[-1, 6:8, :].reshape(1, 1, 2, D_CONV)
    if dseq >= 2:
        conv_s = last_s[:, 6:8, :].reshape(1, nb, 2, D_CONV)
    else:
        raise NotImplementedError("sample blocks shorter than the convolution state")

    w_kv_h = w_kv_b[0].astype(BF16).reshape(KV_RANK, N_HEADS, QK_NOPE + V_DIM)
    wk_h = jnp.transpose(w_kv_h[:, :, :QK_NOPE], (1, 0, 2))
    wv_h = jnp.transpose(w_kv_h[:, :, QK_NOPE:], (1, 0, 2))
    wk = w_kv_h[:, :, :QK_NOPE].reshape(KV_RANK, N_HEADS * QK_NOPE)
    wvt = w_kv_h[:, :, QK_NOPE:].reshape(KV_RANK, N_HEADS * V_DIM).T
    q_p = _q_prep(z_big, cos64, sin64, 0, seq)
    q_s = _q_prep(z_big, cos64, sin64, seq, ns)
    tk = _pick(seq, (512, 256, 128))
    k_p, vt_p = _kv_proj_t(c_p, wk, wvt, tk)
    o_all = _attention_t(q_p, k_p, kr_p.astype(BF16), vt_p, seq, tk, t)
    o_all = _attention_latent(q_s, wk_h, wv_h, cache_ckv[0], cache_krope[0],
                              c_s.reshape(nb, dseq, KV_RANK), kr_s.reshape(nb, dseq, QK_ROPE), seq, o_all)

    m = _merge(ya_in, o_all, w_a_out[0].astype(BF16), w_b_out[0].astype(BF16), z_big)
    w_out = w_o[0].astype(BF16)
    h1, hn = _out_proj_norm(m, w_out, x_p, g_ffn[0], 0)
    h1, hn = _out_proj_norm(m, w_out, x_s, g_ffn[0], seq, into=(h1, hn))

    qp = _matmul(hn, w_pq[0].astype(BF16), BF16, name="peer_query")
    k1, k2, k2h, perm = _key_matrices(sub_keys[0])
    r2, e2, n1, w1 = _peer_select(qp, k1, k2, k2h, perm)
    peer = _peer_main(hn, u_tab[0].astype(BF16), v_tab[0].astype(BF16), r2, e2, n1, w1)

    h2, x2n = _add_rmsnorm(h1, peer, g_ple[0], BF16)
    wg, wp = w_ple_gate[0].astype(BF16), w_ple[0].astype(BF16)
    y_prompt = _ple_final(x2n, p_p, wg, wp, h2, g_final, 0).reshape(1, seq, d)
    y_sample = _ple_final(x2n, p_s, wg, wp, h2, g_final, seq).reshape(nb, dseq, d)

    ckv_p = c_p.reshape(1, 1, seq, KV_RANK)
    kr_p = kr_p.reshape(1, 1, seq, QK_ROPE)
    ckv_s = c_s.reshape(1, nb, dseq, KV_RANK)
    kr_s = kr_s.reshape(1, nb, dseq, QK_ROPE)
    return (y_prompt, y_sample, conv_p, ckv_p, kr_p, conv_s, ckv_s, kr_s)
```

```python
import functools
import math

import numpy as np
import jax
import jax.numpy as jnp
from jax import lax
from jax.experimental import pallas as pl
from jax.experimental.pallas import tpu as pltpu

D_MODEL = 2048
D_CONV = 2048
N_HEADS = 16
QK_NOPE = 128
QK_ROPE = 64
QK_DIM = QK_NOPE + QK_ROPE
V_DIM = 128
KV_RANK = 512
CHUNK = 64
ROPE_THETA = 10000.0
PEER_HEADS = 8
PEER_NKEYS = 128
PEER_TOPK = 16
PEER_DKEY = 256
PEER_EXPERTS = PEER_NKEYS * PEER_NKEYS
RMS_EPS = 1e-6
IN_SIZES = (D_CONV, D_CONV, D_CONV, N_HEADS * QK_DIM, KV_RANK, QK_ROPE, D_MODEL, D_MODEL)

BF16 = jnp.bfloat16
F32 = jnp.float32
VMEM_LIMIT_BYTES = 56 * 1024 * 1024
MASK_VALUE = -1e30
NT_DIMS = (((1,), (1,)), ((), ()))
TN_DIMS = (((0,), (0,)), ((), ()))

_PAIRS = tuple((p, q) for p in range(PEER_TOPK) for q in range(PEER_TOPK)
               if (p + 1) * (q + 1) <= PEER_TOPK)


def _pick(n, prefs):
    for p in prefs:
        if n % p == 0:
            return p
    return n


def _params(*sem):
    return pltpu.CompilerParams(dimension_semantics=sem, vmem_limit_bytes=VMEM_LIMIT_BYTES)


def _rms(x, g):
    ms = jnp.mean(x * x, axis=-1, keepdims=True)
    return x * lax.rsqrt(ms + RMS_EPS) * g


def _rms_kernel(x_ref, g_ref, o_ref):
    o_ref[...] = _rms(x_ref[...], g_ref[...]).astype(o_ref.dtype)


def _fill_call(kern, *, into, in_specs, args, tail=None, n_shared=1, **kwargs):
    if tail is not None:
        axis, n_real = tail
        n_in = len(in_specs)
        body = kern

        def kern(*refs):
            i = pl.program_id(axis)

            @pl.when(i < n_real)
            def _():
                body(*refs)

            @pl.when(i >= n_real)
            def _():
                for o in refs[n_in:n_in + n_shared]:
                    o[...] = jnp.zeros(o.shape, o.dtype)

    if into is None:
        return pl.pallas_call(kern, in_specs=in_specs, **kwargs)(*args)
    bufs = tuple(into) if isinstance(into, (tuple, list)) else (into,)
    assert len(bufs) == n_shared

    def filling(*refs):
        kern(*refs[n_shared:])

    return pl.pallas_call(filling, in_specs=[pl.BlockSpec(memory_space=pl.ANY)] * n_shared + list(in_specs),
                          input_output_aliases={k: k for k in range(n_shared)}, **kwargs)(*bufs, *args)


def _tail_blocks(into, end_row, total_rows, tm):
    if into is not None or end_row >= total_rows:
        return 0
    return -(-(total_rows - end_row) // tm)


def _rmsnorm(x, g, out_dtype, out_row0, out_rows, into=None):
    rows, d = x.shape
    tm = _pick(math.gcd(rows, out_row0), (1024, 640, 512, 256, 128))
    bo = out_row0 // tm
    nb = rows // tm
    n_tail = _tail_blocks(into, out_row0 + rows, out_rows, tm)
    return _fill_call(
        _rms_kernel, into=into, tail=(0, nb) if n_tail else None,
        grid=(nb + n_tail,),
        in_specs=[pl.BlockSpec((tm, d), lambda i: (jnp.minimum(i, nb - 1), 0)),
                  pl.BlockSpec((1, d), lambda i: (0, 0))],
        args=(x, g.reshape(1, d)),
        out_specs=pl.BlockSpec((tm, d), lambda i: (bo + i, 0)),
        out_shape=jax.ShapeDtypeStruct((out_rows, d), out_dtype),
        compiler_params=_params("parallel"),
        name="rmsnorm",
    )


def _mm_kernel(x_ref, w_ref, o_ref):
    o_ref[...] = jnp.dot(x_ref[...], w_ref[...], preferred_element_type=F32).astype(o_ref.dtype)


def _matmul(x, w, out_dtype, tn_prefs=(1024, 512, 256, 128), name="matmul"):
    t, k = x.shape
    n = w.shape[1]
    tm = _pick(t, (1280, 1024, 768, 512, 256, 128))
    tn = _pick(n, tn_prefs)
    return pl.pallas_call(
        _mm_kernel,
        grid=(t // tm, n // tn),
        in_specs=[pl.BlockSpec((tm, k), lambda i, j: (i, 0)),
                  pl.BlockSpec((k, tn), lambda i, j: (0, j))],
        out_specs=pl.BlockSpec((tm, tn), lambda i, j: (i, j)),
        out_shape=jax.ShapeDtypeStruct((t, n), out_dtype),
        compiler_params=_params("parallel", "arbitrary"),
        name=name,
    )(x, w)


def _out_proj_norm_kernel(m_ref, w_ref, x_ref, g_ref, h_ref, hn_ref):
    h = x_ref[...] + jnp.dot(m_ref[...], w_ref[...], preferred_element_type=F32)
    h_ref[...] = h
    hn_ref[...] = _rms(h, g_ref[...]).astype(hn_ref.dtype)


def _out_proj_norm(m, w, x_seg, g, row0, into=None):
    t, k = m.shape
    d = w.shape[1]
    rows = x_seg.shape[0]
    tm = _pick(math.gcd(rows, row0), (512, 256, 128))
    b0, nb = row0 // tm, rows // tm
    n_tail = _tail_blocks(into, row0 + rows, t, tm)
    real = lambda i: jnp.minimum(i, nb - 1)
    whole = lambda i: (0, 0)
    row = pl.BlockSpec((tm, d), lambda i: (b0 + i, 0))
    return _fill_call(
        _out_proj_norm_kernel, into=into, tail=(0, nb) if n_tail else None, n_shared=2,
        grid=(nb + n_tail,),
        in_specs=[pl.BlockSpec((tm, k), lambda i: (b0 + real(i), 0)),
                  pl.BlockSpec((k, d), whole),
                  pl.BlockSpec((tm, d), lambda i: (real(i), 0)),
                  pl.BlockSpec((1, d), whole)],
        args=(m, w, x_seg, g.reshape(1, d)),
        out_specs=[row, row],
        out_shape=[jax.ShapeDtypeStruct((t, d), F32), jax.ShapeDtypeStruct((t, d), BF16)],
        compiler_params=_params("parallel"),
        name="out_proj",
    )


def _kv_proj_t_kernel(c_ref, wk_ref, wvt_ref, k_ref, vt_ref):
    c = c_ref[...].astype(BF16)
    k = jnp.dot(c, wk_ref[...], preferred_element_type=F32)
    vt = lax.dot_general(wvt_ref[...], c, NT_DIMS, preferred_element_type=F32)
    for h in range(N_HEADS):
        k_ref[h] = k[:, h * QK_NOPE:(h + 1) * QK_NOPE].astype(k_ref.dtype)
        vt_ref[h] = vt[h * V_DIM:(h + 1) * V_DIM, :].astype(vt_ref.dtype)


def _kv_proj_t(c, wk, wvt, tk):
    r, kd = c.shape
    return pl.pallas_call(
        _kv_proj_t_kernel,
        grid=(r // tk,),
        in_specs=[pl.BlockSpec((tk, kd), lambda i: (i, 0)),
                  pl.BlockSpec((kd, N_HEADS * QK_NOPE), lambda i: (0, 0)),
                  pl.BlockSpec((N_HEADS * V_DIM, kd), lambda i: (0, 0))],
        out_specs=[pl.BlockSpec((N_HEADS, tk, QK_NOPE), lambda i: (0, i, 0)),
                   pl.BlockSpec((N_HEADS, None, V_DIM, tk), lambda i: (0, i, 0, 0))],
        out_shape=[jax.ShapeDtypeStruct((N_HEADS, r, QK_NOPE), BF16),
                   jax.ShapeDtypeStruct((N_HEADS, r // tk, V_DIM, tk), BF16)],
        compiler_params=_params("parallel"),
        name="kv_proj_t",
    )(c, wk, wvt)


def _post_small_kernel(z_ref, g_ref, cos_ref, sin_ref, c_ref, kr_ref):
    z = z_ref[...]
    c_ref[...] = _rms(z[:, :KV_RANK], g_ref[...])
    kr = z[:, KV_RANK:KV_RANK + QK_ROPE]
    kr_sw = z[:, KV_RANK + QK_ROPE:KV_RANK + 2 * QK_ROPE]
    kr_ref[...] = kr * cos_ref[...] + kr_sw * sin_ref[...]


def _post_small(z_small, g_kv, cos64, sin64, row0, rows):
    w = z_small.shape[1]
    tm = _pick(math.gcd(rows, row0), (1024, 512, 256, 128))
    b0 = row0 // tm
    return pl.pallas_call(
        _post_small_kernel,
        grid=(rows // tm,),
        in_specs=[pl.BlockSpec((tm, w), lambda i: (b0 + i, 0)),
                  pl.BlockSpec((1, KV_RANK), lambda i: (0, 0)),
                  pl.BlockSpec((tm, QK_ROPE), lambda i: (b0 + i, 0)),
                  pl.BlockSpec((tm, QK_ROPE), lambda i: (b0 + i, 0))],
        out_specs=[pl.BlockSpec((tm, KV_RANK), lambda i: (i, 0)),
                   pl.BlockSpec((tm, QK_ROPE), lambda i: (i, 0))],
        out_shape=[jax.ShapeDtypeStruct((rows, KV_RANK), F32),
                   jax.ShapeDtypeStruct((rows, QK_ROPE), F32)],
        compiler_params=_params("parallel"),
        name="latent_post",
    )(z_small, g_kv.reshape(1, KV_RANK), cos64, sin64)


def _q_prep_kernel(qn_ref, qr_ref, qsw_ref, cos_ref, sin_ref, o_ref):
    scale = QK_DIM ** -0.5 * math.log2(math.e)
    cos = jnp.tile(cos_ref[...], (1, N_HEADS))
    sin = jnp.tile(sin_ref[...], (1, N_HEADS))
    qr = (qr_ref[...].astype(F32) * cos + qsw_ref[...].astype(F32) * sin) * scale
    qn = qn_ref[...].astype(F32) * scale
    for h in range(N_HEADS):
        o_ref[h, :, 0:QK_NOPE] = qn[:, h * QK_NOPE:(h + 1) * QK_NOPE].astype(o_ref.dtype)
        o_ref[h, :, QK_NOPE:QK_DIM] = qr[:, h * QK_ROPE:(h + 1) * QK_ROPE].astype(o_ref.dtype)


def _q_prep(z_big, cos64, sin64, row0, rows):
    tm = _pick(math.gcd(rows, row0) if row0 else rows, (256, 128, 32))
    b0 = row0 // tm
    nope_blk = 3 * D_CONV // (N_HEADS * QK_NOPE)
    rope_blk = (3 * D_CONV + N_HEADS * QK_NOPE) // (N_HEADS * QK_ROPE)
    return pl.pallas_call(
        _q_prep_kernel,
        grid=(rows // tm,),
        in_specs=[pl.BlockSpec((tm, N_HEADS * QK_NOPE), lambda i: (b0 + i, nope_blk)),
                  pl.BlockSpec((tm, N_HEADS * QK_ROPE), lambda i: (b0 + i, rope_blk)),
                  pl.BlockSpec((tm, N_HEADS * QK_ROPE), lambda i: (b0 + i, rope_blk + 1)),
                  pl.BlockSpec((tm, QK_ROPE), lambda i: (b0 + i, 0)),
                  pl.BlockSpec((tm, QK_ROPE), lambda i: (b0 + i, 0))],
        out_specs=pl.BlockSpec((N_HEADS, tm, QK_DIM), lambda i: (0, i, 0)),
        out_shape=jax.ShapeDtypeStruct((N_HEADS, rows, QK_DIM), BF16),
        compiler_params=_params("parallel"),
        name="q_prep",
    )(z_big, z_big, z_big, cos64, sin64)


def _conv_kernel(state_ref, pc_ref, px_ref, b_ref, c_ref, x_ref, w_ref, o_ref, last_ref, *, tm, halo):
    i = pl.program_id(1)
    u = c_ref[...].astype(F32) * x_ref[...].astype(F32)
    prev = pc_ref[...].astype(F32) * px_ref[...].astype(F32)
    st = state_ref[...]
    first = i == 0
    um1 = jnp.where(first, st[7:8, :], prev[halo - 1:halo, :])
    um2 = jnp.where(first, st[6:7, :], prev[halo - 2:halo - 1, :])
    row = lax.broadcasted_iota(jnp.int32, u.shape, 0)
    s1 = jnp.where(row == 0, um1, pltpu.roll(u, 1, 0))
    s2 = jnp.where(row == 0, um2, jnp.where(row == 1, um1, pltpu.roll(u, 2, 0)))
    w = w_ref[...]
    y = w[0:1, :] * s2 + w[1:2, :] * s1 + w[2:3, :] * u
    o_ref[...] = (b_ref[...].astype(F32) * y).astype(o_ref.dtype)
    last_ref[...] = u[tm - 8:tm, :]


def _conv_gate(z_big, state, conv_w, row0, nseq, seq_len, into=None):
    tm = _pick(seq_len, (512, 256, 128, 32))
    halo = 16
    nb = seq_len // tm
    b0 = row0 // tm
    h0 = row0 // halo
    per = tm // halo
    d = D_CONV

    n_tail = _tail_blocks(into, row0 + nseq * seq_len, z_big.shape[0], tm)
    assert n_tail == 0 or nseq == 1
    real = lambda i: jnp.minimum(i, nb - 1)

    def hmap(col):
        return lambda s, i: (jnp.maximum(h0 + (s * nb + real(i)) * per - 1, 0), col)

    def bmap(col, clamp=True):
        return lambda s, i: (b0 + s * nb + (real(i) if clamp else i), col)

    return _fill_call(
        functools.partial(_conv_kernel, tm=tm, halo=halo), into=into, tail=(1, nb) if n_tail else None,
        grid=(nseq, nb + n_tail),
        in_specs=[pl.BlockSpec((None, 8, d), lambda s, i: (s, 0, 0)),
                  pl.BlockSpec((halo, d), hmap(1)),
                  pl.BlockSpec((halo, d), hmap(2)),
                  pl.BlockSpec((tm, d), bmap(0)),
                  pl.BlockSpec((tm, d), bmap(1)),
                  pl.BlockSpec((tm, d), bmap(2)),
                  pl.BlockSpec((8, d), lambda s, i: (0, 0))],
        args=(state, z_big, z_big, z_big, z_big, z_big, jnp.pad(conv_w, ((0, 5), (0, 0)))),
        out_specs=[pl.BlockSpec((tm, d), bmap(0, clamp=False)),
                   pl.BlockSpec((None, 8, d), lambda s, i: (s * nb + real(i), 0, 0))],
        out_shape=[jax.ShapeDtypeStruct((z_big.shape[0], d), BF16),
                   jax.ShapeDtypeStruct((nseq * nb, 8, d), F32)],
        compiler_params=_params("parallel", "arbitrary"),
        name="conv_gate",
    )


def _attn_latent_kernel(q_ref, wk_ref, wv_ref, cp_ref, krp_ref, cn_ref, krn_ref, o_ref, qa_ref, acc_ref,
                        *, past, tk):
    nh, dseq, _ = q_ref.shape
    rows = nh * dseq
    for h in range(nh):
        qh = q_ref[h]
        qa = lax.dot_general(qh[:, :QK_NOPE], wk_ref[h], NT_DIMS, preferred_element_type=F32)
        qa_ref[h * dseq:(h + 1) * dseq, 0:KV_RANK] = qa.astype(qa_ref.dtype)
        qa_ref[h * dseq:(h + 1) * dseq, KV_RANK:KV_RANK + QK_ROPE] = qh[:, QK_NOPE:QK_DIM]
    q = qa_ref[...]
    acc_ref[...] = jnp.zeros(acc_ref.shape, F32)

    def update(carry, s, c):
        m, l = carry
        m_new = jnp.maximum(m, jnp.max(s, axis=1, keepdims=True))
        p = jnp.exp2(s - m_new)
        alpha = jnp.exp2(m - m_new)
        acc_ref[...] = alpha * acc_ref[...] + jnp.dot(p.astype(BF16), c, preferred_element_type=F32)
        return m_new, alpha * l + jnp.sum(p, axis=1, keepdims=True)

    def past_step(j, carry):
        ks = pl.ds(pl.multiple_of(j * tk, tk), tk)
        c = cp_ref[ks, :].astype(BF16)
        k = jnp.concatenate([c, krp_ref[ks, :].astype(BF16)], axis=1)
        return update(carry, lax.dot_general(q, k, NT_DIMS, preferred_element_type=F32), c)

    carry = (jnp.full((rows, 1), MASK_VALUE, F32), jnp.zeros((rows, 1), F32))
    carry = lax.fori_loop(0, past // tk, past_step, carry)

    cn = cn_ref[...].astype(BF16)
    kn = jnp.concatenate([cn, krn_ref[...].astype(BF16)], axis=1)
    s = lax.dot_general(q, kn, NT_DIMS, preferred_element_type=F32)
    q_pos = past + lax.broadcasted_iota(jnp.int32, (rows, 1), 0) % dseq
    k_pos = past + lax.broadcasted_iota(jnp.int32, (1, dseq), 1)
    s = jnp.where(k_pos // CHUNK <= q_pos // CHUNK, s, MASK_VALUE)
    m, l = update(carry, s, cn)

    lat = (acc_ref[...] / l).astype(BF16)
    for h in range(nh):
        o = jnp.dot(lat[h * dseq:(h + 1) * dseq, :], wv_ref[h], preferred_element_type=F32)
        o_ref[:, h * V_DIM:(h + 1) * V_DIM] = o.astype(o_ref.dtype)


def _attention_latent(q, wk, wv, c_past, kr_past, c_new, kr_new, row0, into):
    nb, past, _ = c_past.shape
    dseq = c_new.shape[1]
    tk = _pick(past, (512, 256, 128))
    assert row0 % dseq == 0
    b0 = row0 // dseq
    per_b = lambda w: pl.BlockSpec((None,) + w, lambda b: (b, 0, 0))
    whole = lambda a: pl.BlockSpec(a.shape, lambda b: (0, 0, 0))
    return _fill_call(
        functools.partial(_attn_latent_kernel, past=past, tk=tk), into=into,
        grid=(nb,),
        in_specs=[pl.BlockSpec((N_HEADS, dseq, QK_DIM), lambda b: (0, b, 0)),
                  whole(wk), whole(wv),
                  per_b((past, KV_RANK)), per_b((past, QK_ROPE)),
                  per_b((dseq, KV_RANK)), per_b((dseq, QK_ROPE))],
        args=(q, wk, wv, c_past, kr_past, c_new, kr_new),
        out_specs=pl.BlockSpec((dseq, N_HEADS * V_DIM), lambda b: (b0 + b, 0)),
        out_shape=jax.ShapeDtypeStruct(into.shape, into.dtype),
        scratch_shapes=[pltpu.VMEM((N_HEADS * dseq, KV_RANK + QK_ROPE), BF16),
                        pltpu.VMEM((N_HEADS * dseq, KV_RANK), F32)],
        compiler_params=_params("parallel"),
        name="attention_latent",
    )


def _attn_t_kernel(q_ref, k_ref, kr_ref, vt_ref, o_ref, m_ref, l_ref, acc_ref, s0_ref, s1_ref, *, tb):
    i = pl.program_id(1)
    q = q_ref[...]
    q_pos = i * tb + lax.broadcasted_iota(jnp.int32, (1, tb), 1)
    col_end = (q_pos // CHUNK + 1) * CHUNK

    m_ref[...] = jnp.full(m_ref.shape, MASK_VALUE, F32)
    l_ref[...] = jnp.zeros(l_ref.shape, F32)
    acc_ref[...] = jnp.zeros(acc_ref.shape, F32)

    def produce(s_ref, j):
        ks = pl.ds(pl.multiple_of(j * tb, tb), tb)
        k = jnp.concatenate([k_ref[ks, :], kr_ref[ks, :]], axis=1)
        s_ref[...] = lax.dot_general(k, q, NT_DIMS, preferred_element_type=F32)

    def consume(s_ref, j, masked):
        s = s_ref[...]
        if masked:
            k_pos = j * tb + lax.broadcasted_iota(jnp.int32, (tb, 1), 0)
            s = jnp.where(k_pos < col_end, s, MASK_VALUE)
        m_old = m_ref[...]
        m_new = jnp.maximum(m_old, jnp.max(s, axis=0, keepdims=True))
        p = jnp.exp2(s - m_new)
        alpha = jnp.exp2(m_old - m_new)
        m_ref[...] = m_new
        l_ref[...] = alpha * l_ref[...] + jnp.sum(p, axis=0, keepdims=True)
        acc_ref[...] = alpha * acc_ref[...] + jnp.dot(vt_ref[j], p.astype(BF16), preferred_element_type=F32)

    produce(s0_ref, 0)

    bufs = (s0_ref, s1_ref)
    unroll = 8

    def group(g, carry):
        j0 = unroll * g
        for t in range(unroll):
            produce(bufs[(t + 1) % 2], j0 + t + 1)
            consume(bufs[t % 2], j0 + t, False)
        return carry

    lax.fori_loop(0, i // unroll, group, 0)
    j0 = i // unroll * unroll

    for rem in range(unroll):
        @pl.when(i - j0 == rem)
        def _(rem=rem):
            for t in range(rem + 1):
                if t < rem:
                    produce(bufs[(t + 1) % 2], j0 + t + 1)
                consume(bufs[t % 2], j0 + t, t == rem)

    o_ref[...] = (acc_ref[...] / l_ref[...]).T.astype(o_ref.dtype)


def _attention_t(q, k, kr, vt, seq, tk, out_rows):
    tq = tk
    assert tk % CHUNK == 0 and seq % tk == 0
    nq = seq // tq
    n_tail = _tail_blocks(None, seq, out_rows, tq)
    return _fill_call(
        functools.partial(_attn_t_kernel, tb=tk), into=None, tail=(1, nq) if n_tail else None,
        grid=(N_HEADS, nq + n_tail),
        in_specs=[pl.BlockSpec((None, tq, QK_DIM), lambda h, i: (h, jnp.minimum(i, nq - 1), 0)),
                  pl.BlockSpec((None, seq, QK_NOPE), lambda h, i: (h, 0, 0)),
                  pl.BlockSpec((seq, QK_ROPE), lambda h, i: (0, 0)),
                  pl.BlockSpec((None, seq // tk, V_DIM, tk), lambda h, i: (h, 0, 0, 0))],
        args=(q, k, kr, vt),
        out_specs=pl.BlockSpec((tq, V_DIM), lambda h, i: (i, h)),
        out_shape=jax.ShapeDtypeStruct((out_rows, N_HEADS * V_DIM), BF16),
        scratch_shapes=[pltpu.VMEM((1, tq), F32), pltpu.VMEM((1, tq), F32), pltpu.VMEM((V_DIM, tq), F32),
                        pltpu.VMEM((tk, tq), F32), pltpu.VMEM((tk, tq), F32)],
        compiler_params=_params("parallel", "arbitrary"),
        name="attention_t",
    )


def _merge_kernel(a_ref, o_ref, wa_ref, wb_ref, ga_ref, gb_ref, m_ref):
    ya = jnp.dot(a_ref[...], wa_ref[...], preferred_element_type=F32)
    yb = jnp.dot(o_ref[...], wb_ref[...], preferred_element_type=F32)
    m = jax.nn.sigmoid(ga_ref[...].astype(F32)) * ya + jax.nn.sigmoid(gb_ref[...].astype(F32)) * yb
    m_ref[...] = m.astype(m_ref.dtype)


def _merge(ya_in, o, wa, wb, z_big):
    t, k = ya_in.shape
    n = wa.shape[1]
    tm = _pick(t, (1280, 1024, 512, 256, 128))
    tn = _pick(n, (512, 256, 128))
    ga_blk = (3 * D_CONV + N_HEADS * (QK_NOPE + 2 * QK_ROPE)) // tn
    gb_blk = ga_blk + D_MODEL // tn
    return pl.pallas_call(
        _merge_kernel,
        grid=(t // tm, n // tn),
        in_specs=[pl.BlockSpec((tm, k), lambda i, j: (i, 0)),
                  pl.BlockSpec((tm, k), lambda i, j: (i, 0)),
                  pl.BlockSpec((k, tn), lambda i, j: (0, j)),
                  pl.BlockSpec((k, tn), lambda i, j: (0, j)),
                  pl.BlockSpec((tm, tn), lambda i, j: (i, ga_blk + j)),
                  pl.BlockSpec((tm, tn), lambda i, j: (i, gb_blk + j))],
        out_specs=pl.BlockSpec((tm, tn), lambda i, j: (i, j)),
        out_shape=jax.ShapeDtypeStruct((t, n), BF16),
        compiler_params=_params("parallel", "arbitrary"),
        name="merge",
    )(ya_in, o, wa, wb, z_big, z_big)


def _extract_top(work_ref, rank_ref, val_ref, idx_ref, *, track_rank):
    nk = PEER_NKEYS
    tm = work_ref.shape[1]
    neg = jnp.full((8, tm), -jnp.inf, F32)

    def round_(p, carry):
        idx_prev, pf = carry
        parts = [neg, neg, neg, neg]
        for k in range(nk):
            rows = slice(8 * k, 8 * k + 8)
            hit = idx_prev == float(k)
            w = jnp.where(hit, -jnp.inf, work_ref[rows, :])
            work_ref[rows, :] = w
            if track_rank:
                rank_ref[rows, :] = jnp.where(hit, pf - 1.0, rank_ref[rows, :])
            parts[k % 4] = jnp.maximum(parts[k % 4], w)
        m = jnp.maximum(jnp.maximum(parts[0], parts[1]), jnp.maximum(parts[2], parts[3]))
        big = jnp.full((8, tm), float(nk), F32)
        iparts = [big, big, big, big]
        for k in range(nk):
            w = work_ref[8 * k:8 * k + 8, :]
            iparts[k % 4] = jnp.minimum(iparts[k % 4], jnp.where(w == m, float(k), float(nk)))
        idx = jnp.minimum(jnp.minimum(iparts[0], iparts[1]), jnp.minimum(iparts[2], iparts[3]))
        val_ref[p] = m
        idx_ref[p] = idx
        return idx, pf + 1.0

    idx_last, _ = lax.fori_loop(0, PEER_TOPK, round_,
                                (jnp.full((8, tm), -1.0, F32), jnp.zeros((8, tm), F32)))
    if track_rank:
        for k in range(nk):
            rows = slice(8 * k, 8 * k + 8)
            rank_ref[rows, :] = jnp.where(idx_last == float(k), float(PEER_TOPK - 1), rank_ref[rows, :])


def _peer_select_kernel(qp_ref, k1_ref, k2_ref, k2h_ref, perm_ref,
                        r2_ref, e2_ref, n1_ref, w1_ref,
                        work_ref, rank_ref, a_ref, ia_ref, b_ref, ib_ref):
    nk = PEER_NKEYS
    qp = qp_ref[...]
    tm = qp.shape[0]
    s2h = lax.dot_general(k2h_ref[...], qp, NT_DIMS, preferred_element_type=F32)
    for h in range(PEER_HEADS):
        blk = s2h[h * nk:(h + 1) * nk, :]
        e2_ref[h * nk:(h + 1) * nk, :] = jnp.exp(blk - jnp.max(blk, axis=0, keepdims=True))

    work_ref[...] = lax.dot_general(k1_ref[...], qp, NT_DIMS, preferred_element_type=F32)
    _extract_top(work_ref, rank_ref, a_ref, ia_ref, track_rank=False)
    work_ref[...] = lax.dot_general(k2_ref[...], qp, NT_DIMS, preferred_element_type=F32)
    rank_ref[...] = jnp.full(rank_ref.shape, float(PEER_TOPK), F32)
    _extract_top(work_ref, rank_ref, b_ref, ib_ref, track_rank=True)
    r2_ref[...] = jnp.dot(perm_ref[...], rank_ref[...].astype(BF16), preferred_element_type=F32)

    a = [a_ref[p] for p in range(PEER_TOPK)]
    b = [b_ref[q] for q in range(PEER_TOPK)]
    cand = [a[p] + b[q] for (p, q) in _PAIRS]
    npair = len(_PAIRS)
    beaten = [jnp.zeros((8, tm), F32) for _ in range(npair)]
    for x in range(npair):
        px, qx = _PAIRS[x]
        for y in range(x + 1, npair):
            py, qy = _PAIRS[y]
            if px <= py and qx <= qy:
                beaten[y] = beaten[y] + 1.0
            else:
                gt = jnp.where(cand[y] > cand[x], 1.0, 0.0)
                beaten[x] = beaten[x] + gt
                beaten[y] = beaten[y] + (1.0 - gt)
    sel = [jnp.where(bt < float(PEER_TOPK), 1.0, 0.0) for bt in beaten]
    ea = [jnp.exp(a[p] - a[0]) for p in range(PEER_TOPK)]
    eb = [jnp.exp(b[q] - b[0]) for q in range(PEER_TOPK)]
    z = jnp.zeros((8, tm), F32)
    cnt = [jnp.zeros((8, tm), F32) for _ in range(PEER_TOPK)]
    for x, (p, q) in enumerate(_PAIRS):
        z = z + sel[x] * (ea[p] * eb[q])
        cnt[p] = cnt[p] + sel[x]
    inv_z = 1.0 / z
    w1 = [ea[p] * inv_z for p in range(PEER_TOPK)]
    ia = [ia_ref[p] for p in range(PEER_TOPK)]
    zero = jnp.zeros((8, tm), F32)
    for k in range(nk):
        n1k = zero
        w1k = zero
        for p in range(PEER_TOPK):
            hit = ia[p] == float(k)
            n1k = jnp.where(hit, cnt[p], n1k)
            w1k = jnp.where(hit, w1[p], w1k)
        n1_ref[8 * k:8 * k + 8, :] = n1k
        w1_ref[8 * k:8 * k + 8, :] = w1k


def _peer_select(qp, k1, k2, k2h, perm):
    t, d = qp.shape
    tm = _pick(t, (256, 128))
    rows = PEER_NKEYS * PEER_HEADS
    full = lambda i: (0, 0)
    out = jax.ShapeDtypeStruct((rows, t), F32)
    ospec = pl.BlockSpec((rows, tm), lambda i: (0, i))
    return pl.pallas_call(
        _peer_select_kernel,
        grid=(t // tm,),
        in_specs=[pl.BlockSpec((tm, d), lambda i: (i, 0)),
                  pl.BlockSpec((rows, d), full),
                  pl.BlockSpec((rows, d), full),
                  pl.BlockSpec((rows, d), full),
                  pl.BlockSpec((rows, rows), full)],
        out_specs=[ospec, ospec, ospec, ospec],
        out_shape=[out, out, out, out],
        scratch_shapes=[pltpu.VMEM((rows, tm), F32), pltpu.VMEM((rows, tm), F32),
                        pltpu.VMEM((PEER_TOPK, 8, tm), F32), pltpu.VMEM((PEER_TOPK, 8, tm), F32),
                        pltpu.VMEM((PEER_TOPK, 8, tm), F32), pltpu.VMEM((PEER_TOPK, 8, tm), F32)],
        compiler_params=_params("parallel"),
        name="peer_select",
    )(qp, k1, k2, k2h, perm)


def _gelu_tanh(x):
    c = math.sqrt(2.0 / math.pi)
    return 0.5 * x * (1.0 + jnp.tanh(c * (x + 0.044715 * (x * x * x))))


def _peer_main_kernel(hn_ref, u_ref, va_ref, vb_ref, r2_ref, e2_ref, n1_ref, w1_ref,
                      o_ref, ga_ref, gb_ref, pa_ref, pb_ref, act_ref, *, tc):
    nk = PEER_NKEYS
    te, tm = ga_ref.shape
    j = pl.program_id(1)
    last = pl.num_programs(1) - 1

    hn = hn_ref[...]

    nw = 256
    d = o_ref.shape[1]

    n_il = te // nk
    jr = 64

    def gate_tile(g_ref, nref, wref, half, c, jh):
        cols = slice(c * tc, (c + 1) * tc)
        accs = [jnp.zeros((jr, tc), F32) for _ in range(n_il)]
        for h in range(PEER_HEADS):
            hr = slice(h * nk + jh * jr, h * nk + (jh + 1) * jr)
            r2 = r2_ref[hr, cols]
            e2 = e2_ref[hr, cols]
            for il in range(n_il):
                r = (half * n_il + il) * PEER_HEADS + h
                keep = r2 < nref[r:r + 1, cols]
                accs[il] = accs[il] + jnp.where(keep, e2, 0.0) * wref[r:r + 1, cols]
        for il in range(n_il):
            g_ref[il * nk + jh * jr:il * nk + (jh + 1) * jr, cols] = accs[il]

    def gate(g_ref, nref, wref, half):
        for c in range(tm // tc):
            for jh in range(nk // jr):
                gate_tile(g_ref, nref, wref, half, c, jh)

    @pl.when(j == 0)
    def _():
        o_ref[...] = jnp.zeros(o_ref.shape, o_ref.dtype)
        pb_ref[...] = jnp.zeros(pb_ref.shape, pb_ref.dtype)

    def act_piece(half, n):
        u = u_ref[half * te + n * nw:half * te + (n + 1) * nw, :]
        return _gelu_tanh(lax.dot_general(hn, u, NT_DIMS, preferred_element_type=F32))

    def out_piece(p_ref, v_ref, n):
        cols = slice(n * nw, (n + 1) * nw)
        o_ref[:, cols] += jnp.dot(p_ref[...], v_ref[:, cols], preferred_element_type=F32)

    def sub_block(half, g_ref, p_new_ref, p_old_ref, v_old_ref, next_gate):
        for n in range(te // nw):
            act_ref[:, n * nw:(n + 1) * nw] = act_piece(half, n)
        for n in range(d // nw):
            out_piece(p_old_ref, v_old_ref, n)
        gate(*next_gate)
        for il in range(n_il):
            ecols = slice(il * nk, (il + 1) * nk)
            for c in range(tm // tc):
                rows = slice(c * tc, (c + 1) * tc)
                g = g_ref[ecols, rows].T
                p_new_ref[rows, ecols] = (g * act_ref[rows, ecols]).astype(p_new_ref.dtype)

    @pl.when(j < last)
    def _():
        sub_block(0, ga_ref, pa_ref, pb_ref, vb_ref, (ga_ref, n1_ref, w1_ref, 0))
        sub_block(1, gb_ref, pb_ref, pa_ref, va_ref, (gb_ref, n1_ref, w1_ref, 1))

    @pl.when(j == last)
    def _():
        for n in range(d // nw):
            out_piece(pb_ref, vb_ref, n)


def _peer_main(hn, u, v, r2, e2, n1, w1):
    t, d = hn.shape
    e = u.shape[0]
    tm = _pick(t, (640, 512, 256, 128))
    te = 512
    tc = 128
    rows = PEER_NKEYS * PEER_HEADS
    sub = 2 * te // PEER_NKEYS * PEER_HEADS
    nj = e // (2 * te)
    cur = lambda j: jnp.minimum(j, nj - 1)
    return pl.pallas_call(
        functools.partial(_peer_main_kernel, tc=tc),
        grid=(t // tm, nj + 1),
        in_specs=[pl.BlockSpec((tm, d), lambda i, j: (i, 0)),
                  pl.BlockSpec((2 * te, d), lambda i, j: (cur(j), 0)),
                  pl.BlockSpec((te, d), lambda i, j: (2 * cur(j), 0)),
                  pl.BlockSpec((te, d), lambda i, j: (jnp.maximum(2 * j - 1, 0), 0)),
                  pl.BlockSpec((rows, tm), lambda i, j: (0, i)),
                  pl.BlockSpec((rows, tm), lambda i, j: (0, i)),
                  pl.BlockSpec((sub, tm), lambda i, j: (cur(j), i)),
                  pl.BlockSpec((sub, tm), lambda i, j: (cur(j), i))],
        out_specs=pl.BlockSpec((tm, d), lambda i, j: (i, 0)),
        out_shape=jax.ShapeDtypeStruct((t, d), F32),
        scratch_shapes=[pltpu.VMEM((te, tm), F32), pltpu.VMEM((te, tm), F32),
                        pltpu.VMEM((tm, te), BF16), pltpu.VMEM((tm, te), BF16),
                        pltpu.VMEM((tm, te), F32)],
        compiler_params=_params("parallel", "arbitrary"),
        name="peer_main",
    )(hn, u, v, v, r2, e2, n1, w1)


def _ple_final_kernel(h_ref, f_ref, p_ref, wg_ref, wp_ref, gp_ref, gf_ref, o_ref):
    h = h_ref[...] + f_ref[...]
    gate = jnp.dot(_rms(h, gp_ref[...]).astype(BF16), wg_ref[...], preferred_element_type=F32)
    pe = jnp.dot(p_ref[...].astype(BF16), wp_ref[...], preferred_element_type=F32)
    o_ref[...] = _rms(h + pe * jax.nn.sigmoid(gate), gf_ref[...])


def _ple_final(h, f, p, wg, wp, g_ple, g_final, row0):
    d = h.shape[1]
    rows, pd = p.shape
    tm = _pick(math.gcd(rows, row0), (256, 128))
    b0 = row0 // tm
    whole = lambda i: (0, 0)
    row = pl.BlockSpec((tm, d), lambda i: (b0 + i, 0))
    return pl.pallas_call(
        _ple_final_kernel,
        grid=(rows // tm,),
        in_specs=[row, row,
                  pl.BlockSpec((tm, pd), lambda i: (i, 0)),
                  pl.BlockSpec((d, d), whole),
                  pl.BlockSpec((pd, d), whole),
                  pl.BlockSpec((1, d), whole),
                  pl.BlockSpec((1, d), whole)],
        out_specs=pl.BlockSpec((tm, d), lambda i: (i, 0)),
        out_shape=jax.ShapeDtypeStruct((rows, d), F32),
        compiler_params=_params("parallel"),
        name="ple_final",
    )(h, f, p, wg, wp, g_ple.reshape(1, d), g_final.reshape(1, d))


def _swap_halves(w):
    half = w.shape[-1] // 2
    return jnp.concatenate([w[..., half:], w[..., :half]], axis=-1)


def _split_w_in(w_in):
    d = w_in.shape[0]
    splits = [int(s) for s in np.cumsum(IN_SIZES)[:-1]]
    wb, wc, wx, wq, wckv, wkr, wga, wgb = jnp.split(w_in, splits, axis=1)
    wq = wq.reshape(d, N_HEADS, QK_DIM)
    wq_n = wq[:, :, :QK_NOPE].reshape(d, N_HEADS * QK_NOPE)
    wq_r = wq[:, :, QK_NOPE:]
    w_big = jnp.concatenate(
        [wb, wc, wx, wq_n, wq_r.reshape(d, -1), _swap_halves(wq_r).reshape(d, -1), wga, wgb], axis=1)
    w_small = jnp.concatenate([wckv, wkr, _swap_halves(wkr)], axis=1)
    return w_big.astype(BF16), w_small.astype(BF16)


def _rope_tables(pos):
    inv = 1.0 / (ROPE_THETA ** (jnp.arange(0, QK_ROPE, 2, dtype=F32) / QK_ROPE))
    ang = pos.astype(F32)[:, None] * inv[None, :]
    cos, sin = jnp.cos(ang), jnp.sin(ang)
    return jnp.concatenate([cos, cos], axis=1), jnp.concatenate([-sin, sin], axis=1)


def _key_matrices(sub_keys):
    nk, hd, half = PEER_NKEYS, PEER_HEADS, PEER_DKEY // 2
    eye = jnp.eye(hd, dtype=F32)

    def build(c, head_major):
        sel = jnp.zeros((2,), F32).at[c].set(1.0)
        m = sub_keys[c][:, None, None, None, :] * eye[None, :, :, None, None] * sel[None, None, None, :, None]
        if head_major:
            m = jnp.transpose(m, (1, 0, 2, 3, 4))
        return m.reshape(nk * hd, hd * PEER_DKEY).astype(BF16)

    r = np.arange(nk * hd)
    perm = np.zeros((nk * hd, nk * hd), np.float32)
    perm[(r % hd) * nk + r // hd, r] = 1.0
    return build(0, False), build(1, False), build(1, True), jnp.asarray(perm, BF16)


def kernel(x_prompt, x_sample, cache_conv, cache_ckv, cache_krope, p_prompt, p_sample, g_mix, w_in, conv_w, g_kv, w_kv_b, w_a_out, w_b_out, w_o, g_ffn, w_pq, sub_keys, u_tab, v_tab, g_ple, w_ple_gate, w_ple, g_final):
    assert x_prompt.shape[0] == 1 and w_in.shape[0] == 1
    seq = x_prompt.shape[1]
    nb, dseq = x_sample.shape[0], x_sample.shape[1]
    past = cache_ckv.shape[2]
    d = D_MODEL
    ns = nb * dseq
    t = seq + ns

    x_p, x_s = x_prompt.reshape(seq, d), x_sample.reshape(ns, d)
    p_p, p_s = p_prompt[0].reshape(seq, -1), p_sample[0].reshape(ns, -1)
    pos = jnp.concatenate([jnp.arange(seq), jnp.tile(past + jnp.arange(dseq), nb)])
    cos64, sin64 = _rope_tables(pos)
    w_big, w_small = _split_w_in(w_in[0])

    xn = _rmsnorm(x_p, g_mix[0], BF16, 0, t)
    xn = _rmsnorm(x_s, g_mix[0], BF16, seq, t, into=xn)
    z_big = _matmul(xn, w_big, BF16, name="in_proj")
    z_small = _matmul(xn, w_small, F32, tn_prefs=(w_small.shape[1],), name="in_proj_small")
    c_p, kr_p = _post_small(z_small, g_kv[0], cos64, sin64, 0, seq)
    c_s, kr_s = _post_small(z_small, g_kv[0], cos64, sin64, seq, ns)

    zero_state = jnp.zeros((1, 8, D_CONV), F32)
    ya_in, last_p = _conv_gate(z_big, zero_state, conv_w[0], 0, 1, seq)
    state_s = jnp.pad(cache_conv[0], ((0, 0), (6, 0), (0, 0)))
    ya_in, last_s = _conv_gate(z_big, state_s, conv_w[0], seq, nb, dseq, into=ya_in)
    conv_p = last_p[-1, 6:8, :].reshape(1, 1, 2, D_CONV)
    if dseq >= 2:
        conv_s = last_s[:, 6:8, :].reshape(1, nb, 2, D_CONV)
    else:
        raise NotImplementedError("sample blocks shorter than the convolution state")

    w_kv_h = w_kv_b[0].astype(BF16).reshape(KV_RANK, N_HEADS, QK_NOPE + V_DIM)
    wk_h = jnp.transpose(w_kv_h[:, :, :QK_NOPE], (1, 0, 2))
    wv_h = jnp.transpose(w_kv_h[:, :, QK_NOPE:], (1, 0, 2))
    wk = w_kv_h[:, :, :QK_NOPE].reshape(KV_RANK, N_HEADS * QK_NOPE)
    wvt = w_kv_h[:, :, QK_NOPE:].reshape(KV_RANK, N_HEADS * V_DIM).T
    q_p = _q_prep(z_big, cos64, sin64, 0, seq)
    q_s = _q_prep(z_big, cos64, sin64, seq, ns)
    tk = _pick(seq, (512, 256, 128))
    k_p, vt_p = _kv_proj_t(c_p, wk, wvt, tk)
    o_all = _attention_t(q_p, k_p, kr_p.astype(BF16), vt_p, seq, tk, t)
    o_all = _attention_latent(q_s, wk_h, wv_h, cache_ckv[0], cache_krope[0],
                              c_s.reshape(nb, dseq, KV_RANK), kr_s.reshape(nb, dseq, QK_ROPE), seq, o_all)

    m = _merge(ya_in, o_all, w_a_out[0].astype(BF16), w_b_out[0].astype(BF16), z_big)
    w_out = w_o[0].astype(BF16)
    h1, hn = _out_proj_norm(m, w_out, x_p, g_ffn[0], 0)
    h1, hn = _out_proj_norm(m, w_out, x_s, g_ffn[0], seq, into=(h1, hn))

    qp = _matmul(hn, w_pq[0].astype(BF16), BF16, name="peer_query")
    k1, k2, k2h, perm = _key_matrices(sub_keys[0])
    r2, e2, n1, w1 = _peer_select(qp, k1, k2, k2h, perm)
    peer = _peer_main(hn, u_tab[0].astype(BF16), v_tab[0].astype(BF16), r2, e2, n1, w1)

    wg, wp = w_ple_gate[0].astype(BF16), w_ple[0].astype(BF16)
    y_prompt = _ple_final(h1, peer, p_p, wg, wp, g_ple[0], g_final, 0).reshape(1, seq, d)
    y_sample = _ple_final(h1, peer, p_s, wg, wp, g_ple[0], g_final, seq).reshape(nb, dseq, d)

    ckv_p = c_p.reshape(1, 1, seq, KV_RANK)
    kr_p = kr_p.reshape(1, 1, seq, QK_ROPE)
    ckv_s = c_s.reshape(1, nb, dseq, KV_RANK)
    kr_s = kr_s.reshape(1, nb, dseq, QK_ROPE)
    return (y_prompt, y_sample, conv_p, ckv_p, kr_p, conv_s, ckv_s, kr_s)
```

```python
import functools
import math

import numpy as np
import jax
import jax.numpy as jnp
from jax import lax
from jax.experimental import pallas as pl
from jax.experimental.pallas import tpu as pltpu

D_MODEL = 2048
D_CONV = 2048
N_HEADS = 16
QK_NOPE = 128
QK_ROPE = 64
QK_DIM = QK_NOPE + QK_ROPE
V_DIM = 128
KV_RANK = 512
CHUNK = 64
ROPE_THETA = 10000.0
PEER_HEADS = 8
PEER_NKEYS = 128
PEER_TOPK = 16
PEER_DKEY = 256
PEER_EXPERTS = PEER_NKEYS * PEER_NKEYS
RMS_EPS = 1e-6
IN_SIZES = (D_CONV, D_CONV, D_CONV, N_HEADS * QK_DIM, KV_RANK, QK_ROPE, D_MODEL, D_MODEL)

BF16 = jnp.bfloat16
F32 = jnp.float32
VMEM_LIMIT_BYTES = 56 * 1024 * 1024
MASK_VALUE = -1e30
NT_DIMS = (((1,), (1,)), ((), ()))
TN_DIMS = (((0,), (0,)), ((), ()))

_PAIRS = tuple((p, q) for p in range(PEER_TOPK) for q in range(PEER_TOPK)
               if (p + 1) * (q + 1) <= PEER_TOPK)


def _pick(n, prefs):
    for p in prefs:
        if n % p == 0:
            return p
    return n


def _params(*sem):
    return pltpu.CompilerParams(dimension_semantics=sem, vmem_limit_bytes=VMEM_LIMIT_BYTES)


def _rms(x, g):
    ms = jnp.mean(x * x, axis=-1, keepdims=True)
    return x * lax.rsqrt(ms + RMS_EPS) * g


def _rms_kernel(x_ref, g_ref, o_ref):
    o_ref[...] = _rms(x_ref[...], g_ref[...]).astype(o_ref.dtype)


def _fill_call(kern, *, into, in_specs, args, tail=None, n_shared=1, **kwargs):
    if tail is not None:
        axis, n_real = tail
        n_in = len(in_specs)
        body = kern

        def kern(*refs):
            i = pl.program_id(axis)

            @pl.when(i < n_real)
            def _():
                body(*refs)

            @pl.when(i >= n_real)
            def _():
                for o in refs[n_in:n_in + n_shared]:
                    o[...] = jnp.zeros(o.shape, o.dtype)

    if into is None:
        return pl.pallas_call(kern, in_specs=in_specs, **kwargs)(*args)
    bufs = tuple(into) if isinstance(into, (tuple, list)) else (into,)
    assert len(bufs) == n_shared

    def filling(*refs):
        kern(*refs[n_shared:])

    return pl.pallas_call(filling, in_specs=[pl.BlockSpec(memory_space=pl.ANY)] * n_shared + list(in_specs),
                          input_output_aliases={k: k for k in range(n_shared)}, **kwargs)(*bufs, *args)


def _tail_blocks(into, end_row, total_rows, tm):
    if into is not None or end_row >= total_rows:
        return 0
    return -(-(total_rows - end_row) // tm)


def _rmsnorm(x, g, out_dtype, out_row0, out_rows, into=None):
    rows, d = x.shape
    tm = _pick(math.gcd(rows, out_row0), (1024, 640, 512, 256, 128))
    bo = out_row0 // tm
    nb = rows // tm
    n_tail = _tail_blocks(into, out_row0 + rows, out_rows, tm)
    return _fill_call(
        _rms_kernel, into=into, tail=(0, nb) if n_tail else None,
        grid=(nb + n_tail,),
        in_specs=[pl.BlockSpec((tm, d), lambda i: (jnp.minimum(i, nb - 1), 0)),
                  pl.BlockSpec((1, d), lambda i: (0, 0))],
        args=(x, g.reshape(1, d)),
        out_specs=pl.BlockSpec((tm, d), lambda i: (bo + i, 0)),
        out_shape=jax.ShapeDtypeStruct((out_rows, d), out_dtype),
        compiler_params=_params("parallel"),
        name="rmsnorm",
    )


def _mm_kernel(x_ref, w_ref, o_ref):
    o_ref[...] = jnp.dot(x_ref[...], w_ref[...], preferred_element_type=F32).astype(o_ref.dtype)


def _matmul(x, w, out_dtype, tn_prefs=(1024, 512, 256, 128), name="matmul"):
    t, k = x.shape
    n = w.shape[1]
    tm = _pick(t, (1280, 1024, 768, 512, 256, 128))
    tn = _pick(n, tn_prefs)
    return pl.pallas_call(
        _mm_kernel,
        grid=(t // tm, n // tn),
        in_specs=[pl.BlockSpec((tm, k), lambda i, j: (i, 0)),
                  pl.BlockSpec((k, tn), lambda i, j: (0, j))],
        out_specs=pl.BlockSpec((tm, tn), lambda i, j: (i, j)),
        out_shape=jax.ShapeDtypeStruct((t, n), out_dtype),
        compiler_params=_params("parallel", "arbitrary"),
        name=name,
    )(x, w)


def _out_proj_norm_kernel(m_ref, w_ref, x_ref, g_ref, h_ref, hn_ref):
    h = x_ref[...] + jnp.dot(m_ref[...], w_ref[...], preferred_element_type=F32)
    h_ref[...] = h
    hn_ref[...] = _rms(h, g_ref[...]).astype(hn_ref.dtype)


def _out_proj_norm(m, w, x_seg, g, row0, into=None):
    t, k = m.shape
    d = w.shape[1]
    rows = x_seg.shape[0]
    tm = _pick(math.gcd(rows, row0), (512, 256, 128))
    b0, nb = row0 // tm, rows // tm
    n_tail = _tail_blocks(into, row0 + rows, t, tm)
    real = lambda i: jnp.minimum(i, nb - 1)
    whole = lambda i: (0, 0)
    row = pl.BlockSpec((tm, d), lambda i: (b0 + i, 0))
    return _fill_call(
        _out_proj_norm_kernel, into=into, tail=(0, nb) if n_tail else None, n_shared=2,
        grid=(nb + n_tail,),
        in_specs=[pl.BlockSpec((tm, k), lambda i: (b0 + real(i), 0)),
                  pl.BlockSpec((k, d), whole),
                  pl.BlockSpec((tm, d), lambda i: (real(i), 0)),
                  pl.BlockSpec((1, d), whole)],
        args=(m, w, x_seg, g.reshape(1, d)),
        out_specs=[row, row],
        out_shape=[jax.ShapeDtypeStruct((t, d), F32), jax.ShapeDtypeStruct((t, d), BF16)],
        compiler_params=_params("parallel"),
        name="out_proj",
    )


def _kv_proj_t_kernel(c_ref, wk_ref, wvt_ref, k_ref, vt_ref):
    c = c_ref[...].astype(BF16)
    k = jnp.dot(c, wk_ref[...], preferred_element_type=F32)
    vt = lax.dot_general(wvt_ref[...], c, NT_DIMS, preferred_element_type=F32)
    for h in range(N_HEADS):
        k_ref[h] = k[:, h * QK_NOPE:(h + 1) * QK_NOPE].astype(k_ref.dtype)
        vt_ref[h] = vt[h * V_DIM:(h + 1) * V_DIM, :].astype(vt_ref.dtype)


def _kv_proj_t(c, wk, wvt, tk):
    r, kd = c.shape
    return pl.pallas_call(
        _kv_proj_t_kernel,
        grid=(r // tk,),
        in_specs=[pl.BlockSpec((tk, kd), lambda i: (i, 0)),
                  pl.BlockSpec((kd, N_HEADS * QK_NOPE), lambda i: (0, 0)),
                  pl.BlockSpec((N_HEADS * V_DIM, kd), lambda i: (0, 0))],
        out_specs=[pl.BlockSpec((N_HEADS, tk, QK_NOPE), lambda i: (0, i, 0)),
                   pl.BlockSpec((N_HEADS, None, V_DIM, tk), lambda i: (0, i, 0, 0))],
        out_shape=[jax.ShapeDtypeStruct((N_HEADS, r, QK_NOPE), BF16),
                   jax.ShapeDtypeStruct((N_HEADS, r // tk, V_DIM, tk), BF16)],
        compiler_params=_params("parallel"),
        name="kv_proj_t",
    )(c, wk, wvt)


def _post_small_kernel(z_ref, g_ref, cos_ref, sin_ref, c_ref, kr_ref):
    z = z_ref[...]
    c_ref[...] = _rms(z[:, :KV_RANK], g_ref[...])
    kr = z[:, KV_RANK:KV_RANK + QK_ROPE]
    kr_sw = z[:, KV_RANK + QK_ROPE:KV_RANK + 2 * QK_ROPE]
    kr_ref[...] = kr * cos_ref[...] + kr_sw * sin_ref[...]


def _post_small(z_small, g_kv, cos64, sin64, row0, rows):
    w = z_small.shape[1]
    tm = _pick(math.gcd(rows, row0), (1024, 512, 256, 128))
    b0 = row0 // tm
    return pl.pallas_call(
        _post_small_kernel,
        grid=(rows // tm,),
        in_specs=[pl.BlockSpec((tm, w), lambda i: (b0 + i, 0)),
                  pl.BlockSpec((1, KV_RANK), lambda i: (0, 0)),
                  pl.BlockSpec((tm, QK_ROPE), lambda i: (b0 + i, 0)),
                  pl.BlockSpec((tm, QK_ROPE), lambda i: (b0 + i, 0))],
        out_specs=[pl.BlockSpec((tm, KV_RANK), lambda i: (i, 0)),
                   pl.BlockSpec((tm, QK_ROPE), lambda i: (i, 0))],
        out_shape=[jax.ShapeDtypeStruct((rows, KV_RANK), F32),
                   jax.ShapeDtypeStruct((rows, QK_ROPE), F32)],
        compiler_params=_params("parallel"),
        name="latent_post",
    )(z_small, g_kv.reshape(1, KV_RANK), cos64, sin64)


def _q_prep_kernel(qn_ref, qr_ref, qsw_ref, cos_ref, sin_ref, o_ref):
    scale = QK_DIM ** -0.5 * math.log2(math.e)
    cos = jnp.tile(cos_ref[...], (1, N_HEADS))
    sin = jnp.tile(sin_ref[...], (1, N_HEADS))
    qr = (qr_ref[...].astype(F32) * cos + qsw_ref[...].astype(F32) * sin) * scale
    qn = qn_ref[...].astype(F32) * scale
    for h in range(N_HEADS):
        o_ref[h, :, 0:QK_NOPE] = qn[:, h * QK_NOPE:(h + 1) * QK_NOPE].astype(o_ref.dtype)
        o_ref[h, :, QK_NOPE:QK_DIM] = qr[:, h * QK_ROPE:(h + 1) * QK_ROPE].astype(o_ref.dtype)


def _q_prep(z_big, cos64, sin64, row0, rows):
    tm = _pick(math.gcd(rows, row0) if row0 else rows, (256, 128, 32))
    b0 = row0 // tm
    nope_blk = 3 * D_CONV // (N_HEADS * QK_NOPE)
    rope_blk = (3 * D_CONV + N_HEADS * QK_NOPE) // (N_HEADS * QK_ROPE)
    return pl.pallas_call(
        _q_prep_kernel,
        grid=(rows // tm,),
        in_specs=[pl.BlockSpec((tm, N_HEADS * QK_NOPE), lambda i: (b0 + i, nope_blk)),
                  pl.BlockSpec((tm, N_HEADS * QK_ROPE), lambda i: (b0 + i, rope_blk)),
                  pl.BlockSpec((tm, N_HEADS * QK_ROPE), lambda i: (b0 + i, rope_blk + 1)),
                  pl.BlockSpec((tm, QK_ROPE), lambda i: (b0 + i, 0)),
                  pl.BlockSpec((tm, QK_ROPE), lambda i: (b0 + i, 0))],
        out_specs=pl.BlockSpec((N_HEADS, tm, QK_DIM), lambda i: (0, i, 0)),
        out_shape=jax.ShapeDtypeStruct((N_HEADS, rows, QK_DIM), BF16),
        compiler_params=_params("parallel"),
        name="q_prep",
    )(z_big, z_big, z_big, cos64, sin64)


def _conv_kernel(state_ref, pc_ref, px_ref, b_ref, c_ref, x_ref, w_ref, o_ref, last_ref, *, tm, halo):
    i = pl.program_id(1)
    u = c_ref[...].astype(F32) * x_ref[...].astype(F32)
    prev = pc_ref[...].astype(F32) * px_ref[...].astype(F32)
    st = state_ref[...]
    first = i == 0
    um1 = jnp.where(first, st[7:8, :], prev[halo - 1:halo, :])
    um2 = jnp.where(first, st[6:7, :], prev[halo - 2:halo - 1, :])
    row = lax.broadcasted_iota(jnp.int32, u.shape, 0)
    s1 = jnp.where(row == 0, um1, pltpu.roll(u, 1, 0))
    s2 = jnp.where(row == 0, um2, jnp.where(row == 1, um1, pltpu.roll(u, 2, 0)))
    w = w_ref[...]
    y = w[0:1, :] * s2 + w[1:2, :] * s1 + w[2:3, :] * u
    o_ref[...] = (b_ref[...].astype(F32) * y).astype(o_ref.dtype)
    last_ref[...] = u[tm - 8:tm, :]


def _conv_gate(z_big, state, conv_w, row0, nseq, seq_len, into=None):
    tm = _pick(seq_len, (512, 256, 128, 32))
    halo = 16
    nb = seq_len // tm
    b0 = row0 // tm
    h0 = row0 // halo
    per = tm // halo
    d = D_CONV

    n_tail = _tail_blocks(into, row0 + nseq * seq_len, z_big.shape[0], tm)
    assert n_tail == 0 or nseq == 1
    real = lambda i: jnp.minimum(i, nb - 1)

    def hmap(col):
        return lambda s, i: (jnp.maximum(h0 + (s * nb + real(i)) * per - 1, 0), col)

    def bmap(col, clamp=True):
        return lambda s, i: (b0 + s * nb + (real(i) if clamp else i), col)

    return _fill_call(
        functools.partial(_conv_kernel, tm=tm, halo=halo), into=into, tail=(1, nb) if n_tail else None,
        grid=(nseq, nb + n_tail),
        in_specs=[pl.BlockSpec((None, 8, d), lambda s, i: (s, 0, 0)),
                  pl.BlockSpec((halo, d), hmap(1)),
                  pl.BlockSpec((halo, d), hmap(2)),
                  pl.BlockSpec((tm, d), bmap(0)),
                  pl.BlockSpec((tm, d), bmap(1)),
                  pl.BlockSpec((tm, d), bmap(2)),
                  pl.BlockSpec((8, d), lambda s, i: (0, 0))],
        args=(state, z_big, z_big, z_big, z_big, z_big, jnp.pad(conv_w, ((0, 5), (0, 0)))),
        out_specs=[pl.BlockSpec((tm, d), bmap(0, clamp=False)),
                   pl.BlockSpec((None, 8, d), lambda s, i: (s * nb + real(i), 0, 0))],
        out_shape=[jax.ShapeDtypeStruct((z_big.shape[0], d), BF16),
                   jax.ShapeDtypeStruct((nseq * nb, 8, d), F32)],
        compiler_params=_params("parallel", "arbitrary"),
        name="conv_gate",
    )


def _attn_latent_kernel(q_ref, wk_ref, wv_ref, cp_ref, krp_ref, cn_ref, krn_ref, o_ref, qa_ref, acc_ref,
                        *, past, tk):
    nh, dseq, _ = q_ref.shape
    rows = nh * dseq
    for h in range(nh):
        qh = q_ref[h]
        qa = lax.dot_general(qh[:, :QK_NOPE], wk_ref[h], NT_DIMS, preferred_element_type=F32)
        qa_ref[h * dseq:(h + 1) * dseq, 0:KV_RANK] = qa.astype(qa_ref.dtype)
        qa_ref[h * dseq:(h + 1) * dseq, KV_RANK:KV_RANK + QK_ROPE] = qh[:, QK_NOPE:QK_DIM]
    q = qa_ref[...]
    acc_ref[...] = jnp.zeros(acc_ref.shape, F32)

    def update(carry, s, c):
        m, l = carry
        m_new = jnp.maximum(m, jnp.max(s, axis=1, keepdims=True))
        p = jnp.exp2(s - m_new)
        alpha = jnp.exp2(m - m_new)
        acc_ref[...] = alpha * acc_ref[...] + jnp.dot(p.astype(BF16), c, preferred_element_type=F32)
        return m_new, alpha * l + jnp.sum(p, axis=1, keepdims=True)

    def past_step(j, carry):
        ks = pl.ds(pl.multiple_of(j * tk, tk), tk)
        c = cp_ref[ks, :].astype(BF16)
        k = jnp.concatenate([c, krp_ref[ks, :].astype(BF16)], axis=1)
        return update(carry, lax.dot_general(q, k, NT_DIMS, preferred_element_type=F32), c)

    carry = (jnp.full((rows, 1), MASK_VALUE, F32), jnp.zeros((rows, 1), F32))
    carry = lax.fori_loop(0, past // tk, past_step, carry)

    cn = cn_ref[...].astype(BF16)
    kn = jnp.concatenate([cn, krn_ref[...].astype(BF16)], axis=1)
    s = lax.dot_general(q, kn, NT_DIMS, preferred_element_type=F32)
    q_pos = past + lax.broadcasted_iota(jnp.int32, (rows, 1), 0) % dseq
    k_pos = past + lax.broadcasted_iota(jnp.int32, (1, dseq), 1)
    s = jnp.where(k_pos // CHUNK <= q_pos // CHUNK, s, MASK_VALUE)
    m, l = update(carry, s, cn)

    lat = (acc_ref[...] / l).astype(BF16)
    for h in range(nh):
        o = jnp.dot(lat[h * dseq:(h + 1) * dseq, :], wv_ref[h], preferred_element_type=F32)
        o_ref[:, h * V_DIM:(h + 1) * V_DIM] = o.astype(o_ref.dtype)


def _attention_latent(q, wk, wv, c_past, kr_past, c_new, kr_new, row0, into):
    nb, past, _ = c_past.shape
    dseq = c_new.shape[1]
    tk = _pick(past, (512, 256, 128))
    assert row0 % dseq == 0
    b0 = row0 // dseq
    per_b = lambda w: pl.BlockSpec((None,) + w, lambda b: (b, 0, 0))
    whole = lambda a: pl.BlockSpec(a.shape, lambda b: (0, 0, 0))
    return _fill_call(
        functools.partial(_attn_latent_kernel, past=past, tk=tk), into=into,
        grid=(nb,),
        in_specs=[pl.BlockSpec((N_HEADS, dseq, QK_DIM), lambda b: (0, b, 0)),
                  whole(wk), whole(wv),
                  per_b((past, KV_RANK)), per_b((past, QK_ROPE)),
                  per_b((dseq, KV_RANK)), per_b((dseq, QK_ROPE))],
        args=(q, wk, wv, c_past, kr_past, c_new, kr_new),
        out_specs=pl.BlockSpec((dseq, N_HEADS * V_DIM), lambda b: (b0 + b, 0)),
        out_shape=jax.ShapeDtypeStruct(into.shape, into.dtype),
        scratch_shapes=[pltpu.VMEM((N_HEADS * dseq, KV_RANK + QK_ROPE), BF16),
                        pltpu.VMEM((N_HEADS * dseq, KV_RANK), F32)],
        compiler_params=_params("parallel"),
        name="attention_latent",
    )


def _attn_t_kernel(q_ref, k_ref, kr_ref, vt_ref, o_ref, m_ref, l_ref, acc_ref, s0_ref, s1_ref, *, tb):
    i = pl.program_id(1)
    q = q_ref[...]
    q_pos = i * tb + lax.broadcasted_iota(jnp.int32, (1, tb), 1)
    col_end = (q_pos // CHUNK + 1) * CHUNK

    m_ref[...] = jnp.full(m_ref.shape, MASK_VALUE, F32)
    l_ref[...] = jnp.zeros(l_ref.shape, F32)
    acc_ref[...] = jnp.zeros(acc_ref.shape, F32)

    def produce(s_ref, j):
        ks = pl.ds(pl.multiple_of(j * tb, tb), tb)
        k = jnp.concatenate([k_ref[ks, :], kr_ref[ks, :]], axis=1)
        s_ref[...] = lax.dot_general(k, q, NT_DIMS, preferred_element_type=F32)

    def consume(s_ref, j, masked):
        s = s_ref[...]
        if masked:
            k_pos = j * tb + lax.broadcasted_iota(jnp.int32, (tb, 1), 0)
            s = jnp.where(k_pos < col_end, s, MASK_VALUE)
        m_old = m_ref[...]
        m_new = jnp.maximum(m_old, jnp.max(s, axis=0, keepdims=True))
        p = jnp.exp2(s - m_new)
        alpha = jnp.exp2(m_old - m_new)
        m_ref[...] = m_new
        l_ref[...] = alpha * l_ref[...] + jnp.sum(p, axis=0, keepdims=True)
        acc_ref[...] = alpha * acc_ref[...] + jnp.dot(vt_ref[j], p.astype(BF16), preferred_element_type=F32)

    produce(s0_ref, 0)

    bufs = (s0_ref, s1_ref)
    unroll = 8

    def group(g, carry):
        j0 = unroll * g
        for t in range(unroll):
            produce(bufs[(t + 1) % 2], j0 + t + 1)
            consume(bufs[t % 2], j0 + t, False)
        return carry

    lax.fori_loop(0, i // unroll, group, 0)
    j0 = i // unroll * unroll

    for rem in range(unroll):
        @pl.when(i - j0 == rem)
        def _(rem=rem):
            for t in range(rem + 1):
                if t < rem:
                    produce(bufs[(t + 1) % 2], j0 + t + 1)
                consume(bufs[t % 2], j0 + t, t == rem)

    o_ref[...] = (acc_ref[...] / l_ref[...]).T.astype(o_ref.dtype)


def _attention_t(q, k, kr, vt, seq, tk, out_rows):
    tq = tk
    assert tk % CHUNK == 0 and seq % tk == 0
    nq = seq // tq
    n_tail = _tail_blocks(None, seq, out_rows, tq)
    return _fill_call(
        functools.partial(_attn_t_kernel, tb=tk), into=None, tail=(1, nq) if n_tail else None,
        grid=(N_HEADS, nq + n_tail),
        in_specs=[pl.BlockSpec((None, tq, QK_DIM), lambda h, i: (h, jnp.minimum(i, nq - 1), 0)),
                  pl.BlockSpec((None, seq, QK_NOPE), lambda h, i: (h, 0, 0)),
                  pl.BlockSpec((seq, QK_ROPE), lambda h, i: (0, 0)),
                  pl.BlockSpec((None, seq // tk, V_DIM, tk), lambda h, i: (h, 0, 0, 0))],
        args=(q, k, kr, vt),
        out_specs=pl.BlockSpec((tq, V_DIM), lambda h, i: (i, h)),
        out_shape=jax.ShapeDtypeStruct((out_rows, N_HEADS * V_DIM), BF16),
        scratch_shapes=[pltpu.VMEM((1, tq), F32), pltpu.VMEM((1, tq), F32), pltpu.VMEM((V_DIM, tq), F32),
                        pltpu.VMEM((tk, tq), F32), pltpu.VMEM((tk, tq), F32)],
        compiler_params=_params("parallel", "arbitrary"),
        name="attention_t",
    )


def _merge_kernel(a_ref, o_ref, wa_ref, wb_ref, ga_ref, gb_ref, m_ref):
    ya = jnp.dot(a_ref[...], wa_ref[...], preferred_element_type=F32)
    yb = jnp.dot(o_ref[...], wb_ref[...], preferred_element_type=F32)
    m = jax.nn.sigmoid(ga_ref[...].astype(F32)) * ya + jax.nn.sigmoid(gb_ref[...].astype(F32)) * yb
    m_ref[...] = m.astype(m_ref.dtype)


def _merge(ya_in, o, wa, wb, z_big):
    t, k = ya_in.shape
    n = wa.shape[1]
    tm = _pick(t, (1280, 1024, 512, 256, 128))
    tn = _pick(n, (512, 256, 128))
    ga_blk = (3 * D_CONV + N_HEADS * (QK_NOPE + 2 * QK_ROPE)) // tn
    gb_blk = ga_blk + D_MODEL // tn
    return pl.pallas_call(
        _merge_kernel,
        grid=(t // tm, n // tn),
        in_specs=[pl.BlockSpec((tm, k), lambda i, j: (i, 0)),
                  pl.BlockSpec((tm, k), lambda i, j: (i, 0)),
                  pl.BlockSpec((k, tn), lambda i, j: (0, j)),
                  pl.BlockSpec((k, tn), lambda i, j: (0, j)),
                  pl.BlockSpec((tm, tn), lambda i, j: (i, ga_blk + j)),
                  pl.BlockSpec((tm, tn), lambda i, j: (i, gb_blk + j))],
        out_specs=pl.BlockSpec((tm, tn), lambda i, j: (i, j)),
        out_shape=jax.ShapeDtypeStruct((t, n), BF16),
        compiler_params=_params("parallel", "arbitrary"),
        name="merge",
    )(ya_in, o, wa, wb, z_big, z_big)


def _extract_top(work_ref, rank_ref, val_ref, idx_ref, *, track_rank):
    nk = PEER_NKEYS
    tm = work_ref.shape[1]
    neg = jnp.full((8, tm), -jnp.inf, F32)

    def round_(p, carry):
        idx_prev, pf = carry
        parts = [neg, neg, neg, neg]
        for k in range(nk):
            rows = slice(8 * k, 8 * k + 8)
            hit = idx_prev == float(k)
            w = jnp.where(hit, -jnp.inf, work_ref[rows, :])
            work_ref[rows, :] = w
            if track_rank:
                rank_ref[rows, :] = jnp.where(hit, pf - 1.0, rank_ref[rows, :])
            parts[k % 4] = jnp.maximum(parts[k % 4], w)
        m = jnp.maximum(jnp.maximum(parts[0], parts[1]), jnp.maximum(parts[2], parts[3]))
        big = jnp.full((8, tm), float(nk), F32)
        iparts = [big, big, big, big]
        for k in range(nk):
            w = work_ref[8 * k:8 * k + 8, :]
            iparts[k % 4] = jnp.minimum(iparts[k % 4], jnp.where(w == m, float(k), float(nk)))
        idx = jnp.minimum(jnp.minimum(iparts[0], iparts[1]), jnp.minimum(iparts[2], iparts[3]))
        val_ref[p] = m
        idx_ref[p] = idx
        return idx, pf + 1.0

    idx_last, _ = lax.fori_loop(0, PEER_TOPK, round_,
                                (jnp.full((8, tm), -1.0, F32), jnp.zeros((8, tm), F32)))
    if track_rank:
        for k in range(nk):
            rows = slice(8 * k, 8 * k + 8)
            rank_ref[rows, :] = jnp.where(idx_last == float(k), float(PEER_TOPK - 1), rank_ref[rows, :])


def _peer_select_kernel(qp_ref, k1_ref, k2_ref, k2h_ref, perm_ref,
                        r2_ref, e2_ref, n1_ref, w1_ref,
                        work_ref, rank_ref, a_ref, ia_ref, b_ref, ib_ref):
    nk = PEER_NKEYS
    qp = qp_ref[...]
    tm = qp.shape[0]
    s2h = lax.dot_general(k2h_ref[...], qp, NT_DIMS, preferred_element_type=F32)
    for h in range(PEER_HEADS):
        blk = s2h[h * nk:(h + 1) * nk, :]
        e2_ref[h * nk:(h + 1) * nk, :] = jnp.exp(blk - jnp.max(blk, axis=0, keepdims=True))

    work_ref[...] = lax.dot_general(k1_ref[...], qp, NT_DIMS, preferred_element_type=F32)
    _extract_top(work_ref, rank_ref, a_ref, ia_ref, track_rank=False)
    work_ref[...] = lax.dot_general(k2_ref[...], qp, NT_DIMS, preferred_element_type=F32)
    rank_ref[...] = jnp.full(rank_ref.shape, float(PEER_TOPK), F32)
    _extract_top(work_ref, rank_ref, b_ref, ib_ref, track_rank=True)
    r2_ref[...] = jnp.dot(perm_ref[...], rank_ref[...].astype(BF16), preferred_element_type=F32)

    a = [a_ref[p] for p in range(PEER_TOPK)]
    b = [b_ref[q] for q in range(PEER_TOPK)]
    cand = [a[p] + b[q] for (p, q) in _PAIRS]
    npair = len(_PAIRS)
    beaten = [jnp.zeros((8, tm), F32) for _ in range(npair)]
    for x in range(npair):
        px, qx = _PAIRS[x]
        for y in range(x + 1, npair):
            py, qy = _PAIRS[y]
            if px <= py and qx <= qy:
                beaten[y] = beaten[y] + 1.0
            else:
                gt = jnp.where(cand[y] > cand[x], 1.0, 0.0)
                beaten[x] = beaten[x] + gt
                beaten[y] = beaten[y] + (1.0 - gt)
    sel = [jnp.where(bt < float(PEER_TOPK), 1.0, 0.0) for bt in beaten]
    ea = [jnp.exp(a[p] - a[0]) for p in range(PEER_TOPK)]
    eb = [jnp.exp(b[q] - b[0]) for q in range(PEER_TOPK)]
    z = jnp.zeros((8, tm), F32)
    cnt = [jnp.zeros((8, tm), F32) for _ in range(PEER_TOPK)]
    for x, (p, q) in enumerate(_PAIRS):
        z = z + sel[x] * (ea[p] * eb[q])
        cnt[p] = cnt[p] + sel[x]
    half_inv_z = 0.5 / z
    w1 = [ea[p] * half_inv_z for p in range(PEER_TOPK)]
    ia = [ia_ref[p] for p in range(PEER_TOPK)]
    zero = jnp.zeros((8, tm), F32)
    for k in range(nk):
        n1k = zero
        w1k = zero
        for p in range(PEER_TOPK):
            hit = ia[p] == float(k)
            n1k = jnp.where(hit, cnt[p], n1k)
            w1k = jnp.where(hit, w1[p], w1k)
        n1_ref[8 * k:8 * k + 8, :] = n1k
        w1_ref[8 * k:8 * k + 8, :] = w1k


def _peer_select(qp, k1, k2, k2h, perm):
    t, d = qp.shape
    tm = _pick(t, (256, 128))
    rows = PEER_NKEYS * PEER_HEADS
    full = lambda i: (0, 0)
    out = jax.ShapeDtypeStruct((rows, t), F32)
    ospec = pl.BlockSpec((rows, tm), lambda i: (0, i))
    return pl.pallas_call(
        _peer_select_kernel,
        grid=(t // tm,),
        in_specs=[pl.BlockSpec((tm, d), lambda i: (i, 0)),
                  pl.BlockSpec((rows, d), full),
                  pl.BlockSpec((rows, d), full),
                  pl.BlockSpec((rows, d), full),
                  pl.BlockSpec((rows, rows), full)],
        out_specs=[ospec, ospec, ospec, ospec],
        out_shape=[out, out, out, out],
        scratch_shapes=[pltpu.VMEM((rows, tm), F32), pltpu.VMEM((rows, tm), F32),
                        pltpu.VMEM((PEER_TOPK, 8, tm), F32), pltpu.VMEM((PEER_TOPK, 8, tm), F32),
                        pltpu.VMEM((PEER_TOPK, 8, tm), F32), pltpu.VMEM((PEER_TOPK, 8, tm), F32)],
        compiler_params=_params("parallel"),
        name="peer_select",
    )(qp, k1, k2, k2h, perm)


def _gelu_tanh_x2(x):
    c = math.sqrt(2.0 / math.pi)
    return x + x * jnp.tanh(x * (c + (0.044715 * c) * (x * x)))


def _peer_main_kernel(hn_ref, u_ref, va_ref, vb_ref, r2_ref, e2_ref, n1_ref, w1_ref,
                      o_ref, ga_ref, gb_ref, pa_ref, pb_ref, act_ref, *, tc):
    nk = PEER_NKEYS
    te, tm = ga_ref.shape
    j = pl.program_id(1)
    last = pl.num_programs(1) - 1

    hn = hn_ref[...]

    nw = 256
    d = o_ref.shape[1]

    n_il = te // nk
    jr = 64

    def gate_tile(g_ref, nref, wref, half, c, jh):
        cols = slice(c * tc, (c + 1) * tc)
        accs = [jnp.zeros((jr, tc), F32) for _ in range(n_il)]
        for h in range(PEER_HEADS):
            hr = slice(h * nk + jh * jr, h * nk + (jh + 1) * jr)
            r2 = r2_ref[hr, cols]
            e2 = e2_ref[hr, cols]
            for il in range(n_il):
                r = (half * n_il + il) * PEER_HEADS + h
                keep = r2 < nref[r:r + 1, cols]
                accs[il] = accs[il] + jnp.where(keep, e2, 0.0) * wref[r:r + 1, cols]
        for il in range(n_il):
            g_ref[il * nk + jh * jr:il * nk + (jh + 1) * jr, cols] = accs[il]

    def gate(g_ref, nref, wref, half):
        for c in range(tm // tc):
            for jh in range(nk // jr):
                gate_tile(g_ref, nref, wref, half, c, jh)

    @pl.when(j == 0)
    def _():
        o_ref[...] = jnp.zeros(o_ref.shape, o_ref.dtype)
        pb_ref[...] = jnp.zeros(pb_ref.shape, pb_ref.dtype)

    def act_piece(half, n):
        u = u_ref[half * te + n * nw:half * te + (n + 1) * nw, :]
        return _gelu_tanh_x2(lax.dot_general(hn, u, NT_DIMS, preferred_element_type=F32))

    def out_piece(p_ref, v_ref, n):
        cols = slice(n * nw, (n + 1) * nw)
        o_ref[:, cols] += jnp.dot(p_ref[...], v_ref[:, cols], preferred_element_type=F32)

    def sub_block(half, g_ref, p_new_ref, p_old_ref, v_old_ref, next_gate):
        for n in range(te // nw):
            act_ref[:, n * nw:(n + 1) * nw] = act_piece(half, n)
        for n in range(d // nw):
            out_piece(p_old_ref, v_old_ref, n)
        gate(*next_gate)
        for il in range(n_il):
            ecols = slice(il * nk, (il + 1) * nk)
            for c in range(tm // tc):
                rows = slice(c * tc, (c + 1) * tc)
                g = g_ref[ecols, rows].T
                p_new_ref[rows, ecols] = (g * act_ref[rows, ecols]).astype(p_new_ref.dtype)

    @pl.when(j < last)
    def _():
        sub_block(0, ga_ref, pa_ref, pb_ref, vb_ref, (ga_ref, n1_ref, w1_ref, 0))
        sub_block(1, gb_ref, pb_ref, pa_ref, va_ref, (gb_ref, n1_ref, w1_ref, 1))

    @pl.when(j == last)
    def _():
        for n in range(d // nw):
            out_piece(pb_ref, vb_ref, n)


def _peer_main(hn, u, v, r2, e2, n1, w1):
    t, d = hn.shape
    e = u.shape[0]
    tm = _pick(t, (640, 512, 256, 128))
    te = 512
    tc = 128
    rows = PEER_NKEYS * PEER_HEADS
    sub = 2 * te // PEER_NKEYS * PEER_HEADS
    nj = e // (2 * te)
    cur = lambda j: jnp.minimum(j, nj - 1)
    return pl.pallas_call(
        functools.partial(_peer_main_kernel, tc=tc),
        grid=(t // tm, nj + 1),
        in_specs=[pl.BlockSpec((tm, d), lambda i, j: (i, 0)),
                  pl.BlockSpec((2 * te, d), lambda i, j: (cur(j), 0)),
                  pl.BlockSpec((te, d), lambda i, j: (2 * cur(j), 0)),
                  pl.BlockSpec((te, d), lambda i, j: (jnp.maximum(2 * j - 1, 0), 0)),
                  pl.BlockSpec((rows, tm), lambda i, j: (0, i)),
                  pl.BlockSpec((rows, tm), lambda i, j: (0, i)),
                  pl.BlockSpec((sub, tm), lambda i, j: (cur(j), i)),
                  pl.BlockSpec((sub, tm), lambda i, j: (cur(j), i))],
        out_specs=pl.BlockSpec((tm, d), lambda i, j: (i, 0)),
        out_shape=jax.ShapeDtypeStruct((t, d), F32),
        scratch_shapes=[pltpu.VMEM((te, tm), F32), pltpu.VMEM((te, tm), F32),
                        pltpu.VMEM((tm, te), BF16), pltpu.VMEM((tm, te), BF16),
                        pltpu.VMEM((tm, te), F32)],
        compiler_params=_params("parallel", "arbitrary"),
        name="peer_main",
    )(hn, u, v, v, r2, e2, n1, w1)


def _ple_final_kernel(h_ref, f_ref, p_ref, wg_ref, wp_ref, gp_ref, gf_ref, o_ref):
    h = h_ref[...] + f_ref[...]
    gate = jnp.dot(_rms(h, gp_ref[...]).astype(BF16), wg_ref[...], preferred_element_type=F32)
    pe = jnp.dot(p_ref[...].astype(BF16), wp_ref[...], preferred_element_type=F32)
    o_ref[...] = _rms(h + pe * jax.nn.sigmoid(gate), gf_ref[...])


def _ple_final(h, f, p, wg, wp, g_ple, g_final, row0):
    d = h.shape[1]
    rows, pd = p.shape
    tm = _pick(math.gcd(rows, row0), (256, 128))
    b0 = row0 // tm
    whole = lambda i: (0, 0)
    row = pl.BlockSpec((tm, d), lambda i: (b0 + i, 0))
    return pl.pallas_call(
        _ple_final_kernel,
        grid=(rows // tm,),
        in_specs=[row, row,
                  pl.BlockSpec((tm, pd), lambda i: (i, 0)),
                  pl.BlockSpec((d, d), whole),
                  pl.BlockSpec((pd, d), whole),
                  pl.BlockSpec((1, d), whole),
                  pl.BlockSpec((1, d), whole)],
        out_specs=pl.BlockSpec((tm, d), lambda i: (i, 0)),
        out_shape=jax.ShapeDtypeStruct((rows, d), F32),
        compiler_params=_params("parallel"),
        name="ple_final",
    )(h, f, p, wg, wp, g_ple.reshape(1, d), g_final.reshape(1, d))


def _swap_halves(w):
    half = w.shape[-1] // 2
    return jnp.concatenate([w[..., half:], w[..., :half]], axis=-1)


def _split_w_in(w_in):
    d = w_in.shape[0]
    splits = [int(s) for s in np.cumsum(IN_SIZES)[:-1]]
    wb, wc, wx, wq, wckv, wkr, wga, wgb = jnp.split(w_in, splits, axis=1)
    wq = wq.reshape(d, N_HEADS, QK_DIM)
    wq_n = wq[:, :, :QK_NOPE].reshape(d, N_HEADS * QK_NOPE)
    wq_r = wq[:, :, QK_NOPE:]
    w_big = jnp.concatenate(
        [wb, wc, wx, wq_n, wq_r.reshape(d, -1), _swap_halves(wq_r).reshape(d, -1), wga, wgb], axis=1)
    w_small = jnp.concatenate([wckv, wkr, _swap_halves(wkr)], axis=1)
    return w_big, w_small


def _rope_tables(pos):
    inv = 1.0 / (ROPE_THETA ** (jnp.arange(0, QK_ROPE, 2, dtype=F32) / QK_ROPE))
    ang = pos.astype(F32)[:, None] * inv[None, :]
    cos, sin = jnp.cos(ang), jnp.sin(ang)
    return jnp.concatenate([cos, cos], axis=1), jnp.concatenate([-sin, sin], axis=1)


def _key_matrices(sub_keys):
    nk, hd, half = PEER_NKEYS, PEER_HEADS, PEER_DKEY // 2
    eye = jnp.eye(hd, dtype=F32)

    def build(c, head_major):
        sel = jnp.zeros((2,), F32).at[c].set(1.0)
        m = sub_keys[c][:, None, None, None, :] * eye[None, :, :, None, None] * sel[None, None, None, :, None]
        if head_major:
            m = jnp.transpose(m, (1, 0, 2, 3, 4))
        return m.reshape(nk * hd, hd * PEER_DKEY).astype(BF16)

    r = np.arange(nk * hd)
    perm = np.zeros((nk * hd, nk * hd), np.float32)
    perm[(r % hd) * nk + r // hd, r] = 1.0
    return build(0, False), build(1, False), build(1, True), jnp.asarray(perm, BF16)


def kernel(x_prompt, x_sample, cache_conv, cache_ckv, cache_krope, p_prompt, p_sample, g_mix, w_in, conv_w, g_kv, w_kv_b, w_a_out, w_b_out, w_o, g_ffn, w_pq, sub_keys, u_tab, v_tab, g_ple, w_ple_gate, w_ple, g_final):
    assert x_prompt.shape[0] == 1 and w_in.shape[0] == 1
    seq = x_prompt.shape[1]
    nb, dseq = x_sample.shape[0], x_sample.shape[1]
    past = cache_ckv.shape[2]
    d = D_MODEL
    ns = nb * dseq
    t = seq + ns

    x_p, x_s = x_prompt.reshape(seq, d), x_sample.reshape(ns, d)
    p_p, p_s = p_prompt[0].reshape(seq, -1), p_sample[0].reshape(ns, -1)
    pos = jnp.concatenate([jnp.arange(seq), jnp.tile(past + jnp.arange(dseq), nb)])
    cos64, sin64 = _rope_tables(pos)
    w_big, w_small = _split_w_in(w_in[0].astype(BF16))

    xn = _rmsnorm(x_p, g_mix[0], BF16, 0, t)
    xn = _rmsnorm(x_s, g_mix[0], BF16, seq, t, into=xn)
    z_big = _matmul(xn, w_big, BF16, name="in_proj")
    z_small = _matmul(xn, w_small, F32, tn_prefs=(w_small.shape[1],), name="in_proj_small")
    c_p, kr_p = _post_small(z_small, g_kv[0], cos64, sin64, 0, seq)
    c_s, kr_s = _post_small(z_small, g_kv[0], cos64, sin64, seq, ns)

    zero_state = jnp.zeros((1, 8, D_CONV), F32)
    ya_in, last_p = _conv_gate(z_big, zero_state, conv_w[0], 0, 1, seq)
    state_s = jnp.pad(cache_conv[0], ((0, 0), (6, 0), (0, 0)))
    ya_in, last_s = _conv_gate(z_big, state_s, conv_w[0], seq, nb, dseq, into=ya_in)
    conv_p = last_p[-1, 6:8, :].reshape(1, 1, 2, D_CONV)
    if dseq >= 2:
        conv_s = last_s[:, 6:8, :].reshape(1, nb, 2, D_CONV)
    else:
        raise NotImplementedError("sample blocks shorter than the convolution state")

    w_kv_h = w_kv_b[0].astype(BF16).reshape(KV_RANK, N_HEADS, QK_NOPE + V_DIM)
    wk_h = jnp.transpose(w_kv_h[:, :, :QK_NOPE], (1, 0, 2))
    wv_h = jnp.transpose(w_kv_h[:, :, QK_NOPE:], (1, 0, 2))
    wk = w_kv_h[:, :, :QK_NOPE].reshape(KV_RANK, N_HEADS * QK_NOPE)
    wvt = w_kv_h[:, :, QK_NOPE:].reshape(KV_RANK, N_HEADS * V_DIM).T
    q_p = _q_prep(z_big, cos64, sin64, 0, seq)
    q_s = _q_prep(z_big, cos64, sin64, seq, ns)
    tk = _pick(seq, (512, 256, 128))
    k_p, vt_p = _kv_proj_t(c_p, wk, wvt, tk)
    o_all = _attention_t(q_p, k_p, kr_p.astype(BF16), vt_p, seq, tk, t)
    o_all = _attention_latent(q_s, wk_h, wv_h, cache_ckv[0], cache_krope[0],
                              c_s.reshape(nb, dseq, KV_RANK), kr_s.reshape(nb, dseq, QK_ROPE), seq, o_all)

    m = _merge(ya_in, o_all, w_a_out[0].astype(BF16), w_b_out[0].astype(BF16), z_big)
    w_out = w_o[0].astype(BF16)
    h1, hn = _out_proj_norm(m, w_out, x_p, g_ffn[0], 0)
    h1, hn = _out_proj_norm(m, w_out, x_s, g_ffn[0], seq, into=(h1, hn))

    qp = _matmul(hn, w_pq[0].astype(BF16), BF16, name="peer_query")
    k1, k2, k2h, perm = _key_matrices(sub_keys[0])
    r2, e2, n1, w1 = _peer_select(qp, k1, k2, k2h, perm)
    peer = _peer_main(hn, u_tab[0].astype(BF16), v_tab[0].astype(BF16), r2, e2, n1, w1)

    wg, wp = w_ple_gate[0].astype(BF16), w_ple[0].astype(BF16)
    y_prompt = _ple_final(h1, peer, p_p, wg, wp, g_ple[0], g_final, 0).reshape(1, seq, d)
    y_sample = _ple_final(h1, peer, p_s, wg, wp, g_ple[0], g_final, seq).reshape(nb, dseq, d)

    ckv_p = c_p.reshape(1, 1, seq, KV_RANK)
    kr_p = kr_p.reshape(1, 1, seq, QK_ROPE)
    ckv_s = c_s.reshape(1, nb, dseq, KV_RANK)
    kr_s = kr_s.reshape(1, nb, dseq, QK_ROPE)
    return (y_prompt, y_sample, conv_p, ckv_p, kr_p, conv_s, ckv_s, kr_s)
```

```python
import functools
import math

import numpy as np
import jax
import jax.numpy as jnp
from jax import lax
from jax.experimental import pallas as pl
from jax.experimental.pallas import tpu as pltpu

D_MODEL = 2048
D_CONV = 2048
N_HEADS = 16
QK_NOPE = 128
QK_ROPE = 64
QK_DIM = QK_NOPE + QK_ROPE
V_DIM = 128
KV_RANK = 512
CHUNK = 64
ROPE_THETA = 10000.0
PEER_HEADS = 8
PEER_NKEYS = 128
PEER_TOPK = 16
PEER_DKEY = 256
PEER_EXPERTS = PEER_NKEYS * PEER_NKEYS
RMS_EPS = 1e-6
IN_SIZES = (D_CONV, D_CONV, D_CONV, N_HEADS * QK_DIM, KV_RANK, QK_ROPE, D_MODEL, D_MODEL)

BF16 = jnp.bfloat16
F32 = jnp.float32
VMEM_LIMIT_BYTES = 56 * 1024 * 1024
MASK_VALUE = -1e30
LANES = 128
NT_DIMS = (((1,), (1,)), ((), ()))
TN_DIMS = (((0,), (0,)), ((), ()))

_PAIRS = tuple((p, q) for p in range(PEER_TOPK) for q in range(PEER_TOPK)
               if (p + 1) * (q + 1) <= PEER_TOPK)


def _pick(n, prefs):
    for p in prefs:
        if n % p == 0:
            return p
    return n


def _params(*sem):
    return pltpu.CompilerParams(dimension_semantics=sem, vmem_limit_bytes=VMEM_LIMIT_BYTES)


def _rms(x, g):
    ms = jnp.mean(x * x, axis=-1, keepdims=True)
    return x * lax.rsqrt(ms + RMS_EPS) * g


def _rms_kernel(x_ref, g_ref, o_ref):
    o_ref[...] = _rms(x_ref[...], g_ref[...]).astype(o_ref.dtype)


def _fill_call(kern, *, into, in_specs, args, tail=None, n_shared=1, **kwargs):
    if tail is not None:
        axis, n_real = tail
        n_in = len(in_specs)
        body = kern

        def kern(*refs):
            i = pl.program_id(axis)

            @pl.when(i < n_real)
            def _():
                body(*refs)

            @pl.when(i >= n_real)
            def _():
                for o in refs[n_in:n_in + n_shared]:
                    o[...] = jnp.zeros(o.shape, o.dtype)

    if into is None:
        return pl.pallas_call(kern, in_specs=in_specs, **kwargs)(*args)
    bufs = tuple(into) if isinstance(into, (tuple, list)) else (into,)
    assert len(bufs) == n_shared

    def filling(*refs):
        kern(*refs[n_shared:])

    return pl.pallas_call(filling, in_specs=[pl.BlockSpec(memory_space=pl.ANY)] * n_shared + list(in_specs),
                          input_output_aliases={k: k for k in range(n_shared)}, **kwargs)(*bufs, *args)


def _tail_blocks(into, end_row, total_rows, tm):
    if into is not None or end_row >= total_rows:
        return 0
    return -(-(total_rows - end_row) // tm)


def _rmsnorm(x, g, out_dtype, out_row0, out_rows, into=None):
    rows, d = x.shape
    tm = _pick(math.gcd(rows, out_row0), (1024, 640, 512, 256, 128))
    bo = out_row0 // tm
    nb = rows // tm
    n_tail = _tail_blocks(into, out_row0 + rows, out_rows, tm)
    return _fill_call(
        _rms_kernel, into=into, tail=(0, nb) if n_tail else None,
        grid=(nb + n_tail,),
        in_specs=[pl.BlockSpec((tm, d), lambda i: (jnp.minimum(i, nb - 1), 0)),
                  pl.BlockSpec((1, d), lambda i: (0, 0))],
        args=(x, g.reshape(1, d)),
        out_specs=pl.BlockSpec((tm, d), lambda i: (bo + i, 0)),
        out_shape=jax.ShapeDtypeStruct((out_rows, d), out_dtype),
        compiler_params=_params("parallel"),
        name="rmsnorm",
    )


def _mm_kernel(x_ref, w_ref, o_ref):
    o_ref[...] = jnp.dot(x_ref[...], w_ref[...], preferred_element_type=F32).astype(o_ref.dtype)


def _matmul(x, w, out_dtype, tn_prefs=(1024, 512, 256, 128), name="matmul"):
    t, k = x.shape
    n = w.shape[1]
    tm = _pick(t, (1280, 1024, 768, 512, 256, 128))
    tn = _pick(n, tn_prefs)
    return pl.pallas_call(
        _mm_kernel,
        grid=(t // tm, n // tn),
        in_specs=[pl.BlockSpec((tm, k), lambda i, j: (i, 0)),
                  pl.BlockSpec((k, tn), lambda i, j: (0, j))],
        out_specs=pl.BlockSpec((tm, tn), lambda i, j: (i, j)),
        out_shape=jax.ShapeDtypeStruct((t, n), out_dtype),
        compiler_params=_params("parallel", "arbitrary"),
        name=name,
    )(x, w)


def _out_proj_norm_kernel(m_ref, w_ref, x_ref, g_ref, h_ref, hn_ref):
    h = x_ref[...] + jnp.dot(m_ref[...], w_ref[...], preferred_element_type=F32)
    h_ref[...] = h
    hn_ref[...] = _rms(h, g_ref[...]).astype(hn_ref.dtype)


def _out_proj_norm(m, w, x_seg, g, row0, into=None):
    t, k = m.shape
    d = w.shape[1]
    rows = x_seg.shape[0]
    tm = _pick(math.gcd(rows, row0), (512, 256, 128))
    b0, nb = row0 // tm, rows // tm
    n_tail = _tail_blocks(into, row0 + rows, t, tm)
    real = lambda i: jnp.minimum(i, nb - 1)
    whole = lambda i: (0, 0)
    row = pl.BlockSpec((tm, d), lambda i: (b0 + i, 0))
    return _fill_call(
        _out_proj_norm_kernel, into=into, tail=(0, nb) if n_tail else None, n_shared=2,
        grid=(nb + n_tail,),
        in_specs=[pl.BlockSpec((tm, k), lambda i: (b0 + real(i), 0)),
                  pl.BlockSpec((k, d), whole),
                  pl.BlockSpec((tm, d), lambda i: (real(i), 0)),
                  pl.BlockSpec((1, d), whole)],
        args=(m, w, x_seg, g.reshape(1, d)),
        out_specs=[row, row],
        out_shape=[jax.ShapeDtypeStruct((t, d), F32), jax.ShapeDtypeStruct((t, d), BF16)],
        compiler_params=_params("parallel"),
        name="out_proj",
    )


def _kv_proj_t_kernel(c_ref, wk_ref, wvt_ref, k_ref, vt_ref):
    c = c_ref[...].astype(BF16)
    k = jnp.dot(c, wk_ref[...], preferred_element_type=F32)
    vt = lax.dot_general(wvt_ref[...], c, NT_DIMS, preferred_element_type=F32)
    for h in range(N_HEADS):
        k_ref[h] = k[:, h * QK_NOPE:(h + 1) * QK_NOPE].astype(k_ref.dtype)
        vt_ref[h] = vt[h * V_DIM:(h + 1) * V_DIM, :].astype(vt_ref.dtype)


def _kv_proj_t(c, wk, wvt, tk):
    r, kd = c.shape
    return pl.pallas_call(
        _kv_proj_t_kernel,
        grid=(r // tk,),
        in_specs=[pl.BlockSpec((tk, kd), lambda i: (i, 0)),
                  pl.BlockSpec((kd, N_HEADS * QK_NOPE), lambda i: (0, 0)),
                  pl.BlockSpec((N_HEADS * V_DIM, kd), lambda i: (0, 0))],
        out_specs=[pl.BlockSpec((N_HEADS, tk, QK_NOPE), lambda i: (0, i, 0)),
                   pl.BlockSpec((N_HEADS, None, V_DIM, tk), lambda i: (0, i, 0, 0))],
        out_shape=[jax.ShapeDtypeStruct((N_HEADS, r, QK_NOPE), BF16),
                   jax.ShapeDtypeStruct((N_HEADS, r // tk, V_DIM, tk), BF16)],
        compiler_params=_params("parallel"),
        name="kv_proj_t",
    )(c, wk, wvt)


def _post_small_kernel(z_ref, g_ref, cos_ref, sin_ref, c_ref, kr_ref):
    z = z_ref[...]
    c_ref[...] = _rms(z[:, :KV_RANK], g_ref[...])
    kr = z[:, KV_RANK:KV_RANK + QK_ROPE]
    kr_sw = z[:, KV_RANK + QK_ROPE:KV_RANK + 2 * QK_ROPE]
    kr_ref[...] = kr * cos_ref[...] + kr_sw * sin_ref[...]


def _post_small(z_small, g_kv, cos64, sin64, row0, rows):
    w = z_small.shape[1]
    tm = _pick(math.gcd(rows, row0), (1024, 512, 256, 128))
    b0 = row0 // tm
    return pl.pallas_call(
        _post_small_kernel,
        grid=(rows // tm,),
        in_specs=[pl.BlockSpec((tm, w), lambda i: (b0 + i, 0)),
                  pl.BlockSpec((1, KV_RANK), lambda i: (0, 0)),
                  pl.BlockSpec((tm, QK_ROPE), lambda i: (b0 + i, 0)),
                  pl.BlockSpec((tm, QK_ROPE), lambda i: (b0 + i, 0))],
        out_specs=[pl.BlockSpec((tm, KV_RANK), lambda i: (i, 0)),
                   pl.BlockSpec((tm, QK_ROPE), lambda i: (i, 0))],
        out_shape=[jax.ShapeDtypeStruct((rows, KV_RANK), F32),
                   jax.ShapeDtypeStruct((rows, QK_ROPE), F32)],
        compiler_params=_params("parallel"),
        name="latent_post",
    )(z_small, g_kv.reshape(1, KV_RANK), cos64, sin64)


def _q_prep_kernel(qn_ref, qr_ref, qsw_ref, cos_ref, sin_ref, o_ref):
    scale = QK_DIM ** -0.5 * math.log2(math.e)
    cos = jnp.tile(cos_ref[...], (1, N_HEADS))
    sin = jnp.tile(sin_ref[...], (1, N_HEADS))
    qr = (qr_ref[...].astype(F32) * cos + qsw_ref[...].astype(F32) * sin) * scale
    qn = qn_ref[...].astype(F32) * scale
    for h in range(N_HEADS):
        o_ref[h, :, 0:QK_NOPE] = qn[:, h * QK_NOPE:(h + 1) * QK_NOPE].astype(o_ref.dtype)
        o_ref[h, :, QK_NOPE:QK_DIM] = qr[:, h * QK_ROPE:(h + 1) * QK_ROPE].astype(o_ref.dtype)


def _q_prep(z_big, cos64, sin64, row0, rows):
    tm = _pick(math.gcd(rows, row0) if row0 else rows, (256, 128, 32))
    b0 = row0 // tm
    nope_blk = 3 * D_CONV // (N_HEADS * QK_NOPE)
    rope_blk = (3 * D_CONV + N_HEADS * QK_NOPE) // (N_HEADS * QK_ROPE)
    return pl.pallas_call(
        _q_prep_kernel,
        grid=(rows // tm,),
        in_specs=[pl.BlockSpec((tm, N_HEADS * QK_NOPE), lambda i: (b0 + i, nope_blk)),
                  pl.BlockSpec((tm, N_HEADS * QK_ROPE), lambda i: (b0 + i, rope_blk)),
                  pl.BlockSpec((tm, N_HEADS * QK_ROPE), lambda i: (b0 + i, rope_blk + 1)),
                  pl.BlockSpec((tm, QK_ROPE), lambda i: (b0 + i, 0)),
                  pl.BlockSpec((tm, QK_ROPE), lambda i: (b0 + i, 0))],
        out_specs=pl.BlockSpec((N_HEADS, tm, QK_DIM), lambda i: (0, i, 0)),
        out_shape=jax.ShapeDtypeStruct((N_HEADS, rows, QK_DIM), BF16),
        compiler_params=_params("parallel"),
        name="q_prep",
    )(z_big, z_big, z_big, cos64, sin64)


def _conv_kernel(state_ref, pc_ref, px_ref, b_ref, c_ref, x_ref, w_ref, o_ref, last_ref, *, tm, halo):
    i = pl.program_id(1)
    u = c_ref[...].astype(F32) * x_ref[...].astype(F32)
    prev = pc_ref[...].astype(F32) * px_ref[...].astype(F32)
    st = state_ref[...]
    first = i == 0
    um1 = jnp.where(first, st[7:8, :], prev[halo - 1:halo, :])
    um2 = jnp.where(first, st[6:7, :], prev[halo - 2:halo - 1, :])
    row = lax.broadcasted_iota(jnp.int32, u.shape, 0)
    s1 = jnp.where(row == 0, um1, pltpu.roll(u, 1, 0))
    s2 = jnp.where(row == 0, um2, jnp.where(row == 1, um1, pltpu.roll(u, 2, 0)))
    w = w_ref[...]
    y = w[0:1, :] * s2 + w[1:2, :] * s1 + w[2:3, :] * u
    o_ref[...] = (b_ref[...].astype(F32) * y).astype(o_ref.dtype)
    last_ref[...] = u[tm - 8:tm, :]


def _conv_gate(z_big, state, conv_w, row0, nseq, seq_len, into=None):
    tm = _pick(seq_len, (512, 256, 128, 32))
    halo = 16
    nb = seq_len // tm
    b0 = row0 // tm
    h0 = row0 // halo
    per = tm // halo
    d = D_CONV

    n_tail = _tail_blocks(into, row0 + nseq * seq_len, z_big.shape[0], tm)
    assert n_tail == 0 or nseq == 1
    real = lambda i: jnp.minimum(i, nb - 1)

    def hmap(col):
        return lambda s, i: (jnp.maximum(h0 + (s * nb + real(i)) * per - 1, 0), col)

    def bmap(col, clamp=True):
        return lambda s, i: (b0 + s * nb + (real(i) if clamp else i), col)

    return _fill_call(
        functools.partial(_conv_kernel, tm=tm, halo=halo), into=into, tail=(1, nb) if n_tail else None,
        grid=(nseq, nb + n_tail),
        in_specs=[pl.BlockSpec((None, 8, d), lambda s, i: (s, 0, 0)),
                  pl.BlockSpec((halo, d), hmap(1)),
                  pl.BlockSpec((halo, d), hmap(2)),
                  pl.BlockSpec((tm, d), bmap(0)),
                  pl.BlockSpec((tm, d), bmap(1)),
                  pl.BlockSpec((tm, d), bmap(2)),
                  pl.BlockSpec((8, d), lambda s, i: (0, 0))],
        args=(state, z_big, z_big, z_big, z_big, z_big, jnp.pad(conv_w, ((0, 5), (0, 0)))),
        out_specs=[pl.BlockSpec((tm, d), bmap(0, clamp=False)),
                   pl.BlockSpec((None, 8, d), lambda s, i: (s * nb + real(i), 0, 0))],
        out_shape=[jax.ShapeDtypeStruct((z_big.shape[0], d), BF16),
                   jax.ShapeDtypeStruct((nseq * nb, 8, d), F32)],
        compiler_params=_params("parallel", "arbitrary"),
        name="conv_gate",
    )


def _attn_latent_kernel(q_ref, wk_ref, wv_ref, cp_ref, krp_ref, cn_ref, krn_ref, o_ref, qa_ref, acc_ref,
                        *, past, tk):
    nh, dseq, _ = q_ref.shape
    rows = nh * dseq
    for h in range(nh):
        qh = q_ref[h]
        qa = lax.dot_general(qh[:, :QK_NOPE], wk_ref[h], NT_DIMS, preferred_element_type=F32)
        qa_ref[h * dseq:(h + 1) * dseq, 0:KV_RANK] = qa.astype(qa_ref.dtype)
        qa_ref[h * dseq:(h + 1) * dseq, KV_RANK:KV_RANK + QK_ROPE] = qh[:, QK_NOPE:QK_DIM]
    q = qa_ref[...]
    acc_ref[...] = jnp.zeros(acc_ref.shape, F32)

    def update(carry, s, c):
        m, l = carry
        m_new = jnp.maximum(m, jnp.max(s, axis=1, keepdims=True))
        p = jnp.exp2(s - m_new)
        alpha = jnp.exp2(m - m_new)
        acc_ref[...] = alpha * acc_ref[...] + jnp.dot(p.astype(BF16), c, preferred_element_type=F32)
        return m_new, alpha * l + jnp.sum(p, axis=1, keepdims=True)

    def past_step(j, carry):
        ks = pl.ds(pl.multiple_of(j * tk, tk), tk)
        c = cp_ref[ks, :].astype(BF16)
        k = jnp.concatenate([c, krp_ref[ks, :].astype(BF16)], axis=1)
        return update(carry, lax.dot_general(q, k, NT_DIMS, preferred_element_type=F32), c)

    carry = (jnp.full((rows, 1), MASK_VALUE, F32), jnp.zeros((rows, 1), F32))
    carry = lax.fori_loop(0, past // tk, past_step, carry)

    cn = cn_ref[...].astype(BF16)
    kn = jnp.concatenate([cn, krn_ref[...].astype(BF16)], axis=1)
    s = lax.dot_general(q, kn, NT_DIMS, preferred_element_type=F32)
    q_pos = past + lax.broadcasted_iota(jnp.int32, (rows, 1), 0) % dseq
    k_pos = past + lax.broadcasted_iota(jnp.int32, (1, dseq), 1)
    s = jnp.where(k_pos // CHUNK <= q_pos // CHUNK, s, MASK_VALUE)
    m, l = update(carry, s, cn)

    lat = (acc_ref[...] / l).astype(BF16)
    for h in range(nh):
        o = jnp.dot(lat[h * dseq:(h + 1) * dseq, :], wv_ref[h], preferred_element_type=F32)
        o_ref[:, h * V_DIM:(h + 1) * V_DIM] = o.astype(o_ref.dtype)


def _attention_latent(q, wk, wv, c_past, kr_past, c_new, kr_new, row0, into):
    nb, past, _ = c_past.shape
    dseq = c_new.shape[1]
    tk = _pick(past, (512, 256, 128))
    assert row0 % dseq == 0
    b0 = row0 // dseq
    per_b = lambda w: pl.BlockSpec((None,) + w, lambda b: (b, 0, 0))
    whole = lambda a: pl.BlockSpec(a.shape, lambda b: (0, 0, 0))
    return _fill_call(
        functools.partial(_attn_latent_kernel, past=past, tk=tk), into=into,
        grid=(nb,),
        in_specs=[pl.BlockSpec((N_HEADS, dseq, QK_DIM), lambda b: (0, b, 0)),
                  whole(wk), whole(wv),
                  per_b((past, KV_RANK)), per_b((past, QK_ROPE)),
                  per_b((dseq, KV_RANK)), per_b((dseq, QK_ROPE))],
        args=(q, wk, wv, c_past, kr_past, c_new, kr_new),
        out_specs=pl.BlockSpec((dseq, N_HEADS * V_DIM), lambda b: (b0 + b, 0)),
        out_shape=jax.ShapeDtypeStruct(into.shape, into.dtype),
        scratch_shapes=[pltpu.VMEM((N_HEADS * dseq, KV_RANK + QK_ROPE), BF16),
                        pltpu.VMEM((N_HEADS * dseq, KV_RANK), F32)],
        compiler_params=_params("parallel"),
        name="attention_latent",
    )


def _attn_t_kernel(q_ref, k_ref, kr_ref, vt_ref, o_ref, m_ref, l_ref, acc_ref, s0_ref, s1_ref, *, tb):
    i = pl.program_id(1)
    q = q_ref[...]
    q_pos = i * tb + lax.broadcasted_iota(jnp.int32, (1, tb), 1)
    col_end = (q_pos // CHUNK + 1) * CHUNK

    m_ref[...] = jnp.full(m_ref.shape, MASK_VALUE, F32)
    l_ref[...] = jnp.zeros(l_ref.shape, F32)
    acc_ref[...] = jnp.zeros(acc_ref.shape, F32)

    def produce(s_ref, j):
        ks = pl.ds(pl.multiple_of(j * tb, tb), tb)
        k = jnp.concatenate([k_ref[ks, :], kr_ref[ks, :]], axis=1)
        s_ref[:, :tb] = lax.dot_general(k, q, NT_DIMS, preferred_element_type=F32)

    def consume(s_ref, j, masked):
        s = s_ref[:, :tb]
        if masked:
            k_pos = j * tb + lax.broadcasted_iota(jnp.int32, (tb, 1), 0)
            s = jnp.where(k_pos < col_end, s, MASK_VALUE)
        m_old = m_ref[...]
        m_new = jnp.maximum(m_old, jnp.max(s, axis=0, keepdims=True))
        p = jnp.exp2(s - m_new)
        alpha = jnp.exp2(m_old - m_new)
        m_ref[...] = m_new
        l_ref[...] = alpha * l_ref[...] + jnp.sum(p, axis=0, keepdims=True)
        acc_ref[...] = alpha * acc_ref[...] + jnp.dot(vt_ref[j], p.astype(BF16), preferred_element_type=F32)

    produce(s0_ref, 0)

    bufs = (s0_ref, s1_ref)
    unroll = 8

    def group(g, carry):
        j0 = unroll * g
        for t in range(unroll):
            produce(bufs[(t + 1) % 2], j0 + t + 1)
            consume(bufs[t % 2], j0 + t, False)
        return carry

    lax.fori_loop(0, i // unroll, group, 0)
    j0 = i // unroll * unroll

    for rem in range(unroll):
        @pl.when(i - j0 == rem)
        def _(rem=rem):
            for t in range(rem + 1):
                if t < rem:
                    produce(bufs[(t + 1) % 2], j0 + t + 1)
                consume(bufs[t % 2], j0 + t, t == rem)

    o_ref[...] = (acc_ref[...] / l_ref[...]).T.astype(o_ref.dtype)


def _attention_t(q, k, kr, vt, seq, tk, out_rows):
    tq = tk
    assert tk % CHUNK == 0 and seq % tk == 0
    nq = seq // tq
    n_tail = _tail_blocks(None, seq, out_rows, tq)
    return _fill_call(
        functools.partial(_attn_t_kernel, tb=tk), into=None, tail=(1, nq) if n_tail else None,
        grid=(N_HEADS, nq + n_tail),
        in_specs=[pl.BlockSpec((None, tq, QK_DIM), lambda h, i: (h, jnp.minimum(i, nq - 1), 0)),
                  pl.BlockSpec((None, seq, QK_NOPE), lambda h, i: (h, 0, 0)),
                  pl.BlockSpec((seq, QK_ROPE), lambda h, i: (0, 0)),
                  pl.BlockSpec((None, seq // tk, V_DIM, tk), lambda h, i: (h, 0, 0, 0))],
        args=(q, k, kr, vt),
        out_specs=pl.BlockSpec((tq, V_DIM), lambda h, i: (i, h)),
        out_shape=jax.ShapeDtypeStruct((out_rows, N_HEADS * V_DIM), BF16),
        scratch_shapes=[pltpu.VMEM((1, tq), F32), pltpu.VMEM((1, tq), F32), pltpu.VMEM((V_DIM, tq), F32),
                        pltpu.VMEM((tk, tq + LANES), F32), pltpu.VMEM((tk, tq + LANES), F32)],
        compiler_params=_params("parallel", "arbitrary"),
        name="attention_t",
    )


def _merge_kernel(a_ref, o_ref, wa_ref, wb_ref, ga_ref, gb_ref, m_ref):
    ya = jnp.dot(a_ref[...], wa_ref[...], preferred_element_type=F32)
    yb = jnp.dot(o_ref[...], wb_ref[...], preferred_element_type=F32)
    m = jax.nn.sigmoid(ga_ref[...].astype(F32)) * ya + jax.nn.sigmoid(gb_ref[...].astype(F32)) * yb
    m_ref[...] = m.astype(m_ref.dtype)


def _merge(ya_in, o, wa, wb, z_big):
    t, k = ya_in.shape
    n = wa.shape[1]
    tm = _pick(t, (1280, 1024, 512, 256, 128))
    tn = _pick(n, (512, 256, 128))
    ga_blk = (3 * D_CONV + N_HEADS * (QK_NOPE + 2 * QK_ROPE)) // tn
    gb_blk = ga_blk + D_MODEL // tn
    return pl.pallas_call(
        _merge_kernel,
        grid=(t // tm, n // tn),
        in_specs=[pl.BlockSpec((tm, k), lambda i, j: (i, 0)),
                  pl.BlockSpec((tm, k), lambda i, j: (i, 0)),
                  pl.BlockSpec((k, tn), lambda i, j: (0, j)),
                  pl.BlockSpec((k, tn), lambda i, j: (0, j)),
                  pl.BlockSpec((tm, tn), lambda i, j: (i, ga_blk + j)),
                  pl.BlockSpec((tm, tn), lambda i, j: (i, gb_blk + j))],
        out_specs=pl.BlockSpec((tm, tn), lambda i, j: (i, j)),
        out_shape=jax.ShapeDtypeStruct((t, n), BF16),
        compiler_params=_params("parallel", "arbitrary"),
        name="merge",
    )(ya_in, o, wa, wb, z_big, z_big)


def _extract_top(work_ref, rank_ref, val_ref, idx_ref, *, track_rank):
    nk = PEER_NKEYS
    tm = work_ref.shape[1]
    neg = jnp.full((8, tm), -jnp.inf, F32)

    def round_(p, carry):
        idx_prev, pf = carry
        parts = [neg, neg, neg, neg]
        for k in range(nk):
            rows = slice(8 * k, 8 * k + 8)
            hit = idx_prev == float(k)
            w = jnp.where(hit, -jnp.inf, work_ref[rows, :])
            work_ref[rows, :] = w
            if track_rank:
                rank_ref[rows, :] = jnp.where(hit, pf - 1.0, rank_ref[rows, :])
            parts[k % 4] = jnp.maximum(parts[k % 4], w)
        m = jnp.maximum(jnp.maximum(parts[0], parts[1]), jnp.maximum(parts[2], parts[3]))
        big = jnp.full((8, tm), float(nk), F32)
        iparts = [big, big, big, big]
        for k in range(nk):
            w = work_ref[8 * k:8 * k + 8, :]
            iparts[k % 4] = jnp.minimum(iparts[k % 4], jnp.where(w == m, float(k), float(nk)))
        idx = jnp.minimum(jnp.minimum(iparts[0], iparts[1]), jnp.minimum(iparts[2], iparts[3]))
        val_ref[p] = m
        idx_ref[p] = idx
        return idx, pf + 1.0

    idx_last, _ = lax.fori_loop(0, PEER_TOPK, round_,
                                (jnp.full((8, tm), -1.0, F32), jnp.zeros((8, tm), F32)))
    if track_rank:
        for k in range(nk):
            rows = slice(8 * k, 8 * k + 8)
            rank_ref[rows, :] = jnp.where(idx_last == float(k), float(PEER_TOPK - 1), rank_ref[rows, :])


def _peer_select_kernel(qp_ref, k1_ref, k2_ref, k2h_ref, perm_ref,
                        r2_ref, e2_ref, n1_ref, w1_ref,
                        work_ref, rank_ref, a_ref, ia_ref, b_ref, ib_ref):
    nk = PEER_NKEYS
    qp = qp_ref[...]
    tm = qp.shape[0]
    s2h = lax.dot_general(k2h_ref[...], qp, NT_DIMS, preferred_element_type=F32)
    for h in range(PEER_HEADS):
        blk = s2h[h * nk:(h + 1) * nk, :]
        e2_ref[h * nk:(h + 1) * nk, :] = jnp.exp(blk - jnp.max(blk, axis=0, keepdims=True))

    work_ref[...] = lax.dot_general(k1_ref[...], qp, NT_DIMS, preferred_element_type=F32)
    _extract_top(work_ref, rank_ref, a_ref, ia_ref, track_rank=False)
    work_ref[...] = lax.dot_general(k2_ref[...], qp, NT_DIMS, preferred_element_type=F32)
    rank_ref[...] = jnp.full(rank_ref.shape, float(PEER_TOPK), F32)
    _extract_top(work_ref, rank_ref, b_ref, ib_ref, track_rank=True)
    r2_ref[...] = jnp.dot(perm_ref[...], rank_ref[...].astype(BF16), preferred_element_type=F32)

    a = [a_ref[p] for p in range(PEER_TOPK)]
    b = [b_ref[q] for q in range(PEER_TOPK)]
    cand = [a[p] + b[q] for (p, q) in _PAIRS]
    npair = len(_PAIRS)
    beaten = [jnp.zeros((8, tm), F32) for _ in range(npair)]
    for x in range(npair):
        px, qx = _PAIRS[x]
        for y in range(x + 1, npair):
            py, qy = _PAIRS[y]
            if px <= py and qx <= qy:
                beaten[y] = beaten[y] + 1.0
            else:
                gt = jnp.where(cand[y] > cand[x], 1.0, 0.0)
                beaten[x] = beaten[x] + gt
                beaten[y] = beaten[y] + (1.0 - gt)
    sel = [jnp.where(bt < float(PEER_TOPK), 1.0, 0.0) for bt in beaten]
    ea = [jnp.exp(a[p] - a[0]) for p in range(PEER_TOPK)]
    eb = [jnp.exp(b[q] - b[0]) for q in range(PEER_TOPK)]
    z = jnp.zeros((8, tm), F32)
    cnt = [jnp.zeros((8, tm), F32) for _ in range(PEER_TOPK)]
    for x, (p, q) in enumerate(_PAIRS):
        z = z + sel[x] * (ea[p] * eb[q])
        cnt[p] = cnt[p] + sel[x]
    half_inv_z = 0.5 / z
    w1 = [ea[p] * half_inv_z for p in range(PEER_TOPK)]
    ia = [ia_ref[p] for p in range(PEER_TOPK)]
    zero = jnp.zeros((8, tm), F32)
    for k in range(nk):
        n1k = zero
        w1k = zero
        for p in range(PEER_TOPK):
            hit = ia[p] == float(k)
            n1k = jnp.where(hit, cnt[p], n1k)
            w1k = jnp.where(hit, w1[p], w1k)
        n1_ref[8 * k:8 * k + 8, :] = n1k
        w1_ref[8 * k:8 * k + 8, :] = w1k


def _peer_select(qp, k1, k2, k2h, perm):
    t, d = qp.shape
    tm = _pick(t, (256, 128))
    rows = PEER_NKEYS * PEER_HEADS
    full = lambda i: (0, 0)
    out = jax.ShapeDtypeStruct((rows, t), F32)
    ospec = pl.BlockSpec((rows, tm), lambda i: (0, i))
    return pl.pallas_call(
        _peer_select_kernel,
        grid=(t // tm,),
        in_specs=[pl.BlockSpec((tm, d), lambda i: (i, 0)),
                  pl.BlockSpec((rows, d), full),
                  pl.BlockSpec((rows, d), full),
                  pl.BlockSpec((rows, d), full),
                  pl.BlockSpec((rows, rows), full)],
        out_specs=[ospec, ospec, ospec, ospec],
        out_shape=[out, out, out, out],
        scratch_shapes=[pltpu.VMEM((rows, tm), F32), pltpu.VMEM((rows, tm), F32),
                        pltpu.VMEM((PEER_TOPK, 8, tm), F32), pltpu.VMEM((PEER_TOPK, 8, tm), F32),
                        pltpu.VMEM((PEER_TOPK, 8, tm), F32), pltpu.VMEM((PEER_TOPK, 8, tm), F32)],
        compiler_params=_params("parallel"),
        name="peer_select",
    )(qp, k1, k2, k2h, perm)


def _gelu_tanh_x2(x):
    c = math.sqrt(2.0 / math.pi)
    return x + x * jnp.tanh(x * (c + (0.044715 * c) * (x * x)))


def _peer_main_kernel(hn_ref, u_ref, va_ref, vb_ref, r2_ref, e2_ref, n1_ref, w1_ref,
                      o_ref, ga_ref, gb_ref, pa_ref, pb_ref, act_ref, *, tc):
    nk = PEER_NKEYS
    te, tm = ga_ref.shape
    j = pl.program_id(1)
    last = pl.num_programs(1) - 1

    hn = hn_ref[...]

    nw = 256
    d = o_ref.shape[1]

    n_il = te // nk
    jr = 64

    def gate_tile(g_ref, nref, wref, half, c, jh):
        cols = slice(c * tc, (c + 1) * tc)
        accs = [jnp.zeros((jr, tc), F32) for _ in range(n_il)]
        for h in range(PEER_HEADS):
            hr = slice(h * nk + jh * jr, h * nk + (jh + 1) * jr)
            r2 = r2_ref[hr, cols]
            e2 = e2_ref[hr, cols]
            for il in range(n_il):
                r = (half * n_il + il) * PEER_HEADS + h
                keep = r2 < nref[r:r + 1, cols]
                accs[il] = accs[il] + jnp.where(keep, e2, 0.0) * wref[r:r + 1, cols]
        for il in range(n_il):
            g_ref[il * nk + jh * jr:il * nk + (jh + 1) * jr, cols] = accs[il]

    def gate(g_ref, nref, wref, half):
        for c in range(tm // tc):
            for jh in range(nk // jr):
                gate_tile(g_ref, nref, wref, half, c, jh)

    @pl.when(j == 0)
    def _():
        o_ref[...] = jnp.zeros(o_ref.shape, o_ref.dtype)
        pb_ref[...] = jnp.zeros(pb_ref.shape, pb_ref.dtype)

    def act_piece(half, n):
        u = u_ref[half * te + n * nw:half * te + (n + 1) * nw, :]
        return _gelu_tanh_x2(lax.dot_general(hn, u, NT_DIMS, preferred_element_type=F32))

    def out_piece(p_ref, v_ref, n):
        cols = slice(n * nw, (n + 1) * nw)
        o_ref[:, cols] += jnp.dot(p_ref[...], v_ref[:, cols], preferred_element_type=F32)

    def sub_block(half, g_ref, p_new_ref, p_old_ref, v_old_ref, next_gate):
        for n in range(te // nw):
            act_ref[:, n * nw:(n + 1) * nw] = act_piece(half, n)
        for n in range(d // nw):
            out_piece(p_old_ref, v_old_ref, n)
        gate(*next_gate)
        for il in range(n_il):
            ecols = slice(il * nk, (il + 1) * nk)
            for c in range(tm // tc):
                rows = slice(c * tc, (c + 1) * tc)
                g = g_ref[ecols, rows].T
                p_new_ref[rows, ecols] = (g * act_ref[rows, ecols]).astype(p_new_ref.dtype)

    @pl.when(j < last)
    def _():
        sub_block(0, ga_ref, pa_ref, pb_ref, vb_ref, (ga_ref, n1_ref, w1_ref, 0))
        sub_block(1, gb_ref, pb_ref, pa_ref, va_ref, (gb_ref, n1_ref, w1_ref, 1))

    @pl.when(j == last)
    def _():
        for n in range(d // nw):
            out_piece(pb_ref, vb_ref, n)


def _peer_main(hn, u, v, r2, e2, n1, w1):
    t, d = hn.shape
    e = u.shape[0]
    tm = _pick(t, (640, 512, 256, 128))
    te = 512
    tc = 128
    rows = PEER_NKEYS * PEER_HEADS
    sub = 2 * te // PEER_NKEYS * PEER_HEADS
    nj = e // (2 * te)
    cur = lambda j: jnp.minimum(j, nj - 1)
    return pl.pallas_call(
        functools.partial(_peer_main_kernel, tc=tc),
        grid=(t // tm, nj + 1),
        in_specs=[pl.BlockSpec((tm, d), lambda i, j: (i, 0)),
                  pl.BlockSpec((2 * te, d), lambda i, j: (cur(j), 0)),
                  pl.BlockSpec((te, d), lambda i, j: (2 * cur(j), 0)),
                  pl.BlockSpec((te, d), lambda i, j: (jnp.maximum(2 * j - 1, 0), 0)),
                  pl.BlockSpec((rows, tm), lambda i, j: (0, i)),
                  pl.BlockSpec((rows, tm), lambda i, j: (0, i)),
                  pl.BlockSpec((sub, tm), lambda i, j: (cur(j), i)),
                  pl.BlockSpec((sub, tm), lambda i, j: (cur(j), i))],
        out_specs=pl.BlockSpec((tm, d), lambda i, j: (i, 0)),
        out_shape=jax.ShapeDtypeStruct((t, d), F32),
        scratch_shapes=[pltpu.VMEM((te, tm), F32), pltpu.VMEM((te, tm), F32),
                        pltpu.VMEM((tm, te), BF16), pltpu.VMEM((tm, te), BF16),
                        pltpu.VMEM((tm, te), F32)],
        compiler_params=_params("parallel", "arbitrary"),
        name="peer_main",
    )(hn, u, v, v, r2, e2, n1, w1)


def _ple_final_kernel(h_ref, f_ref, p_ref, wg_ref, wp_ref, gp_ref, gf_ref, o_ref):
    h = h_ref[...] + f_ref[...]
    gate = jnp.dot(_rms(h, gp_ref[...]).astype(BF16), wg_ref[...], preferred_element_type=F32)
    pe = jnp.dot(p_ref[...].astype(BF16), wp_ref[...], preferred_element_type=F32)
    o_ref[...] = _rms(h + pe * jax.nn.sigmoid(gate), gf_ref[...])


def _ple_final(h, f, p, wg, wp, g_ple, g_final, row0):
    d = h.shape[1]
    rows, pd = p.shape
    tm = _pick(math.gcd(rows, row0), (256, 128))
    b0 = row0 // tm
    whole = lambda i: (0, 0)
    row = pl.BlockSpec((tm, d), lambda i: (b0 + i, 0))
    return pl.pallas_call(
        _ple_final_kernel,
        grid=(rows // tm,),
        in_specs=[row, row,
                  pl.BlockSpec((tm, pd), lambda i: (i, 0)),
                  pl.BlockSpec((d, d), whole),
                  pl.BlockSpec((pd, d), whole),
                  pl.BlockSpec((1, d), whole),
                  pl.BlockSpec((1, d), whole)],
        out_specs=pl.BlockSpec((tm, d), lambda i: (i, 0)),
        out_shape=jax.ShapeDtypeStruct((rows, d), F32),
        compiler_params=_params("parallel"),
        name="ple_final",
    )(h, f, p, wg, wp, g_ple.reshape(1, d), g_final.reshape(1, d))


def _swap_halves(w):
    half = w.shape[-1] // 2
    return jnp.concatenate([w[..., half:], w[..., :half]], axis=-1)


def _split_w_in(w_in):
    d = w_in.shape[0]
    splits = [int(s) for s in np.cumsum(IN_SIZES)[:-1]]
    wb, wc, wx, wq, wckv, wkr, wga, wgb = jnp.split(w_in, splits, axis=1)
    wq = wq.reshape(d, N_HEADS, QK_DIM)
    wq_n = wq[:, :, :QK_NOPE].reshape(d, N_HEADS * QK_NOPE)
    wq_r = wq[:, :, QK_NOPE:]
    w_big = jnp.concatenate(
        [wb, wc, wx, wq_n, wq_r.reshape(d, -1), _swap_halves(wq_r).reshape(d, -1), wga, wgb], axis=1)
    w_small = jnp.concatenate([wckv, wkr, _swap_halves(wkr)], axis=1)
    return w_big, w_small


def _rope_tables(pos):
    inv = 1.0 / (ROPE_THETA ** (jnp.arange(0, QK_ROPE, 2, dtype=F32) / QK_ROPE))
    ang = pos.astype(F32)[:, None] * inv[None, :]
    cos, sin = jnp.cos(ang), jnp.sin(ang)
    return jnp.concatenate([cos, cos], axis=1), jnp.concatenate([-sin, sin], axis=1)


def _key_matrices(sub_keys):
    nk, hd, half = PEER_NKEYS, PEER_HEADS, PEER_DKEY // 2
    eye = jnp.eye(hd, dtype=F32)

    def build(c, head_major):
        sel = jnp.zeros((2,), F32).at[c].set(1.0)
        m = sub_keys[c][:, None, None, None, :] * eye[None, :, :, None, None] * sel[None, None, None, :, None]
        if head_major:
            m = jnp.transpose(m, (1, 0, 2, 3, 4))
        return m.reshape(nk * hd, hd * PEER_DKEY).astype(BF16)

    r = np.arange(nk * hd)
    perm = np.zeros((nk * hd, nk * hd), np.float32)
    perm[(r % hd) * nk + r // hd, r] = 1.0
    return build(0, False), build(1, False), build(1, True), jnp.asarray(perm, BF16)


def kernel(x_prompt, x_sample, cache_conv, cache_ckv, cache_krope, p_prompt, p_sample, g_mix, w_in, conv_w, g_kv, w_kv_b, w_a_out, w_b_out, w_o, g_ffn, w_pq, sub_keys, u_tab, v_tab, g_ple, w_ple_gate, w_ple, g_final):
    assert x_prompt.shape[0] == 1 and w_in.shape[0] == 1
    seq = x_prompt.shape[1]
    nb, dseq = x_sample.shape[0], x_sample.shape[1]
    past = cache_ckv.shape[2]
    d = D_MODEL
    ns = nb * dseq
    t = seq + ns

    x_p, x_s = x_prompt.reshape(seq, d), x_sample.reshape(ns, d)
    p_p, p_s = p_prompt[0].reshape(seq, -1), p_sample[0].reshape(ns, -1)
    pos = jnp.concatenate([jnp.arange(seq), jnp.tile(past + jnp.arange(dseq), nb)])
    cos64, sin64 = _rope_tables(pos)
    w_big, w_small = _split_w_in(w_in[0].astype(BF16))

    xn = _rmsnorm(x_p, g_mix[0], BF16, 0, t)
    xn = _rmsnorm(x_s, g_mix[0], BF16, seq, t, into=xn)
    z_big = _matmul(xn, w_big, BF16, name="in_proj")
    z_small = _matmul(xn, w_small, F32, tn_prefs=(w_small.shape[1],), name="in_proj_small")
    c_p, kr_p = _post_small(z_small, g_kv[0], cos64, sin64, 0, seq)
    c_s, kr_s = _post_small(z_small, g_kv[0], cos64, sin64, seq, ns)

    zero_state = jnp.zeros((1, 8, D_CONV), F32)
    ya_in, last_p = _conv_gate(z_big, zero_state, conv_w[0], 0, 1, seq)
    state_s = jnp.pad(cache_conv[0], ((0, 0), (6, 0), (0, 0)))
    ya_in, last_s = _conv_gate(z_big, state_s, conv_w[0], seq, nb, dseq, into=ya_in)
    conv_p = last_p[-1, 6:8, :].reshape(1, 1, 2, D_CONV)
    if dseq >= 2:
        conv_s = last_s[:, 6:8, :].reshape(1, nb, 2, D_CONV)
    else:
        raise NotImplementedError("sample blocks shorter than the convolution state")

    w_kv_h = w_kv_b[0].astype(BF16).reshape(KV_RANK, N_HEADS, QK_NOPE + V_DIM)
    wk_h = jnp.transpose(w_kv_h[:, :, :QK_NOPE], (1, 0, 2))
    wv_h = jnp.transpose(w_kv_h[:, :, QK_NOPE:], (1, 0, 2))
    wk = w_kv_h[:, :, :QK_NOPE].reshape(KV_RANK, N_HEADS * QK_NOPE)
    wvt = w_kv_h[:, :, QK_NOPE:].reshape(KV_RANK, N_HEADS * V_DIM).T
    q_p = _q_prep(z_big, cos64, sin64, 0, seq)
    q_s = _q_prep(z_big, cos64, sin64, seq, ns)
    tk = _pick(seq, (512, 256, 128))
    k_p, vt_p = _kv_proj_t(c_p, wk, wvt, tk)
    o_all = _attention_t(q_p, k_p, kr_p.astype(BF16), vt_p, seq, tk, t)
    o_all = _attention_latent(q_s, wk_h, wv_h, cache_ckv[0], cache_krope[0],
                              c_s.reshape(nb, dseq, KV_RANK), kr_s.reshape(nb, dseq, QK_ROPE), seq, o_all)

    m = _merge(ya_in, o_all, w_a_out[0].astype(BF16), w_b_out[0].astype(BF16), z_big)
    w_out = w_o[0].astype(BF16)
    h1, hn = _out_proj_norm(m, w_out, x_p, g_ffn[0], 0)
    h1, hn = _out_proj_norm(m, w_out, x_s, g_ffn[0], seq, into=(h1, hn))

    qp = _matmul(hn, w_pq[0].astype(BF16), BF16, name="peer_query")
    k1, k2, k2h, perm = _key_matrices(sub_keys[0])
    r2, e2, n1, w1 = _peer_select(qp, k1, k2, k2h, perm)
    peer = _peer_main(hn, u_tab[0].astype(BF16), v_tab[0].astype(BF16), r2, e2, n1, w1)

    wg, wp = w_ple_gate[0].astype(BF16), w_ple[0].astype(BF16)
    y_prompt = _ple_final(h1, peer, p_p, wg, wp, g_ple[0], g_final, 0).reshape(1, seq, d)
    y_sample = _ple_final(h1, peer, p_s, wg, wp, g_ple[0], g_final, seq).reshape(nb, dseq, d)

    ckv_p = c_p.reshape(1, 1, seq, KV_RANK)
    kr_p = kr_p.reshape(1, 1, seq, QK_ROPE)
    ckv_s = c_s.reshape(1, nb, dseq, KV_RANK)
    kr_s = kr_s.reshape(1, nb, dseq, QK_ROPE)
    return (y_prompt, y_sample, conv_p, ckv_p, kr_p, conv_s, ckv_s, kr_s)
```

```python
import functools
import math

import numpy as np
import jax
import jax.numpy as jnp
from jax import lax
from jax.experimental import pallas as pl
from jax.experimental.pallas import tpu as pltpu

D_MODEL = 2048
D_CONV = 2048
N_HEADS = 16
QK_NOPE = 128
QK_ROPE = 64
QK_DIM = QK_NOPE + QK_ROPE
V_DIM = 128
KV_RANK = 512
CHUNK = 64
ROPE_THETA = 10000.0
PEER_HEADS = 8
PEER_NKEYS = 128
PEER_TOPK = 16
PEER_DKEY = 256
PEER_EXPERTS = PEER_NKEYS * PEER_NKEYS
RMS_EPS = 1e-6
IN_SIZES = (D_CONV, D_CONV, D_CONV, N_HEADS * QK_DIM, KV_RANK, QK_ROPE, D_MODEL, D_MODEL)

BF16 = jnp.bfloat16
F32 = jnp.float32
VMEM_LIMIT_BYTES = 56 * 1024 * 1024
MASK_VALUE = -1e30
LANES = 128
NT_DIMS = (((1,), (1,)), ((), ()))

_PAIRS = tuple((p, q) for p in range(PEER_TOPK) for q in range(PEER_TOPK)
               if (p + 1) * (q + 1) <= PEER_TOPK)


def _pick(n, prefs):
    for p in prefs:
        if n % p == 0:
            return p
    return n


def _params(*sem):
    return pltpu.CompilerParams(dimension_semantics=sem, vmem_limit_bytes=VMEM_LIMIT_BYTES)


def _rms(x, g):
    ms = jnp.mean(x * x, axis=-1, keepdims=True)
    return x * lax.rsqrt(ms + RMS_EPS) * g


def _rms_kernel(x_ref, g_ref, o_ref):
    o_ref[...] = _rms(x_ref[...], g_ref[...]).astype(o_ref.dtype)


def _fill_call(kern, *, into, in_specs, args, tail=None, n_shared=1, **kwargs):
    if tail is not None:
        axis, n_real = tail
        n_in = len(in_specs)
        body = kern

        def kern(*refs):
            i = pl.program_id(axis)

            @pl.when(i < n_real)
            def _():
                body(*refs)

            @pl.when(i >= n_real)
            def _():
                for o in refs[n_in:n_in + n_shared]:
                    o[...] = jnp.zeros(o.shape, o.dtype)

    if into is None:
        return pl.pallas_call(kern, in_specs=in_specs, **kwargs)(*args)
    bufs = tuple(into) if isinstance(into, (tuple, list)) else (into,)
    assert len(bufs) == n_shared

    def filling(*refs):
        kern(*refs[n_shared:])

    return pl.pallas_call(filling, in_specs=[pl.BlockSpec(memory_space=pl.ANY)] * n_shared + list(in_specs),
                          input_output_aliases={k: k for k in range(n_shared)}, **kwargs)(*bufs, *args)


def _tail_blocks(into, end_row, total_rows, tm):
    if into is not None or end_row >= total_rows:
        return 0
    return -(-(total_rows - end_row) // tm)


def _rmsnorm(x, g, out_dtype, out_row0, out_rows, into=None):
    rows, d = x.shape
    tm = _pick(math.gcd(rows, out_row0), (1024, 640, 512, 256, 128))
    bo = out_row0 // tm
    nb = rows // tm
    n_tail = _tail_blocks(into, out_row0 + rows, out_rows, tm)
    return _fill_call(
        _rms_kernel, into=into, tail=(0, nb) if n_tail else None,
        grid=(nb + n_tail,),
        in_specs=[pl.BlockSpec((tm, d), lambda i: (jnp.minimum(i, nb - 1), 0)),
                  pl.BlockSpec((1, d), lambda i: (0, 0))],
        args=(x, g.reshape(1, d)),
        out_specs=pl.BlockSpec((tm, d), lambda i: (bo + i, 0)),
        out_shape=jax.ShapeDtypeStruct((out_rows, d), out_dtype),
        compiler_params=_params("parallel"),
        name="rmsnorm",
    )


def _mm_kernel(x_ref, w_ref, o_ref):
    o_ref[...] = jnp.dot(x_ref[...], w_ref[...], preferred_element_type=F32).astype(o_ref.dtype)


def _matmul(x, w, out_dtype, tn_prefs=(2048, 1024, 512, 256, 128), name="matmul"):
    t, k = x.shape
    n = w.shape[1]
    tm = _pick(t, (1280, 1024, 768, 512, 256, 128))
    tn = _pick(n, tn_prefs)
    return pl.pallas_call(
        _mm_kernel,
        grid=(t // tm, n // tn),
        in_specs=[pl.BlockSpec((tm, k), lambda i, j: (i, 0)),
                  pl.BlockSpec((k, tn), lambda i, j: (0, j))],
        out_specs=pl.BlockSpec((tm, tn), lambda i, j: (i, j)),
        out_shape=jax.ShapeDtypeStruct((t, n), out_dtype),
        compiler_params=_params("parallel", "arbitrary"),
        name=name,
    )(x, w)


def _out_proj_norm_kernel(m_ref, w_ref, x_ref, g_ref, h_ref, hn_ref):
    h = x_ref[...] + jnp.dot(m_ref[...], w_ref[...], preferred_element_type=F32)
    h_ref[...] = h
    hn_ref[...] = _rms(h, g_ref[...]).astype(hn_ref.dtype)


def _out_proj_norm(m, w, x_seg, g, row0, into=None):
    t, k = m.shape
    d = w.shape[1]
    rows = x_seg.shape[0]
    tm = _pick(math.gcd(rows, row0), (512, 256, 128))
    b0, nb = row0 // tm, rows // tm
    n_tail = _tail_blocks(into, row0 + rows, t, tm)
    real = lambda i: jnp.minimum(i, nb - 1)
    whole = lambda i: (0, 0)
    row = pl.BlockSpec((tm, d), lambda i: (b0 + i, 0))
    return _fill_call(
        _out_proj_norm_kernel, into=into, tail=(0, nb) if n_tail else None, n_shared=2,
        grid=(nb + n_tail,),
        in_specs=[pl.BlockSpec((tm, k), lambda i: (b0 + real(i), 0)),
                  pl.BlockSpec((k, d), whole),
                  pl.BlockSpec((tm, d), lambda i: (real(i), 0)),
                  pl.BlockSpec((1, d), whole)],
        args=(m, w, x_seg, g.reshape(1, d)),
        out_specs=[row, row],
        out_shape=[jax.ShapeDtypeStruct((t, d), F32), jax.ShapeDtypeStruct((t, d), BF16)],
        compiler_params=_params("parallel"),
        name="out_proj",
    )


def _kv_proj_t_kernel(c_ref, wk_ref, wvt_ref, k_ref, vt_ref):
    c = c_ref[...].astype(BF16)
    k = jnp.dot(c, wk_ref[...], preferred_element_type=F32)
    vt = lax.dot_general(wvt_ref[...], c, NT_DIMS, preferred_element_type=F32)
    for h in range(N_HEADS):
        k_ref[h] = k[:, h * QK_NOPE:(h + 1) * QK_NOPE].astype(k_ref.dtype)
        vt_ref[h] = vt[h * V_DIM:(h + 1) * V_DIM, :].astype(vt_ref.dtype)


def _kv_proj_t(c, wk, wvt, tk):
    r, kd = c.shape
    return pl.pallas_call(
        _kv_proj_t_kernel,
        grid=(r // tk,),
        in_specs=[pl.BlockSpec((tk, kd), lambda i: (i, 0)),
                  pl.BlockSpec((kd, N_HEADS * QK_NOPE), lambda i: (0, 0)),
                  pl.BlockSpec((N_HEADS * V_DIM, kd), lambda i: (0, 0))],
        out_specs=[pl.BlockSpec((N_HEADS, tk, QK_NOPE), lambda i: (0, i, 0)),
                   pl.BlockSpec((N_HEADS, None, V_DIM, tk), lambda i: (0, i, 0, 0))],
        out_shape=[jax.ShapeDtypeStruct((N_HEADS, r, QK_NOPE), BF16),
                   jax.ShapeDtypeStruct((N_HEADS, r // tk, V_DIM, tk), BF16)],
        compiler_params=_params("parallel"),
        name="kv_proj_t",
    )(c, wk, wvt)


def _post_small_kernel(z_ref, g_ref, cos_ref, sin_ref, c_ref, kr_ref):
    z = z_ref[...]
    c_ref[...] = _rms(z[:, :KV_RANK], g_ref[...])
    kr = z[:, KV_RANK:KV_RANK + QK_ROPE]
    kr_sw = z[:, KV_RANK + QK_ROPE:KV_RANK + 2 * QK_ROPE]
    kr_ref[...] = kr * cos_ref[...] + kr_sw * sin_ref[...]


def _post_small(z_small, g_kv, cos64, sin64, row0, rows):
    w = z_small.shape[1]
    tm = _pick(math.gcd(rows, row0), (1024, 512, 256, 128))
    b0 = row0 // tm
    return pl.pallas_call(
        _post_small_kernel,
        grid=(rows // tm,),
        in_specs=[pl.BlockSpec((tm, w), lambda i: (b0 + i, 0)),
                  pl.BlockSpec((1, KV_RANK), lambda i: (0, 0)),
                  pl.BlockSpec((tm, QK_ROPE), lambda i: (b0 + i, 0)),
                  pl.BlockSpec((tm, QK_ROPE), lambda i: (b0 + i, 0))],
        out_specs=[pl.BlockSpec((tm, KV_RANK), lambda i: (i, 0)),
                   pl.BlockSpec((tm, QK_ROPE), lambda i: (i, 0))],
        out_shape=[jax.ShapeDtypeStruct((rows, KV_RANK), F32),
                   jax.ShapeDtypeStruct((rows, QK_ROPE), F32)],
        compiler_params=_params("parallel"),
        name="latent_post",
    )(z_small, g_kv.reshape(1, KV_RANK), cos64, sin64)


def _q_prep_kernel(qn_ref, qr_ref, qsw_ref, cos_ref, sin_ref, o_ref):
    scale = QK_DIM ** -0.5 * math.log2(math.e)
    cos = jnp.tile(cos_ref[...], (1, N_HEADS))
    sin = jnp.tile(sin_ref[...], (1, N_HEADS))
    qr = (qr_ref[...].astype(F32) * cos + qsw_ref[...].astype(F32) * sin) * scale
    qn = qn_ref[...].astype(F32) * scale
    for h in range(N_HEADS):
        o_ref[h, :, 0:QK_NOPE] = qn[:, h * QK_NOPE:(h + 1) * QK_NOPE].astype(o_ref.dtype)
        o_ref[h, :, QK_NOPE:QK_DIM] = qr[:, h * QK_ROPE:(h + 1) * QK_ROPE].astype(o_ref.dtype)


def _q_prep(z_big, cos64, sin64, row0, rows):
    tm = _pick(math.gcd(rows, row0) if row0 else rows, (256, 128, 32))
    b0 = row0 // tm
    nope_blk = 3 * D_CONV // (N_HEADS * QK_NOPE)
    rope_blk = (3 * D_CONV + N_HEADS * QK_NOPE) // (N_HEADS * QK_ROPE)
    return pl.pallas_call(
        _q_prep_kernel,
        grid=(rows // tm,),
        in_specs=[pl.BlockSpec((tm, N_HEADS * QK_NOPE), lambda i: (b0 + i, nope_blk)),
                  pl.BlockSpec((tm, N_HEADS * QK_ROPE), lambda i: (b0 + i, rope_blk)),
                  pl.BlockSpec((tm, N_HEADS * QK_ROPE), lambda i: (b0 + i, rope_blk + 1)),
                  pl.BlockSpec((tm, QK_ROPE), lambda i: (b0 + i, 0)),
                  pl.BlockSpec((tm, QK_ROPE), lambda i: (b0 + i, 0))],
        out_specs=pl.BlockSpec((N_HEADS, tm, QK_DIM), lambda i: (0, i, 0)),
        out_shape=jax.ShapeDtypeStruct((N_HEADS, rows, QK_DIM), BF16),
        compiler_params=_params("parallel"),
        name="q_prep",
    )(z_big, z_big, z_big, cos64, sin64)


def _conv_kernel(state_ref, pc_ref, px_ref, b_ref, c_ref, x_ref, w_ref, o_ref, last_ref, *, tm, halo):
    i = pl.program_id(1)
    u = c_ref[...].astype(F32) * x_ref[...].astype(F32)
    prev = pc_ref[...].astype(F32) * px_ref[...].astype(F32)
    st = state_ref[...]
    first = i == 0
    um1 = jnp.where(first, st[7:8, :], prev[halo - 1:halo, :])
    um2 = jnp.where(first, st[6:7, :], prev[halo - 2:halo - 1, :])
    row = lax.broadcasted_iota(jnp.int32, u.shape, 0)
    s1 = jnp.where(row == 0, um1, pltpu.roll(u, 1, 0))
    s2 = jnp.where(row == 0, um2, jnp.where(row == 1, um1, pltpu.roll(u, 2, 0)))
    w = w_ref[...]
    y = w[0:1, :] * s2 + w[1:2, :] * s1 + w[2:3, :] * u
    o_ref[...] = (b_ref[...].astype(F32) * y).astype(o_ref.dtype)
    last_ref[...] = u[tm - 8:tm, :]


def _conv_gate(z_big, state, conv_w, row0, nseq, seq_len, into=None):
    tm = _pick(seq_len, (512, 256, 128, 32))
    halo = 16
    nb = seq_len // tm
    b0 = row0 // tm
    h0 = row0 // halo
    per = tm // halo
    d = D_CONV

    n_tail = _tail_blocks(into, row0 + nseq * seq_len, z_big.shape[0], tm)
    assert n_tail == 0 or nseq == 1
    real = lambda i: jnp.minimum(i, nb - 1)

    def hmap(col):
        return lambda s, i: (jnp.maximum(h0 + (s * nb + real(i)) * per - 1, 0), col)

    def bmap(col, clamp=True):
        return lambda s, i: (b0 + s * nb + (real(i) if clamp else i), col)

    return _fill_call(
        functools.partial(_conv_kernel, tm=tm, halo=halo), into=into, tail=(1, nb) if n_tail else None,
        grid=(nseq, nb + n_tail),
        in_specs=[pl.BlockSpec((None, 8, d), lambda s, i: (s, 0, 0)),
                  pl.BlockSpec((halo, d), hmap(1)),
                  pl.BlockSpec((halo, d), hmap(2)),
                  pl.BlockSpec((tm, d), bmap(0)),
                  pl.BlockSpec((tm, d), bmap(1)),
                  pl.BlockSpec((tm, d), bmap(2)),
                  pl.BlockSpec((8, d), lambda s, i: (0, 0))],
        args=(state, z_big, z_big, z_big, z_big, z_big, jnp.pad(conv_w, ((0, 5), (0, 0)))),
        out_specs=[pl.BlockSpec((tm, d), bmap(0, clamp=False)),
                   pl.BlockSpec((None, 8, d), lambda s, i: (s * nb + real(i), 0, 0))],
        out_shape=[jax.ShapeDtypeStruct((z_big.shape[0], d), BF16),
                   jax.ShapeDtypeStruct((nseq * nb, 8, d), F32)],
        compiler_params=_params("parallel", "arbitrary"),
        name="conv_gate",
    )


def _attn_latent_kernel(q_ref, wk_ref, wv_ref, cp_ref, krp_ref, cn_ref, krn_ref, o_ref, qa_ref, acc_ref,
                        *, past, tk):
    nh, dseq, _ = q_ref.shape
    rows = nh * dseq
    for h in range(nh):
        qh = q_ref[h]
        qa = lax.dot_general(qh[:, :QK_NOPE], wk_ref[h], NT_DIMS, preferred_element_type=F32)
        qa_ref[h * dseq:(h + 1) * dseq, 0:KV_RANK] = qa.astype(qa_ref.dtype)
        qa_ref[h * dseq:(h + 1) * dseq, KV_RANK:KV_RANK + QK_ROPE] = qh[:, QK_NOPE:QK_DIM]
    q = qa_ref[...]
    acc_ref[...] = jnp.zeros(acc_ref.shape, F32)

    def update(carry, s, c):
        m, l = carry
        m_new = jnp.maximum(m, jnp.max(s, axis=1, keepdims=True))
        p = jnp.exp2(s - m_new)
        alpha = jnp.exp2(m - m_new)
        acc_ref[...] = alpha * acc_ref[...] + jnp.dot(p.astype(BF16), c, preferred_element_type=F32)
        return m_new, alpha * l + jnp.sum(p, axis=1, keepdims=True)

    def past_step(j, carry):
        ks = pl.ds(pl.multiple_of(j * tk, tk), tk)
        c = cp_ref[ks, :].astype(BF16)
        k = jnp.concatenate([c, krp_ref[ks, :].astype(BF16)], axis=1)
        return update(carry, lax.dot_general(q, k, NT_DIMS, preferred_element_type=F32), c)

    carry = (jnp.full((rows, 1), MASK_VALUE, F32), jnp.zeros((rows, 1), F32))
    carry = lax.fori_loop(0, past // tk, past_step, carry)

    cn = cn_ref[...].astype(BF16)
    kn = jnp.concatenate([cn, krn_ref[...].astype(BF16)], axis=1)
    s = lax.dot_general(q, kn, NT_DIMS, preferred_element_type=F32)
    q_pos = past + lax.broadcasted_iota(jnp.int32, (rows, 1), 0) % dseq
    k_pos = past + lax.broadcasted_iota(jnp.int32, (1, dseq), 1)
    s = jnp.where(k_pos // CHUNK <= q_pos // CHUNK, s, MASK_VALUE)
    m, l = update(carry, s, cn)

    lat = (acc_ref[...] / l).astype(BF16)
    for h in range(nh):
        o = jnp.dot(lat[h * dseq:(h + 1) * dseq, :], wv_ref[h], preferred_element_type=F32)
        o_ref[:, h * V_DIM:(h + 1) * V_DIM] = o.astype(o_ref.dtype)


def _attention_latent(q, wk, wv, c_past, kr_past, c_new, kr_new, row0, into):
    nb, past, _ = c_past.shape
    dseq = c_new.shape[1]
    tk = _pick(past, (512, 256, 128))
    assert row0 % dseq == 0
    b0 = row0 // dseq
    per_b = lambda w: pl.BlockSpec((None,) + w, lambda b: (b, 0, 0))
    whole = lambda a: pl.BlockSpec(a.shape, lambda b: (0, 0, 0))
    return _fill_call(
        functools.partial(_attn_latent_kernel, past=past, tk=tk), into=into,
        grid=(nb,),
        in_specs=[pl.BlockSpec((N_HEADS, dseq, QK_DIM), lambda b: (0, b, 0)),
                  whole(wk), whole(wv),
                  per_b((past, KV_RANK)), per_b((past, QK_ROPE)),
                  per_b((dseq, KV_RANK)), per_b((dseq, QK_ROPE))],
        args=(q, wk, wv, c_past, kr_past, c_new, kr_new),
        out_specs=pl.BlockSpec((dseq, N_HEADS * V_DIM), lambda b: (b0 + b, 0)),
        out_shape=jax.ShapeDtypeStruct(into.shape, into.dtype),
        scratch_shapes=[pltpu.VMEM((N_HEADS * dseq, KV_RANK + QK_ROPE), BF16),
                        pltpu.VMEM((N_HEADS * dseq, KV_RANK), F32)],
        compiler_params=_params("parallel"),
        name="attention_latent",
    )


def _attn_t_kernel(q_ref, k_ref, kr_ref, vt_ref, o_ref, m_ref, l_ref, acc_ref, s0_ref, s1_ref, *, tb):
    i = pl.program_id(1)
    q = q_ref[...]
    q_pos = i * tb + lax.broadcasted_iota(jnp.int32, (1, tb), 1)
    col_end = (q_pos // CHUNK + 1) * CHUNK

    m_ref[...] = jnp.full(m_ref.shape, MASK_VALUE, F32)
    l_ref[...] = jnp.zeros(l_ref.shape, F32)
    acc_ref[...] = jnp.zeros(acc_ref.shape, F32)

    def produce(s_ref, j):
        ks = pl.ds(pl.multiple_of(j * tb, tb), tb)
        k = jnp.concatenate([k_ref[ks, :], kr_ref[ks, :]], axis=1)
        s_ref[:, :tb] = lax.dot_general(k, q, NT_DIMS, preferred_element_type=F32)

    def consume(s_ref, j, masked):
        s = s_ref[:, :tb]
        if masked:
            k_pos = j * tb + lax.broadcasted_iota(jnp.int32, (tb, 1), 0)
            s = jnp.where(k_pos < col_end, s, MASK_VALUE)
        m_old = m_ref[...]
        m_new = jnp.maximum(m_old, jnp.max(s, axis=0, keepdims=True))
        p = jnp.exp2(s - m_new)
        alpha = jnp.exp2(m_old - m_new)
        m_ref[...] = m_new
        l_ref[...] = alpha * l_ref[...] + jnp.sum(p, axis=0, keepdims=True)
        acc_ref[...] = alpha * acc_ref[...] + jnp.dot(vt_ref[j], p.astype(BF16), preferred_element_type=F32)

    produce(s0_ref, 0)

    bufs = (s0_ref, s1_ref)
    unroll = 8

    def group(g, carry):
        j0 = unroll * g
        for t in range(unroll):
            produce(bufs[(t + 1) % 2], j0 + t + 1)
            consume(bufs[t % 2], j0 + t, False)
        return carry

    lax.fori_loop(0, i // unroll, group, 0)
    j0 = i // unroll * unroll

    for rem in range(unroll):
        @pl.when(i - j0 == rem)
        def _(rem=rem):
            for t in range(rem + 1):
                if t < rem:
                    produce(bufs[(t + 1) % 2], j0 + t + 1)
                consume(bufs[t % 2], j0 + t, t == rem)

    o_ref[...] = (acc_ref[...] / l_ref[...]).T.astype(o_ref.dtype)


def _attention_t(q, k, kr, vt, seq, tk, out_rows):
    tq = tk
    assert tk % CHUNK == 0 and seq % tk == 0
    nq = seq // tq
    n_tail = _tail_blocks(None, seq, out_rows, tq)
    return _fill_call(
        functools.partial(_attn_t_kernel, tb=tk), into=None, tail=(1, nq) if n_tail else None,
        grid=(N_HEADS, nq + n_tail),
        in_specs=[pl.BlockSpec((None, tq, QK_DIM), lambda h, i: (h, jnp.minimum(i, nq - 1), 0)),
                  pl.BlockSpec((None, seq, QK_NOPE), lambda h, i: (h, 0, 0)),
                  pl.BlockSpec((seq, QK_ROPE), lambda h, i: (0, 0)),
                  pl.BlockSpec((None, seq // tk, V_DIM, tk), lambda h, i: (h, 0, 0, 0))],
        args=(q, k, kr, vt),
        out_specs=pl.BlockSpec((tq, V_DIM), lambda h, i: (i, h)),
        out_shape=jax.ShapeDtypeStruct((out_rows, N_HEADS * V_DIM), BF16),
        scratch_shapes=[pltpu.VMEM((1, tq), F32), pltpu.VMEM((1, tq), F32), pltpu.VMEM((V_DIM, tq), F32),
                        pltpu.VMEM((tk, tq + LANES), F32), pltpu.VMEM((tk, tq + LANES), F32)],
        compiler_params=_params("parallel", "arbitrary"),
        name="attention_t",
    )


def _merge_kernel(a_ref, o_ref, wa_ref, wb_ref, ga_ref, gb_ref, m_ref):
    ya = jnp.dot(a_ref[...], wa_ref[...], preferred_element_type=F32)
    yb = jnp.dot(o_ref[...], wb_ref[...], preferred_element_type=F32)
    m = jax.nn.sigmoid(ga_ref[...].astype(F32)) * ya + jax.nn.sigmoid(gb_ref[...].astype(F32)) * yb
    m_ref[...] = m.astype(m_ref.dtype)


def _merge(ya_in, o, wa, wb, z_big):
    t, k = ya_in.shape
    n = wa.shape[1]
    tm = _pick(t, (1280, 1024, 512, 256, 128))
    tn = _pick(n, (512, 256, 128))
    ga_blk = (3 * D_CONV + N_HEADS * (QK_NOPE + 2 * QK_ROPE)) // tn
    gb_blk = ga_blk + D_MODEL // tn
    return pl.pallas_call(
        _merge_kernel,
        grid=(t // tm, n // tn),
        in_specs=[pl.BlockSpec((tm, k), lambda i, j: (i, 0)),
                  pl.BlockSpec((tm, k), lambda i, j: (i, 0)),
                  pl.BlockSpec((k, tn), lambda i, j: (0, j)),
                  pl.BlockSpec((k, tn), lambda i, j: (0, j)),
                  pl.BlockSpec((tm, tn), lambda i, j: (i, ga_blk + j)),
                  pl.BlockSpec((tm, tn), lambda i, j: (i, gb_blk + j))],
        out_specs=pl.BlockSpec((tm, tn), lambda i, j: (i, j)),
        out_shape=jax.ShapeDtypeStruct((t, n), BF16),
        compiler_params=_params("parallel", "arbitrary"),
        name="merge",
    )(ya_in, o, wa, wb, z_big, z_big)


def _extract_top(work_ref, rank_ref, val_ref, idx_ref, *, track_rank):
    nk = PEER_NKEYS
    tm = work_ref.shape[1]
    neg = jnp.full((8, tm), -jnp.inf, F32)

    def round_(p, carry):
        idx_prev, pf = carry
        parts = [neg, neg, neg, neg]
        for k in range(nk):
            rows = slice(8 * k, 8 * k + 8)
            hit = idx_prev == float(k)
            w = jnp.where(hit, -jnp.inf, work_ref[rows, :])
            work_ref[rows, :] = w
            if track_rank:
                rank_ref[rows, :] = jnp.where(hit, pf - 1.0, rank_ref[rows, :])
            parts[k % 4] = jnp.maximum(parts[k % 4], w)
        m = jnp.maximum(jnp.maximum(parts[0], parts[1]), jnp.maximum(parts[2], parts[3]))
        big = jnp.full((8, tm), float(nk), F32)
        iparts = [big, big, big, big]
        for k in range(nk):
            w = work_ref[8 * k:8 * k + 8, :]
            iparts[k % 4] = jnp.minimum(iparts[k % 4], jnp.where(w == m, float(k), float(nk)))
        idx = jnp.minimum(jnp.minimum(iparts[0], iparts[1]), jnp.minimum(iparts[2], iparts[3]))
        val_ref[p] = m
        idx_ref[p] = idx
        return idx, pf + 1.0

    idx_last, _ = lax.fori_loop(0, PEER_TOPK, round_,
                                (jnp.full((8, tm), -1.0, F32), jnp.zeros((8, tm), F32)))
    if track_rank:
        for k in range(nk):
            rows = slice(8 * k, 8 * k + 8)
            rank_ref[rows, :] = jnp.where(idx_last == float(k), float(PEER_TOPK - 1), rank_ref[rows, :])


def _peer_select_kernel(qp_ref, k1_ref, k2_ref, k2h_ref, perm_ref,
                        r2_ref, e2_ref, n1_ref, w1_ref,
                        work_ref, rank_ref, a_ref, ia_ref, b_ref, ib_ref):
    nk = PEER_NKEYS
    qp = qp_ref[...]
    tm = qp.shape[0]
    s2h = lax.dot_general(k2h_ref[...], qp, NT_DIMS, preferred_element_type=F32)
    for h in range(PEER_HEADS):
        blk = s2h[h * nk:(h + 1) * nk, :]
        e2_ref[h * nk:(h + 1) * nk, :] = jnp.exp(blk - jnp.max(blk, axis=0, keepdims=True))

    work_ref[...] = lax.dot_general(k1_ref[...], qp, NT_DIMS, preferred_element_type=F32)
    _extract_top(work_ref, rank_ref, a_ref, ia_ref, track_rank=False)
    work_ref[...] = lax.dot_general(k2_ref[...], qp, NT_DIMS, preferred_element_type=F32)
    rank_ref[...] = jnp.full(rank_ref.shape, float(PEER_TOPK), F32)
    _extract_top(work_ref, rank_ref, b_ref, ib_ref, track_rank=True)
    r2_ref[...] = jnp.dot(perm_ref[...], rank_ref[...].astype(BF16), preferred_element_type=F32)

    a = [a_ref[p] for p in range(PEER_TOPK)]
    b = [b_ref[q] for q in range(PEER_TOPK)]
    cand = [a[p] + b[q] for (p, q) in _PAIRS]
    npair = len(_PAIRS)
    beaten = [jnp.zeros((8, tm), F32) for _ in range(npair)]
    for x in range(npair):
        px, qx = _PAIRS[x]
        for y in range(x + 1, npair):
            py, qy = _PAIRS[y]
            if px <= py and qx <= qy:
                beaten[y] = beaten[y] + 1.0
            else:
                gt = jnp.where(cand[y] > cand[x], 1.0, 0.0)
                beaten[x] = beaten[x] + gt
                beaten[y] = beaten[y] + (1.0 - gt)
    sel = [jnp.where(bt < float(PEER_TOPK), 1.0, 0.0) for bt in beaten]
    ea = [jnp.exp(a[p] - a[0]) for p in range(PEER_TOPK)]
    eb = [jnp.exp(b[q] - b[0]) for q in range(PEER_TOPK)]
    z = jnp.zeros((8, tm), F32)
    cnt = [jnp.zeros((8, tm), F32) for _ in range(PEER_TOPK)]
    for x, (p, q) in enumerate(_PAIRS):
        z = z + sel[x] * (ea[p] * eb[q])
        cnt[p] = cnt[p] + sel[x]
    half_inv_z = 0.5 / z
    w1 = [ea[p] * half_inv_z for p in range(PEER_TOPK)]
    ia = [ia_ref[p] for p in range(PEER_TOPK)]
    zero = jnp.zeros((8, tm), F32)
    for k in range(nk):
        n1k = zero
        w1k = zero
        for p in range(PEER_TOPK):
            hit = ia[p] == float(k)
            n1k = jnp.where(hit, cnt[p], n1k)
            w1k = jnp.where(hit, w1[p], w1k)
        n1_ref[8 * k:8 * k + 8, :] = n1k
        w1_ref[8 * k:8 * k + 8, :] = w1k


def _peer_select(qp, k1, k2, k2h, perm):
    t, d = qp.shape
    tm = _pick(t, (256, 128))
    rows = PEER_NKEYS * PEER_HEADS
    full = lambda i: (0, 0)
    out = jax.ShapeDtypeStruct((rows, t), F32)
    ospec = pl.BlockSpec((rows, tm), lambda i: (0, i))
    return pl.pallas_call(
        _peer_select_kernel,
        grid=(t // tm,),
        in_specs=[pl.BlockSpec((tm, d), lambda i: (i, 0)),
                  pl.BlockSpec((rows, d), full),
                  pl.BlockSpec((rows, d), full),
                  pl.BlockSpec((rows, d), full),
                  pl.BlockSpec((rows, rows), full)],
        out_specs=[ospec, ospec, ospec, ospec],
        out_shape=[out, out, out, out],
        scratch_shapes=[pltpu.VMEM((rows, tm), F32), pltpu.VMEM((rows, tm), F32),
                        pltpu.VMEM((PEER_TOPK, 8, tm), F32), pltpu.VMEM((PEER_TOPK, 8, tm), F32),
                        pltpu.VMEM((PEER_TOPK, 8, tm), F32), pltpu.VMEM((PEER_TOPK, 8, tm), F32)],
        compiler_params=_params("parallel"),
        name="peer_select",
    )(qp, k1, k2, k2h, perm)


def _gelu_tanh_x2(x):
    c = math.sqrt(2.0 / math.pi)
    return x + x * jnp.tanh(x * (c + (0.044715 * c) * (x * x)))


def _peer_main_kernel(hn_ref, u_ref, va_ref, vb_ref, r2_ref, e2_ref, n1_ref, w1_ref,
                      o_ref, ga_ref, gb_ref, pa_ref, pb_ref, act_ref, *, tc):
    nk = PEER_NKEYS
    te, tm = ga_ref.shape
    j = pl.program_id(1)
    last = pl.num_programs(1) - 1

    nw = 256
    d = o_ref.shape[1]

    n_il = te // nk
    jr = 64

    def gate_tile(g_ref, nref, wref, half, c, jh):
        cols = slice(c * tc, (c + 1) * tc)
        accs = [jnp.zeros((jr, tc), F32) for _ in range(n_il)]
        for h in range(PEER_HEADS):
            hr = slice(h * nk + jh * jr, h * nk + (jh + 1) * jr)
            r2 = r2_ref[hr, cols]
            e2 = e2_ref[hr, cols]
            for il in range(n_il):
                r = (half * n_il + il) * PEER_HEADS + h
                keep = r2 < nref[r:r + 1, cols]
                accs[il] = accs[il] + jnp.where(keep, e2, 0.0) * wref[r:r + 1, cols]
        for il in range(n_il):
            g_ref[il * nk + jh * jr:il * nk + (jh + 1) * jr, cols] = accs[il]

    def gate(g_ref, nref, wref, half):
        for c in range(tm // tc):
            for jh in range(nk // jr):
                gate_tile(g_ref, nref, wref, half, c, jh)

    @pl.when(j == 0)
    def _():
        o_ref[...] = jnp.zeros(o_ref.shape, o_ref.dtype)
        pb_ref[...] = jnp.zeros(pb_ref.shape, pb_ref.dtype)

    def act_piece(half, n):
        u = u_ref[half * te + n * nw:half * te + (n + 1) * nw, :]
        return _gelu_tanh_x2(lax.dot_general(hn_ref[...], u, NT_DIMS, preferred_element_type=F32))

    def out_piece(p_ref, v_ref, n):
        cols = slice(n * nw, (n + 1) * nw)
        o_ref[:, cols] += jnp.dot(p_ref[...], v_ref[:, cols], preferred_element_type=F32)

    def sub_block(half, g_ref, p_new_ref, p_old_ref, v_old_ref, next_gate):
        for n in range(te // nw):
            act_ref[:, n * nw:(n + 1) * nw] = act_piece(half, n)
        for n in range(d // nw):
            out_piece(p_old_ref, v_old_ref, n)
        gate(*next_gate)
        for il in range(n_il):
            ecols = slice(il * nk, (il + 1) * nk)
            for c in range(tm // tc):
                rows = slice(c * tc, (c + 1) * tc)
                g = g_ref[ecols, rows].T
                p_new_ref[rows, ecols] = (g * act_ref[rows, ecols]).astype(p_new_ref.dtype)

    @pl.when(j < last)
    def _():
        sub_block(0, ga_ref, pa_ref, pb_ref, vb_ref, (ga_ref, n1_ref, w1_ref, 0))
        sub_block(1, gb_ref, pb_ref, pa_ref, va_ref, (gb_ref, n1_ref, w1_ref, 1))

    @pl.when(j == last)
    def _():
        for n in range(d // nw):
            out_piece(pb_ref, vb_ref, n)


def _peer_main(hn, u, v, r2, e2, n1, w1):
    t, d = hn.shape
    e = u.shape[0]
    tm = _pick(t, (640, 512, 256, 128))
    te = 512
    tc = 128
    rows = PEER_NKEYS * PEER_HEADS
    sub = 2 * te // PEER_NKEYS * PEER_HEADS
    nj = e // (2 * te)
    cur = lambda j: jnp.minimum(j, nj - 1)
    return pl.pallas_call(
        functools.partial(_peer_main_kernel, tc=tc),
        grid=(t // tm, nj + 1),
        in_specs=[pl.BlockSpec((tm, d), lambda i, j: (i, 0)),
                  pl.BlockSpec((2 * te, d), lambda i, j: (cur(j), 0)),
                  pl.BlockSpec((te, d), lambda i, j: (2 * cur(j), 0)),
                  pl.BlockSpec((te, d), lambda i, j: (jnp.maximum(2 * j - 1, 0), 0)),
                  pl.BlockSpec((rows, tm), lambda i, j: (0, i)),
                  pl.BlockSpec((rows, tm), lambda i, j: (0, i)),
                  pl.BlockSpec((sub, tm), lambda i, j: (cur(j), i)),
                  pl.BlockSpec((sub, tm), lambda i, j: (cur(j), i))],
        out_specs=pl.BlockSpec((tm, d), lambda i, j: (i, 0)),
        out_shape=jax.ShapeDtypeStruct((t, d), F32),
        scratch_shapes=[pltpu.VMEM((te, tm), F32), pltpu.VMEM((te, tm), F32),
                        pltpu.VMEM((tm, te), BF16), pltpu.VMEM((tm, te), BF16),
                        pltpu.VMEM((tm, te), F32)],
        compiler_params=_params("parallel", "arbitrary"),
        name="peer_main",
    )(hn, u, v, v, r2, e2, n1, w1)


def _ple_final_kernel(h_ref, f_ref, p_ref, wg_ref, wp_ref, gp_ref, gf_ref, o_ref):
    h = h_ref[...] + f_ref[...]
    gate = jnp.dot(_rms(h, gp_ref[...]).astype(BF16), wg_ref[...], preferred_element_type=F32)
    pe = jnp.dot(p_ref[...].astype(BF16), wp_ref[...], preferred_element_type=F32)
    o_ref[...] = _rms(h + pe * jax.nn.sigmoid(gate), gf_ref[...])


def _ple_final(h, f, p, wg, wp, g_ple, g_final, row0):
    d = h.shape[1]
    rows, pd = p.shape
    tm = _pick(math.gcd(rows, row0), (256, 128))
    b0 = row0 // tm
    whole = lambda i: (0, 0)
    row = pl.BlockSpec((tm, d), lambda i: (b0 + i, 0))
    return pl.pallas_call(
        _ple_final_kernel,
        grid=(rows // tm,),
        in_specs=[row, row,
                  pl.BlockSpec((tm, pd), lambda i: (i, 0)),
                  pl.BlockSpec((d, d), whole),
                  pl.BlockSpec((pd, d), whole),
                  pl.BlockSpec((1, d), whole),
                  pl.BlockSpec((1, d), whole)],
        out_specs=pl.BlockSpec((tm, d), lambda i: (i, 0)),
        out_shape=jax.ShapeDtypeStruct((rows, d), F32),
        compiler_params=_params("parallel"),
        name="ple_final",
    )(h, f, p, wg, wp, g_ple.reshape(1, d), g_final.reshape(1, d))


def _swap_halves(w):
    half = w.shape[-1] // 2
    return jnp.concatenate([w[..., half:], w[..., :half]], axis=-1)


def _split_w_in(w_in):
    d = w_in.shape[0]
    splits = [int(s) for s in np.cumsum(IN_SIZES)[:-1]]
    wb, wc, wx, wq, wckv, wkr, wga, wgb = jnp.split(w_in, splits, axis=1)
    wq = wq.reshape(d, N_HEADS, QK_DIM)
    wq_n = wq[:, :, :QK_NOPE].reshape(d, N_HEADS * QK_NOPE)
    wq_r = wq[:, :, QK_NOPE:]
    w_big = jnp.concatenate(
        [wb, wc, wx, wq_n, wq_r.reshape(d, -1), _swap_halves(wq_r).reshape(d, -1), wga, wgb], axis=1)
    w_small = jnp.concatenate([wckv, wkr, _swap_halves(wkr)], axis=1)
    return w_big, w_small


def _rope_tables(pos):
    inv = 1.0 / (ROPE_THETA ** (jnp.arange(0, QK_ROPE, 2, dtype=F32) / QK_ROPE))
    ang = pos.astype(F32)[:, None] * inv[None, :]
    cos, sin = jnp.cos(ang), jnp.sin(ang)
    return jnp.concatenate([cos, cos], axis=1), jnp.concatenate([-sin, sin], axis=1)


def _key_matrices(sub_keys):
    nk, hd, half = PEER_NKEYS, PEER_HEADS, PEER_DKEY // 2
    eye = jnp.eye(hd, dtype=F32)

    def build(c, head_major):
        sel = jnp.zeros((2,), F32).at[c].set(1.0)
        m = sub_keys[c][:, None, None, None, :] * eye[None, :, :, None, None] * sel[None, None, None, :, None]
        if head_major:
            m = jnp.transpose(m, (1, 0, 2, 3, 4))
        return m.reshape(nk * hd, hd * PEER_DKEY).astype(BF16)

    r = np.arange(nk * hd)
    perm = np.zeros((nk * hd, nk * hd), np.float32)
    perm[(r % hd) * nk + r // hd, r] = 1.0
    return build(0, False), build(1, False), build(1, True), jnp.asarray(perm, BF16)


def kernel(x_prompt, x_sample, cache_conv, cache_ckv, cache_krope, p_prompt, p_sample, g_mix, w_in, conv_w, g_kv, w_kv_b, w_a_out, w_b_out, w_o, g_ffn, w_pq, sub_keys, u_tab, v_tab, g_ple, w_ple_gate, w_ple, g_final):
    assert x_prompt.shape[0] == 1 and w_in.shape[0] == 1
    seq = x_prompt.shape[1]
    nb, dseq = x_sample.shape[0], x_sample.shape[1]
    past = cache_ckv.shape[2]
    d = D_MODEL
    ns = nb * dseq
    t = seq + ns

    x_p, x_s = x_prompt.reshape(seq, d), x_sample.reshape(ns, d)
    p_p, p_s = p_prompt[0].reshape(seq, -1), p_sample[0].reshape(ns, -1)
    pos = jnp.concatenate([jnp.arange(seq), jnp.tile(past + jnp.arange(dseq), nb)])
    cos64, sin64 = _rope_tables(pos)
    w_big, w_small = _split_w_in(w_in[0].astype(BF16))

    xn = _rmsnorm(x_p, g_mix[0], BF16, 0, t)
    xn = _rmsnorm(x_s, g_mix[0], BF16, seq, t, into=xn)
    z_big = _matmul(xn, w_big, BF16, name="in_proj")
    z_small = _matmul(xn, w_small, F32, tn_prefs=(w_small.shape[1],), name="in_proj_small")
    c_p, kr_p = _post_small(z_small, g_kv[0], cos64, sin64, 0, seq)
    c_s, kr_s = _post_small(z_small, g_kv[0], cos64, sin64, seq, ns)

    zero_state = jnp.zeros((1, 8, D_CONV), F32)
    ya_in, last_p = _conv_gate(z_big, zero_state, conv_w[0], 0, 1, seq)
    state_s = jnp.pad(cache_conv[0], ((0, 0), (6, 0), (0, 0)))
    ya_in, last_s = _conv_gate(z_big, state_s, conv_w[0], seq, nb, dseq, into=ya_in)
    conv_p = last_p[-1, 6:8, :].reshape(1, 1, 2, D_CONV)
    if dseq >= 2:
        conv_s = last_s[:, 6:8, :].reshape(1, nb, 2, D_CONV)
    else:
        raise NotImplementedError("sample blocks shorter than the convolution state")

    w_kv_h = w_kv_b[0].astype(BF16).reshape(KV_RANK, N_HEADS, QK_NOPE + V_DIM)
    wk_h = jnp.transpose(w_kv_h[:, :, :QK_NOPE], (1, 0, 2))
    wv_h = jnp.transpose(w_kv_h[:, :, QK_NOPE:], (1, 0, 2))
    wk = w_kv_h[:, :, :QK_NOPE].reshape(KV_RANK, N_HEADS * QK_NOPE)
    wvt = w_kv_h[:, :, QK_NOPE:].reshape(KV_RANK, N_HEADS * V_DIM).T
    q_p = _q_prep(z_big, cos64, sin64, 0, seq)
    q_s = _q_prep(z_big, cos64, sin64, seq, ns)
    tk = _pick(seq, (512, 256, 128))
    k_p, vt_p = _kv_proj_t(c_p, wk, wvt, tk)
    o_all = _attention_t(q_p, k_p, kr_p.astype(BF16), vt_p, seq, tk, t)
    o_all = _attention_latent(q_s, wk_h, wv_h, cache_ckv[0], cache_krope[0],
                              c_s.reshape(nb, dseq, KV_RANK), kr_s.reshape(nb, dseq, QK_ROPE), seq, o_all)

    m = _merge(ya_in, o_all, w_a_out[0].astype(BF16), w_b_out[0].astype(BF16), z_big)
    w_out = w_o[0].astype(BF16)
    h1, hn = _out_proj_norm(m, w_out, x_p, g_ffn[0], 0)
    h1, hn = _out_proj_norm(m, w_out, x_s, g_ffn[0], seq, into=(h1, hn))

    qp = _matmul(hn, w_pq[0].astype(BF16), BF16, name="peer_query")
    k1, k2, k2h, perm = _key_matrices(sub_keys[0])
    r2, e2, n1, w1 = _peer_select(qp, k1, k2, k2h, perm)
    peer = _peer_main(hn, u_tab[0].astype(BF16), v_tab[0].astype(BF16), r2, e2, n1, w1)

    wg, wp = w_ple_gate[0].astype(BF16), w_ple[0].astype(BF16)
    y_prompt = _ple_final(h1, peer, p_p, wg, wp, g_ple[0], g_final, 0).reshape(1, seq, d)
    y_sample = _ple_final(h1, peer, p_s, wg, wp, g_ple[0], g_final, seq).reshape(nb, dseq, d)

    ckv_p = c_p.reshape(1, 1, seq, KV_RANK)
    kr_p = kr_p.reshape(1, 1, seq, QK_ROPE)
    ckv_s = c_s.reshape(1, nb, dseq, KV_RANK)
    kr_s = kr_s.reshape(1, nb, dseq, QK_ROPE)
    return (y_prompt, y_sample, conv_p, ckv_p, kr_p, conv_s, ckv_s, kr_s)
```

```python
import functools
import math

import numpy as np
import jax
import jax.numpy as jnp
from jax import lax
from jax.experimental import pallas as pl
from jax.experimental.pallas import tpu as pltpu

D_MODEL = 2048
D_CONV = 2048
N_HEADS = 16
QK_NOPE = 128
QK_ROPE = 64
QK_DIM = QK_NOPE + QK_ROPE
V_DIM = 128
KV_RANK = 512
CHUNK = 64
ROPE_THETA = 10000.0
PEER_HEADS = 8
PEER_NKEYS = 128
PEER_TOPK = 16
PEER_DKEY = 256
PEER_EXPERTS = PEER_NKEYS * PEER_NKEYS
RMS_EPS = 1e-6
IN_SIZES = (D_CONV, D_CONV, D_CONV, N_HEADS * QK_DIM, KV_RANK, QK_ROPE, D_MODEL, D_MODEL)

BF16 = jnp.bfloat16
F32 = jnp.float32
VMEM_LIMIT_BYTES = 56 * 1024 * 1024
MASK_VALUE = -1e30
LANES = 128
NT_DIMS = (((1,), (1,)), ((), ()))

_PAIRS = tuple((p, q) for p in range(PEER_TOPK) for q in range(PEER_TOPK)
               if (p + 1) * (q + 1) <= PEER_TOPK)


def _pick(n, prefs):
    for p in prefs:
        if n % p == 0:
            return p
    return n


def _params(*sem):
    return pltpu.CompilerParams(dimension_semantics=sem, vmem_limit_bytes=VMEM_LIMIT_BYTES)


def _rms(x, g):
    ms = jnp.mean(x * x, axis=-1, keepdims=True)
    return x * lax.rsqrt(ms + RMS_EPS) * g


def _rms_kernel(x_ref, g_ref, o_ref):
    o_ref[...] = _rms(x_ref[...], g_ref[...]).astype(o_ref.dtype)


def _fill_call(kern, *, into, in_specs, args, tail=None, n_shared=1, **kwargs):
    if tail is not None:
        axis, n_real = tail
        n_in = len(in_specs)
        body = kern

        def kern(*refs):
            i = pl.program_id(axis)

            @pl.when(i < n_real)
            def _():
                body(*refs)

            @pl.when(i >= n_real)
            def _():
                for o in refs[n_in:n_in + n_shared]:
                    o[...] = jnp.zeros(o.shape, o.dtype)

    if into is None:
        return pl.pallas_call(kern, in_specs=in_specs, **kwargs)(*args)
    bufs = tuple(into) if isinstance(into, (tuple, list)) else (into,)
    assert len(bufs) == n_shared

    def filling(*refs):
        kern(*refs[n_shared:])

    return pl.pallas_call(filling, in_specs=[pl.BlockSpec(memory_space=pl.ANY)] * n_shared + list(in_specs),
                          input_output_aliases={k: k for k in range(n_shared)}, **kwargs)(*bufs, *args)


def _tail_blocks(into, end_row, total_rows, tm):
    if into is not None or end_row >= total_rows:
        return 0
    return -(-(total_rows - end_row) // tm)


def _rmsnorm(x, g, out_dtype, out_row0, out_rows, into=None):
    rows, d = x.shape
    tm = _pick(math.gcd(rows, out_row0), (1024, 640, 512, 256, 128))
    bo = out_row0 // tm
    nb = rows // tm
    n_tail = _tail_blocks(into, out_row0 + rows, out_rows, tm)
    return _fill_call(
        _rms_kernel, into=into, tail=(0, nb) if n_tail else None,
        grid=(nb + n_tail,),
        in_specs=[pl.BlockSpec((tm, d), lambda i: (jnp.minimum(i, nb - 1), 0)),
                  pl.BlockSpec((1, d), lambda i: (0, 0))],
        args=(x, g.reshape(1, d)),
        out_specs=pl.BlockSpec((tm, d), lambda i: (bo + i, 0)),
        out_shape=jax.ShapeDtypeStruct((out_rows, d), out_dtype),
        compiler_params=_params("parallel"),
        name="rmsnorm",
    )


def _mm_kernel(x_ref, w_ref, o_ref):
    o_ref[...] = jnp.dot(x_ref[...], w_ref[...], preferred_element_type=F32).astype(o_ref.dtype)


def _matmul(x, w, out_dtype, tn_prefs=(2048, 1024, 512, 256, 128), name="matmul"):
    t, k = x.shape
    n = w.shape[1]
    tm = _pick(t, (1280, 1024, 768, 512, 256, 128))
    tn = _pick(n, tn_prefs)
    return pl.pallas_call(
        _mm_kernel,
        grid=(t // tm, n // tn),
        in_specs=[pl.BlockSpec((tm, k), lambda i, j: (i, 0)),
                  pl.BlockSpec((k, tn), lambda i, j: (0, j))],
        out_specs=pl.BlockSpec((tm, tn), lambda i, j: (i, j)),
        out_shape=jax.ShapeDtypeStruct((t, n), out_dtype),
        compiler_params=_params("parallel", "arbitrary"),
        name=name,
    )(x, w)


def _out_proj_norm_kernel(m_ref, w_ref, x_ref, g_ref, h_ref, hn_ref):
    h = x_ref[...] + jnp.dot(m_ref[...], w_ref[...], preferred_element_type=F32)
    h_ref[...] = h
    hn_ref[...] = _rms(h, g_ref[...]).astype(hn_ref.dtype)


def _out_proj_norm(m, w, x_seg, g, row0, into=None):
    t, k = m.shape
    d = w.shape[1]
    rows = x_seg.shape[0]
    tm = _pick(math.gcd(rows, row0), (512, 256, 128))
    b0, nb = row0 // tm, rows // tm
    n_tail = _tail_blocks(into, row0 + rows, t, tm)
    real = lambda i: jnp.minimum(i, nb - 1)
    whole = lambda i: (0, 0)
    row = pl.BlockSpec((tm, d), lambda i: (b0 + i, 0))
    return _fill_call(
        _out_proj_norm_kernel, into=into, tail=(0, nb) if n_tail else None, n_shared=2,
        grid=(nb + n_tail,),
        in_specs=[pl.BlockSpec((tm, k), lambda i: (b0 + real(i), 0)),
                  pl.BlockSpec((k, d), whole),
                  pl.BlockSpec((tm, d), lambda i: (real(i), 0)),
                  pl.BlockSpec((1, d), whole)],
        args=(m, w, x_seg, g.reshape(1, d)),
        out_specs=[row, row],
        out_shape=[jax.ShapeDtypeStruct((t, d), F32), jax.ShapeDtypeStruct((t, d), BF16)],
        compiler_params=_params("parallel"),
        name="out_proj",
    )


def _kv_proj_t_kernel(c_ref, wk_ref, wvt_ref, k_ref, vt_ref):
    c = c_ref[...].astype(BF16)
    k = jnp.dot(c, wk_ref[...], preferred_element_type=F32)
    vt = lax.dot_general(wvt_ref[...], c, NT_DIMS, preferred_element_type=F32)
    for h in range(N_HEADS):
        k_ref[h] = k[:, h * QK_NOPE:(h + 1) * QK_NOPE].astype(k_ref.dtype)
        vt_ref[h] = vt[h * V_DIM:(h + 1) * V_DIM, :].astype(vt_ref.dtype)


def _kv_proj_t(c, wk, wvt, tk):
    r, kd = c.shape
    return pl.pallas_call(
        _kv_proj_t_kernel,
        grid=(r // tk,),
        in_specs=[pl.BlockSpec((tk, kd), lambda i: (i, 0)),
                  pl.BlockSpec((kd, N_HEADS * QK_NOPE), lambda i: (0, 0)),
                  pl.BlockSpec((N_HEADS * V_DIM, kd), lambda i: (0, 0))],
        out_specs=[pl.BlockSpec((N_HEADS, tk, QK_NOPE), lambda i: (0, i, 0)),
                   pl.BlockSpec((N_HEADS, None, V_DIM, tk), lambda i: (0, i, 0, 0))],
        out_shape=[jax.ShapeDtypeStruct((N_HEADS, r, QK_NOPE), BF16),
                   jax.ShapeDtypeStruct((N_HEADS, r // tk, V_DIM, tk), BF16)],
        compiler_params=_params("parallel"),
        name="kv_proj_t",
    )(c, wk, wvt)


def _post_small_kernel(z_ref, g_ref, cos_ref, sin_ref, c_ref, kr_ref):
    z = z_ref[...]
    c_ref[...] = _rms(z[:, :KV_RANK], g_ref[...])
    kr = z[:, KV_RANK:KV_RANK + QK_ROPE]
    kr_sw = z[:, KV_RANK + QK_ROPE:KV_RANK + 2 * QK_ROPE]
    kr_ref[...] = kr * cos_ref[...] + kr_sw * sin_ref[...]


def _post_small(z_small, g_kv, cos64, sin64, row0, rows):
    w = z_small.shape[1]
    tm = _pick(math.gcd(rows, row0), (1024, 512, 256, 128))
    b0 = row0 // tm
    return pl.pallas_call(
        _post_small_kernel,
        grid=(rows // tm,),
        in_specs=[pl.BlockSpec((tm, w), lambda i: (b0 + i, 0)),
                  pl.BlockSpec((1, KV_RANK), lambda i: (0, 0)),
                  pl.BlockSpec((tm, QK_ROPE), lambda i: (b0 + i, 0)),
                  pl.BlockSpec((tm, QK_ROPE), lambda i: (b0 + i, 0))],
        out_specs=[pl.BlockSpec((tm, KV_RANK), lambda i: (i, 0)),
                   pl.BlockSpec((tm, QK_ROPE), lambda i: (i, 0))],
        out_shape=[jax.ShapeDtypeStruct((rows, KV_RANK), F32),
                   jax.ShapeDtypeStruct((rows, QK_ROPE), F32)],
        compiler_params=_params("parallel"),
        name="latent_post",
    )(z_small, g_kv.reshape(1, KV_RANK), cos64, sin64)


def _q_prep_kernel(qn_ref, qr_ref, qsw_ref, cos_ref, sin_ref, o_ref):
    scale = QK_DIM ** -0.5 * math.log2(math.e)
    cos = jnp.tile(cos_ref[...], (1, N_HEADS))
    sin = jnp.tile(sin_ref[...], (1, N_HEADS))
    qr = (qr_ref[...].astype(F32) * cos + qsw_ref[...].astype(F32) * sin) * scale
    qn = qn_ref[...].astype(F32) * scale
    for h in range(N_HEADS):
        o_ref[h, :, 0:QK_NOPE] = qn[:, h * QK_NOPE:(h + 1) * QK_NOPE].astype(o_ref.dtype)
        o_ref[h, :, QK_NOPE:QK_DIM] = qr[:, h * QK_ROPE:(h + 1) * QK_ROPE].astype(o_ref.dtype)


def _q_prep(z_big, cos64, sin64, row0, rows):
    tm = _pick(math.gcd(rows, row0) if row0 else rows, (256, 128, 32))
    b0 = row0 // tm
    nope_blk = 3 * D_CONV // (N_HEADS * QK_NOPE)
    rope_blk = (3 * D_CONV + N_HEADS * QK_NOPE) // (N_HEADS * QK_ROPE)
    return pl.pallas_call(
        _q_prep_kernel,
        grid=(rows // tm,),
        in_specs=[pl.BlockSpec((tm, N_HEADS * QK_NOPE), lambda i: (b0 + i, nope_blk)),
                  pl.BlockSpec((tm, N_HEADS * QK_ROPE), lambda i: (b0 + i, rope_blk)),
                  pl.BlockSpec((tm, N_HEADS * QK_ROPE), lambda i: (b0 + i, rope_blk + 1)),
                  pl.BlockSpec((tm, QK_ROPE), lambda i: (b0 + i, 0)),
                  pl.BlockSpec((tm, QK_ROPE), lambda i: (b0 + i, 0))],
        out_specs=pl.BlockSpec((N_HEADS, tm, QK_DIM), lambda i: (0, i, 0)),
        out_shape=jax.ShapeDtypeStruct((N_HEADS, rows, QK_DIM), BF16),
        compiler_params=_params("parallel"),
        name="q_prep",
    )(z_big, z_big, z_big, cos64, sin64)


def _conv_kernel(state_ref, pc_ref, px_ref, b_ref, c_ref, x_ref, w_ref, o_ref, last_ref, *, tm, halo):
    i = pl.program_id(1)
    u = c_ref[...].astype(F32) * x_ref[...].astype(F32)
    prev = pc_ref[...].astype(F32) * px_ref[...].astype(F32)
    st = state_ref[...]
    first = i == 0
    um1 = jnp.where(first, st[7:8, :], prev[halo - 1:halo, :])
    um2 = jnp.where(first, st[6:7, :], prev[halo - 2:halo - 1, :])
    row = lax.broadcasted_iota(jnp.int32, u.shape, 0)
    s1 = jnp.where(row == 0, um1, pltpu.roll(u, 1, 0))
    s2 = jnp.where(row == 0, um2, jnp.where(row == 1, um1, pltpu.roll(u, 2, 0)))
    w = w_ref[...]
    y = w[0:1, :] * s2 + w[1:2, :] * s1 + w[2:3, :] * u
    o_ref[...] = (b_ref[...].astype(F32) * y).astype(o_ref.dtype)
    last_ref[...] = u[tm - 8:tm, :]


def _conv_gate(z_big, state, conv_w, row0, nseq, seq_len, into=None):
    tm = _pick(seq_len, (512, 256, 128, 32))
    halo = 16
    nb = seq_len // tm
    b0 = row0 // tm
    h0 = row0 // halo
    per = tm // halo
    d = D_CONV

    n_tail = _tail_blocks(into, row0 + nseq * seq_len, z_big.shape[0], tm)
    assert n_tail == 0 or nseq == 1
    real = lambda i: jnp.minimum(i, nb - 1)

    def hmap(col):
        return lambda s, i: (jnp.maximum(h0 + (s * nb + real(i)) * per - 1, 0), col)

    def bmap(col, clamp=True):
        return lambda s, i: (b0 + s * nb + (real(i) if clamp else i), col)

    return _fill_call(
        functools.partial(_conv_kernel, tm=tm, halo=halo), into=into, tail=(1, nb) if n_tail else None,
        grid=(nseq, nb + n_tail),
        in_specs=[pl.BlockSpec((None, 8, d), lambda s, i: (s, 0, 0)),
                  pl.BlockSpec((halo, d), hmap(1)),
                  pl.BlockSpec((halo, d), hmap(2)),
                  pl.BlockSpec((tm, d), bmap(0)),
                  pl.BlockSpec((tm, d), bmap(1)),
                  pl.BlockSpec((tm, d), bmap(2)),
                  pl.BlockSpec((8, d), lambda s, i: (0, 0))],
        args=(state, z_big, z_big, z_big, z_big, z_big, jnp.pad(conv_w, ((0, 5), (0, 0)))),
        out_specs=[pl.BlockSpec((tm, d), bmap(0, clamp=False)),
                   pl.BlockSpec((None, 8, d), lambda s, i: (s * nb + real(i), 0, 0))],
        out_shape=[jax.ShapeDtypeStruct((z_big.shape[0], d), BF16),
                   jax.ShapeDtypeStruct((nseq * nb, 8, d), F32)],
        compiler_params=_params("parallel", "arbitrary"),
        name="conv_gate",
    )


def _attn_latent_kernel(q_ref, wk_ref, wv_ref, cp_ref, krp_ref, cn_ref, krn_ref, o_ref, qa_ref, acc_ref,
                        *, past, tk):
    nh, dseq, _ = q_ref.shape
    rows = nh * dseq
    for h in range(nh):
        qh = q_ref[h]
        qa = lax.dot_general(qh[:, :QK_NOPE], wk_ref[h], NT_DIMS, preferred_element_type=F32)
        qa_ref[h * dseq:(h + 1) * dseq, 0:KV_RANK] = qa.astype(qa_ref.dtype)
        qa_ref[h * dseq:(h + 1) * dseq, KV_RANK:KV_RANK + QK_ROPE] = qh[:, QK_NOPE:QK_DIM]
    q = qa_ref[...]
    acc_ref[...] = jnp.zeros(acc_ref.shape, F32)

    def update(carry, s, c):
        m, l = carry
        m_new = jnp.maximum(m, jnp.max(s, axis=1, keepdims=True))
        p = jnp.exp2(s - m_new)
        alpha = jnp.exp2(m - m_new)
        acc_ref[...] = alpha * acc_ref[...] + jnp.dot(p.astype(BF16), c, preferred_element_type=F32)
        return m_new, alpha * l + jnp.sum(p, axis=1, keepdims=True)

    def past_step(j, carry):
        ks = pl.ds(pl.multiple_of(j * tk, tk), tk)
        c = cp_ref[ks, :].astype(BF16)
        k = jnp.concatenate([c, krp_ref[ks, :].astype(BF16)], axis=1)
        return update(carry, lax.dot_general(q, k, NT_DIMS, preferred_element_type=F32), c)

    carry = (jnp.full((rows, 1), MASK_VALUE, F32), jnp.zeros((rows, 1), F32))
    carry = lax.fori_loop(0, past // tk, past_step, carry)

    cn = cn_ref[...].astype(BF16)
    kn = jnp.concatenate([cn, krn_ref[...].astype(BF16)], axis=1)
    s = lax.dot_general(q, kn, NT_DIMS, preferred_element_type=F32)
    q_pos = past + lax.broadcasted_iota(jnp.int32, (rows, 1), 0) % dseq
    k_pos = past + lax.broadcasted_iota(jnp.int32, (1, dseq), 1)
    s = jnp.where(k_pos // CHUNK <= q_pos // CHUNK, s, MASK_VALUE)
    m, l = update(carry, s, cn)

    lat = (acc_ref[...] / l).astype(BF16)
    for h in range(nh):
        o = jnp.dot(lat[h * dseq:(h + 1) * dseq, :], wv_ref[h], preferred_element_type=F32)
        o_ref[:, h * V_DIM:(h + 1) * V_DIM] = o.astype(o_ref.dtype)


def _attention_latent(q, wk, wv, c_past, kr_past, c_new, kr_new, row0, into):
    nb, past, _ = c_past.shape
    dseq = c_new.shape[1]
    tk = _pick(past, (512, 256, 128))
    assert row0 % dseq == 0
    b0 = row0 // dseq
    per_b = lambda w: pl.BlockSpec((None,) + w, lambda b: (b, 0, 0))
    whole = lambda a: pl.BlockSpec(a.shape, lambda b: (0, 0, 0))
    return _fill_call(
        functools.partial(_attn_latent_kernel, past=past, tk=tk), into=into,
        grid=(nb,),
        in_specs=[pl.BlockSpec((N_HEADS, dseq, QK_DIM), lambda b: (0, b, 0)),
                  whole(wk), whole(wv),
                  per_b((past, KV_RANK)), per_b((past, QK_ROPE)),
                  per_b((dseq, KV_RANK)), per_b((dseq, QK_ROPE))],
        args=(q, wk, wv, c_past, kr_past, c_new, kr_new),
        out_specs=pl.BlockSpec((dseq, N_HEADS * V_DIM), lambda b: (b0 + b, 0)),
        out_shape=jax.ShapeDtypeStruct(into.shape, into.dtype),
        scratch_shapes=[pltpu.VMEM((N_HEADS * dseq, KV_RANK + QK_ROPE), BF16),
                        pltpu.VMEM((N_HEADS * dseq, KV_RANK), F32)],
        compiler_params=_params("parallel"),
        name="attention_latent",
    )


def _attn_t_kernel(q_ref, k_ref, kr_ref, vt_ref, o_ref, m_ref, l_ref, acc_ref, s0_ref, s1_ref, *, tb):
    i = pl.program_id(1)
    q = q_ref[...]
    q_pos = i * tb + lax.broadcasted_iota(jnp.int32, (1, tb), 1)
    col_end = (q_pos // CHUNK + 1) * CHUNK

    m_ref[...] = jnp.full(m_ref.shape, MASK_VALUE, F32)
    l_ref[...] = jnp.zeros(l_ref.shape, F32)
    acc_ref[...] = jnp.zeros(acc_ref.shape, F32)

    def produce(s_ref, j):
        ks = pl.ds(pl.multiple_of(j * tb, tb), tb)
        k = jnp.concatenate([k_ref[ks, :], kr_ref[ks, :]], axis=1)
        s_ref[:, :tb] = lax.dot_general(k, q, NT_DIMS, preferred_element_type=F32)

    def consume(s_ref, j, masked):
        s = s_ref[:, :tb]
        if masked:
            k_pos = j * tb + lax.broadcasted_iota(jnp.int32, (tb, 1), 0)
            s = jnp.where(k_pos < col_end, s, MASK_VALUE)
        m_old = m_ref[...]
        m_new = jnp.maximum(m_old, jnp.max(s, axis=0, keepdims=True))
        p = jnp.exp2(s - m_new)
        alpha = jnp.exp2(m_old - m_new)
        m_ref[...] = m_new
        l_ref[...] = alpha * l_ref[...] + jnp.sum(p, axis=0, keepdims=True)
        acc_ref[...] = alpha * acc_ref[...] + jnp.dot(vt_ref[j], p.astype(BF16), preferred_element_type=F32)

    produce(s0_ref, 0)

    bufs = (s0_ref, s1_ref)
    unroll = 8

    def group(g, carry):
        j0 = unroll * g
        for t in range(unroll):
            produce(bufs[(t + 1) % 2], j0 + t + 1)
            consume(bufs[t % 2], j0 + t, False)
        return carry

    lax.fori_loop(0, i // unroll, group, 0)
    j0 = i // unroll * unroll

    for rem in range(unroll):
        @pl.when(i - j0 == rem)
        def _(rem=rem):
            for t in range(rem + 1):
                if t < rem:
                    produce(bufs[(t + 1) % 2], j0 + t + 1)
                consume(bufs[t % 2], j0 + t, t == rem)

    o_ref[...] = (acc_ref[...] / l_ref[...]).T.astype(o_ref.dtype)


def _attention_t(q, k, kr, vt, seq, tk, out_rows):
    tq = tk
    assert tk % CHUNK == 0 and seq % tk == 0
    nq = seq // tq
    n_tail = _tail_blocks(None, seq, out_rows, tq)
    return _fill_call(
        functools.partial(_attn_t_kernel, tb=tk), into=None, tail=(1, nq) if n_tail else None,
        grid=(N_HEADS, nq + n_tail),
        in_specs=[pl.BlockSpec((None, tq, QK_DIM), lambda h, i: (h, jnp.minimum(i, nq - 1), 0)),
                  pl.BlockSpec((None, seq, QK_NOPE), lambda h, i: (h, 0, 0)),
                  pl.BlockSpec((seq, QK_ROPE), lambda h, i: (0, 0)),
                  pl.BlockSpec((None, seq // tk, V_DIM, tk), lambda h, i: (h, 0, 0, 0))],
        args=(q, k, kr, vt),
        out_specs=pl.BlockSpec((tq, V_DIM), lambda h, i: (i, h)),
        out_shape=jax.ShapeDtypeStruct((out_rows, N_HEADS * V_DIM), BF16),
        scratch_shapes=[pltpu.VMEM((1, tq), F32), pltpu.VMEM((1, tq), F32), pltpu.VMEM((V_DIM, tq), F32),
                        pltpu.VMEM((tk, tq + LANES), F32), pltpu.VMEM((tk, tq + LANES), F32)],
        compiler_params=_params("parallel", "arbitrary"),
        name="attention_t",
    )


def _merge_kernel(a_ref, o_ref, wa_ref, wb_ref, ga_ref, gb_ref, m_ref):
    ya = jnp.dot(a_ref[...], wa_ref[...], preferred_element_type=F32)
    yb = jnp.dot(o_ref[...], wb_ref[...], preferred_element_type=F32)
    m = jax.nn.sigmoid(ga_ref[...].astype(F32)) * ya + jax.nn.sigmoid(gb_ref[...].astype(F32)) * yb
    m_ref[...] = m.astype(m_ref.dtype)


def _merge(ya_in, o, wa, wb, z_big):
    t, k = ya_in.shape
    n = wa.shape[1]
    tm = _pick(t, (1280, 1024, 512, 256, 128))
    tn = _pick(n, (512, 256, 128))
    ga_blk = (3 * D_CONV + N_HEADS * (QK_NOPE + 2 * QK_ROPE)) // tn
    gb_blk = ga_blk + D_MODEL // tn
    return pl.pallas_call(
        _merge_kernel,
        grid=(t // tm, n // tn),
        in_specs=[pl.BlockSpec((tm, k), lambda i, j: (i, 0)),
                  pl.BlockSpec((tm, k), lambda i, j: (i, 0)),
                  pl.BlockSpec((k, tn), lambda i, j: (0, j)),
                  pl.BlockSpec((k, tn), lambda i, j: (0, j)),
                  pl.BlockSpec((tm, tn), lambda i, j: (i, ga_blk + j)),
                  pl.BlockSpec((tm, tn), lambda i, j: (i, gb_blk + j))],
        out_specs=pl.BlockSpec((tm, tn), lambda i, j: (i, j)),
        out_shape=jax.ShapeDtypeStruct((t, n), BF16),
        compiler_params=_params("parallel", "arbitrary"),
        name="merge",
    )(ya_in, o, wa, wb, z_big, z_big)


def _extract_top(work_ref, rank_ref, val_ref, idx_ref, *, track_rank):
    nk = PEER_NKEYS
    tm = work_ref.shape[1]
    neg = jnp.full((8, tm), -jnp.inf, F32)

    def round_(p, carry):
        idx_prev, pf = carry
        parts = [neg, neg, neg, neg]
        for k in range(nk):
            rows = slice(8 * k, 8 * k + 8)
            hit = idx_prev == float(k)
            w = jnp.where(hit, -jnp.inf, work_ref[rows, :])
            work_ref[rows, :] = w
            if track_rank:
                rank_ref[rows, :] = jnp.where(hit, pf - 1.0, rank_ref[rows, :])
            parts[k % 4] = jnp.maximum(parts[k % 4], w)
        m = jnp.maximum(jnp.maximum(parts[0], parts[1]), jnp.maximum(parts[2], parts[3]))
        big = jnp.full((8, tm), float(nk), F32)
        iparts = [big, big, big, big]
        for k in range(nk):
            w = work_ref[8 * k:8 * k + 8, :]
            iparts[k % 4] = jnp.minimum(iparts[k % 4], jnp.where(w == m, float(k), float(nk)))
        idx = jnp.minimum(jnp.minimum(iparts[0], iparts[1]), jnp.minimum(iparts[2], iparts[3]))
        val_ref[p] = m
        idx_ref[p] = idx
        return idx, pf + 1.0

    idx_last, _ = lax.fori_loop(0, PEER_TOPK, round_,
                                (jnp.full((8, tm), -1.0, F32), jnp.zeros((8, tm), F32)))
    if track_rank:
        for k in range(nk):
            rows = slice(8 * k, 8 * k + 8)
            rank_ref[rows, :] = jnp.where(idx_last == float(k), float(PEER_TOPK - 1), rank_ref[rows, :])


def _peer_select_kernel(hn_ref, wq_ref, k1_ref, k2_ref, k2h_ref, perm_ref,
                        r2_ref, e2_ref, n1_ref, w1_ref,
                        work_ref, rank_ref, a_ref, ia_ref, b_ref, ib_ref):
    nk = PEER_NKEYS
    qp = jnp.dot(hn_ref[...], wq_ref[...], preferred_element_type=F32).astype(BF16)
    tm = qp.shape[0]
    s2h = lax.dot_general(k2h_ref[...], qp, NT_DIMS, preferred_element_type=F32)
    for h in range(PEER_HEADS):
        blk = s2h[h * nk:(h + 1) * nk, :]
        e2_ref[h * nk:(h + 1) * nk, :] = jnp.exp(blk - jnp.max(blk, axis=0, keepdims=True))

    work_ref[...] = lax.dot_general(k1_ref[...], qp, NT_DIMS, preferred_element_type=F32)
    _extract_top(work_ref, rank_ref, a_ref, ia_ref, track_rank=False)
    work_ref[...] = lax.dot_general(k2_ref[...], qp, NT_DIMS, preferred_element_type=F32)
    rank_ref[...] = jnp.full(rank_ref.shape, float(PEER_TOPK), F32)
    _extract_top(work_ref, rank_ref, b_ref, ib_ref, track_rank=True)
    r2_ref[...] = jnp.dot(perm_ref[...], rank_ref[...].astype(BF16), preferred_element_type=F32)

    a = [a_ref[p] for p in range(PEER_TOPK)]
    b = [b_ref[q] for q in range(PEER_TOPK)]
    cand = [a[p] + b[q] for (p, q) in _PAIRS]
    npair = len(_PAIRS)
    beaten = [jnp.zeros((8, tm), F32) for _ in range(npair)]
    for x in range(npair):
        px, qx = _PAIRS[x]
        for y in range(x + 1, npair):
            py, qy = _PAIRS[y]
            if px <= py and qx <= qy:
                beaten[y] = beaten[y] + 1.0
            else:
                gt = jnp.where(cand[y] > cand[x], 1.0, 0.0)
                beaten[x] = beaten[x] + gt
                beaten[y] = beaten[y] + (1.0 - gt)
    sel = [jnp.where(bt < float(PEER_TOPK), 1.0, 0.0) for bt in beaten]
    ea = [jnp.exp(a[p] - a[0]) for p in range(PEER_TOPK)]
    eb = [jnp.exp(b[q] - b[0]) for q in range(PEER_TOPK)]
    z = jnp.zeros((8, tm), F32)
    cnt = [jnp.zeros((8, tm), F32) for _ in range(PEER_TOPK)]
    for x, (p, q) in enumerate(_PAIRS):
        z = z + sel[x] * (ea[p] * eb[q])
        cnt[p] = cnt[p] + sel[x]
    half_inv_z = 0.5 / z
    w1 = [ea[p] * half_inv_z for p in range(PEER_TOPK)]
    ia = [ia_ref[p] for p in range(PEER_TOPK)]
    zero = jnp.zeros((8, tm), F32)
    for k in range(nk):
        n1k = zero
        w1k = zero
        for p in range(PEER_TOPK):
            hit = ia[p] == float(k)
            n1k = jnp.where(hit, cnt[p], n1k)
            w1k = jnp.where(hit, w1[p], w1k)
        n1_ref[8 * k:8 * k + 8, :] = n1k
        w1_ref[8 * k:8 * k + 8, :] = w1k


def _peer_select(hn, wq, k1, k2, k2h, perm):
    t, d = hn.shape
    tm = _pick(t, (256, 128))
    rows = PEER_NKEYS * PEER_HEADS
    out = jax.ShapeDtypeStruct((rows, t), F32)
    ospec = pl.BlockSpec((rows, tm), lambda i: (0, i))

    def const(shape):
        return pl.BlockSpec(shape, lambda i: (0, 0), pipeline_mode=pl.Buffered(1))

    return pl.pallas_call(
        _peer_select_kernel,
        grid=(t // tm,),
        in_specs=[pl.BlockSpec((tm, d), lambda i: (i, 0)),
                  const(wq.shape),
                  const((rows, wq.shape[1])), const((rows, wq.shape[1])), const((rows, wq.shape[1])),
                  const((rows, rows))],
        out_specs=[ospec, ospec, ospec, ospec],
        out_shape=[out, out, out, out],
        scratch_shapes=[pltpu.VMEM((rows, tm), F32), pltpu.VMEM((rows, tm), F32),
                        pltpu.VMEM((PEER_TOPK, 8, tm), F32), pltpu.VMEM((PEER_TOPK, 8, tm), F32),
                        pltpu.VMEM((PEER_TOPK, 8, tm), F32), pltpu.VMEM((PEER_TOPK, 8, tm), F32)],
        compiler_params=_params("parallel"),
        name="peer_select",
    )(hn, wq, k1, k2, k2h, perm)


def _gelu_tanh_x2(x):
    c = math.sqrt(2.0 / math.pi)
    return x + x * jnp.tanh(x * (c + (0.044715 * c) * (x * x)))


def _peer_main_kernel(hn_ref, u_ref, va_ref, vb_ref, r2_ref, e2_ref, n1_ref, w1_ref,
                      o_ref, ga_ref, gb_ref, pa_ref, pb_ref, act_ref, *, tc):
    nk = PEER_NKEYS
    te, tm = ga_ref.shape
    j = pl.program_id(1)
    last = pl.num_programs(1) - 1

    nw = 256
    d = o_ref.shape[1]

    n_il = te // nk
    jr = 64

    def gate_tile(g_ref, nref, wref, half, c, jh):
        cols = slice(c * tc, (c + 1) * tc)
        accs = [jnp.zeros((jr, tc), F32) for _ in range(n_il)]
        for h in range(PEER_HEADS):
            hr = slice(h * nk + jh * jr, h * nk + (jh + 1) * jr)
            r2 = r2_ref[hr, cols]
            e2 = e2_ref[hr, cols]
            for il in range(n_il):
                r = (half * n_il + il) * PEER_HEADS + h
                keep = r2 < nref[r:r + 1, cols]
                accs[il] = accs[il] + jnp.where(keep, e2, 0.0) * wref[r:r + 1, cols]
        for il in range(n_il):
            g_ref[il * nk + jh * jr:il * nk + (jh + 1) * jr, cols] = accs[il]

    def gate(g_ref, nref, wref, half):
        for c in range(tm // tc):
            for jh in range(nk // jr):
                gate_tile(g_ref, nref, wref, half, c, jh)

    @pl.when(j == 0)
    def _():
        o_ref[...] = jnp.zeros(o_ref.shape, o_ref.dtype)
        pb_ref[...] = jnp.zeros(pb_ref.shape, pb_ref.dtype)

    def act_piece(half, n):
        u = u_ref[half * te + n * nw:half * te + (n + 1) * nw, :]
        return _gelu_tanh_x2(lax.dot_general(hn_ref[...], u, NT_DIMS, preferred_element_type=F32))

    def out_piece(p_ref, v_ref, n):
        cols = slice(n * nw, (n + 1) * nw)
        o_ref[:, cols] += jnp.dot(p_ref[...], v_ref[:, cols], preferred_element_type=F32)

    def sub_block(half, g_ref, p_new_ref, p_old_ref, v_old_ref, next_gate):
        for n in range(te // nw):
            act_ref[:, n * nw:(n + 1) * nw] = act_piece(half, n)
        for n in range(d // nw):
            out_piece(p_old_ref, v_old_ref, n)
        gate(*next_gate)
        for il in range(n_il):
            ecols = slice(il * nk, (il + 1) * nk)
            for c in range(tm // tc):
                rows = slice(c * tc, (c + 1) * tc)
                g = g_ref[ecols, rows].T
                p_new_ref[rows, ecols] = (g * act_ref[rows, ecols]).astype(p_new_ref.dtype)

    @pl.when(j < last)
    def _():
        sub_block(0, ga_ref, pa_ref, pb_ref, vb_ref, (ga_ref, n1_ref, w1_ref, 0))
        sub_block(1, gb_ref, pb_ref, pa_ref, va_ref, (gb_ref, n1_ref, w1_ref, 1))

    @pl.when(j == last)
    def _():
        for n in range(d // nw):
            out_piece(pb_ref, vb_ref, n)


def _peer_main(hn, u, v, r2, e2, n1, w1):
    t, d = hn.shape
    e = u.shape[0]
    tm = _pick(t, (640, 512, 256, 128))
    te = 512
    tc = 128
    rows = PEER_NKEYS * PEER_HEADS
    sub = 2 * te // PEER_NKEYS * PEER_HEADS
    nj = e // (2 * te)
    cur = lambda j: jnp.minimum(j, nj - 1)
    return pl.pallas_call(
        functools.partial(_peer_main_kernel, tc=tc),
        grid=(t // tm, nj + 1),
        in_specs=[pl.BlockSpec((tm, d), lambda i, j: (i, 0)),
                  pl.BlockSpec((2 * te, d), lambda i, j: (cur(j), 0)),
                  pl.BlockSpec((te, d), lambda i, j: (2 * cur(j), 0)),
                  pl.BlockSpec((te, d), lambda i, j: (jnp.maximum(2 * j - 1, 0), 0)),
                  pl.BlockSpec((rows, tm), lambda i, j: (0, i)),
                  pl.BlockSpec((rows, tm), lambda i, j: (0, i)),
                  pl.BlockSpec((sub, tm), lambda i, j: (cur(j), i)),
                  pl.BlockSpec((sub, tm), lambda i, j: (cur(j), i))],
        out_specs=pl.BlockSpec((tm, d), lambda i, j: (i, 0)),
        out_shape=jax.ShapeDtypeStruct((t, d), F32),
        scratch_shapes=[pltpu.VMEM((te, tm), F32), pltpu.VMEM((te, tm), F32),
                        pltpu.VMEM((tm, te), BF16), pltpu.VMEM((tm, te), BF16),
                        pltpu.VMEM((tm, te), F32)],
        compiler_params=_params("parallel", "arbitrary"),
        name="peer_main",
    )(hn, u, v, v, r2, e2, n1, w1)


def _ple_final_kernel(h_ref, f_ref, p_ref, wg_ref, wp_ref, gp_ref, gf_ref, o_ref):
    h = h_ref[...] + f_ref[...]
    gate = jnp.dot(_rms(h, gp_ref[...]).astype(BF16), wg_ref[...], preferred_element_type=F32)
    pe = jnp.dot(p_ref[...].astype(BF16), wp_ref[...], preferred_element_type=F32)
    o_ref[...] = _rms(h + pe * jax.nn.sigmoid(gate), gf_ref[...])


def _ple_final(h, f, p, wg, wp, g_ple, g_final, row0):
    d = h.shape[1]
    rows, pd = p.shape
    tm = _pick(math.gcd(rows, row0), (256, 128))
    b0 = row0 // tm
    whole = lambda i: (0, 0)
    row = pl.BlockSpec((tm, d), lambda i: (b0 + i, 0))
    return pl.pallas_call(
        _ple_final_kernel,
        grid=(rows // tm,),
        in_specs=[row, row,
                  pl.BlockSpec((tm, pd), lambda i: (i, 0)),
                  pl.BlockSpec((d, d), whole),
                  pl.BlockSpec((pd, d), whole),
                  pl.BlockSpec((1, d), whole),
                  pl.BlockSpec((1, d), whole)],
        out_specs=pl.BlockSpec((tm, d), lambda i: (i, 0)),
        out_shape=jax.ShapeDtypeStruct((rows, d), F32),
        compiler_params=_params("parallel"),
        name="ple_final",
    )(h, f, p, wg, wp, g_ple.reshape(1, d), g_final.reshape(1, d))


def _swap_halves(w):
    half = w.shape[-1] // 2
    return jnp.concatenate([w[..., half:], w[..., :half]], axis=-1)


def _split_w_in(w_in):
    d = w_in.shape[0]
    splits = [int(s) for s in np.cumsum(IN_SIZES)[:-1]]
    wb, wc, wx, wq, wckv, wkr, wga, wgb = jnp.split(w_in, splits, axis=1)
    wq = wq.reshape(d, N_HEADS, QK_DIM)
    wq_n = wq[:, :, :QK_NOPE].reshape(d, N_HEADS * QK_NOPE)
    wq_r = wq[:, :, QK_NOPE:]
    w_big = jnp.concatenate(
        [wb, wc, wx, wq_n, wq_r.reshape(d, -1), _swap_halves(wq_r).reshape(d, -1), wga, wgb], axis=1)
    w_small = jnp.concatenate([wckv, wkr, _swap_halves(wkr)], axis=1)
    return w_big, w_small


def _rope_tables(pos):
    inv = 1.0 / (ROPE_THETA ** (jnp.arange(0, QK_ROPE, 2, dtype=F32) / QK_ROPE))
    ang = pos.astype(F32)[:, None] * inv[None, :]
    cos, sin = jnp.cos(ang), jnp.sin(ang)
    return jnp.concatenate([cos, cos], axis=1), jnp.concatenate([-sin, sin], axis=1)


def _key_matrices(sub_keys):
    nk, hd, half = PEER_NKEYS, PEER_HEADS, PEER_DKEY // 2
    eye = jnp.eye(hd, dtype=F32)

    def build(c, head_major):
        sel = jnp.zeros((2,), F32).at[c].set(1.0)
        m = sub_keys[c][:, None, None, None, :] * eye[None, :, :, None, None] * sel[None, None, None, :, None]
        if head_major:
            m = jnp.transpose(m, (1, 0, 2, 3, 4))
        return m.reshape(nk * hd, hd * PEER_DKEY).astype(BF16)

    r = np.arange(nk * hd)
    perm = np.zeros((nk * hd, nk * hd), np.float32)
    perm[(r % hd) * nk + r // hd, r] = 1.0
    return build(0, False), build(1, False), build(1, True), jnp.asarray(perm, BF16)


def kernel(x_prompt, x_sample, cache_conv, cache_ckv, cache_krope, p_prompt, p_sample, g_mix, w_in, conv_w, g_kv, w_kv_b, w_a_out, w_b_out, w_o, g_ffn, w_pq, sub_keys, u_tab, v_tab, g_ple, w_ple_gate, w_ple, g_final):
    assert x_prompt.shape[0] == 1 and w_in.shape[0] == 1
    seq = x_prompt.shape[1]
    nb, dseq = x_sample.shape[0], x_sample.shape[1]
    past = cache_ckv.shape[2]
    d = D_MODEL
    ns = nb * dseq
    t = seq + ns

    x_p, x_s = x_prompt.reshape(seq, d), x_sample.reshape(ns, d)
    p_p, p_s = p_prompt[0].reshape(seq, -1), p_sample[0].reshape(ns, -1)
    pos = jnp.concatenate([jnp.arange(seq), jnp.tile(past + jnp.arange(dseq), nb)])
    cos64, sin64 = _rope_tables(pos)
    w_big, w_small = _split_w_in(w_in[0].astype(BF16))

    xn = _rmsnorm(x_p, g_mix[0], BF16, 0, t)
    xn = _rmsnorm(x_s, g_mix[0], BF16, seq, t, into=xn)
    z_big = _matmul(xn, w_big, BF16, name="in_proj")
    z_small = _matmul(xn, w_small, F32, tn_prefs=(w_small.shape[1],), name="in_proj_small")
    c_p, kr_p = _post_small(z_small, g_kv[0], cos64, sin64, 0, seq)
    c_s, kr_s = _post_small(z_small, g_kv[0], cos64, sin64, seq, ns)

    zero_state = jnp.zeros((1, 8, D_CONV), F32)
    ya_in, last_p = _conv_gate(z_big, zero_state, conv_w[0], 0, 1, seq)
    state_s = jnp.pad(cache_conv[0], ((0, 0), (6, 0), (0, 0)))
    ya_in, last_s = _conv_gate(z_big, state_s, conv_w[0], seq, nb, dseq, into=ya_in)
    conv_p = last_p[-1, 6:8, :].reshape(1, 1, 2, D_CONV)
    if dseq >= 2:
        conv_s = last_s[:, 6:8, :].reshape(1, nb, 2, D_CONV)
    else:
        raise NotImplementedError("sample blocks shorter than the convolution state")

    w_kv_h = w_kv_b[0].astype(BF16).reshape(KV_RANK, N_HEADS, QK_NOPE + V_DIM)
    wk_h = jnp.transpose(w_kv_h[:, :, :QK_NOPE], (1, 0, 2))
    wv_h = jnp.transpose(w_kv_h[:, :, QK_NOPE:], (1, 0, 2))
    wk = w_kv_h[:, :, :QK_NOPE].reshape(KV_RANK, N_HEADS * QK_NOPE)
    wvt = w_kv_h[:, :, QK_NOPE:].reshape(KV_RANK, N_HEADS * V_DIM).T
    q_p = _q_prep(z_big, cos64, sin64, 0, seq)
    q_s = _q_prep(z_big, cos64, sin64, seq, ns)
    tk = _pick(seq, (512, 256, 128))
    k_p, vt_p = _kv_proj_t(c_p, wk, wvt, tk)
    o_all = _attention_t(q_p, k_p, kr_p.astype(BF16), vt_p, seq, tk, t)
    o_all = _attention_latent(q_s, wk_h, wv_h, cache_ckv[0], cache_krope[0],
                              c_s.reshape(nb, dseq, KV_RANK), kr_s.reshape(nb, dseq, QK_ROPE), seq, o_all)

    m = _merge(ya_in, o_all, w_a_out[0].astype(BF16), w_b_out[0].astype(BF16), z_big)
    w_out = w_o[0].astype(BF16)
    h1, hn = _out_proj_norm(m, w_out, x_p, g_ffn[0], 0)
    h1, hn = _out_proj_norm(m, w_out, x_s, g_ffn[0], seq, into=(h1, hn))

    k1, k2, k2h, perm = _key_matrices(sub_keys[0])
    r2, e2, n1, w1 = _peer_select(hn, w_pq[0].astype(BF16), k1, k2, k2h, perm)
    peer = _peer_main(hn, u_tab[0].astype(BF16), v_tab[0].astype(BF16), r2, e2, n1, w1)

    wg, wp = w_ple_gate[0].astype(BF16), w_ple[0].astype(BF16)
    y_prompt = _ple_final(h1, peer, p_p, wg, wp, g_ple[0], g_final, 0).reshape(1, seq, d)
    y_sample = _ple_final(h1, peer, p_s, wg, wp, g_ple[0], g_final, seq).reshape(nb, dseq, d)

    ckv_p = c_p.reshape(1, 1, seq, KV_RANK)
    kr_p = kr_p.reshape(1, 1, seq, QK_ROPE)
    ckv_s = c_s.reshape(1, nb, dseq, KV_RANK)
    kr_s = kr_s.reshape(1, nb, dseq, QK_ROPE)
    return (y_prompt, y_sample, conv_p, ckv_p, kr_p, conv_s, ckv_s, kr_s)
```

```python
import functools
import math

import numpy as np
import jax
import jax.numpy as jnp
from jax import lax
from jax.experimental import pallas as pl
from jax.experimental.pallas import tpu as pltpu

D_MODEL = 2048
D_CONV = 2048
N_HEADS = 16
QK_NOPE = 128
QK_ROPE = 64
QK_DIM = QK_NOPE + QK_ROPE
V_DIM = 128
KV_RANK = 512
CHUNK = 64
ROPE_THETA = 10000.0
PEER_HEADS = 8
PEER_NKEYS = 128
PEER_TOPK = 16
PEER_DKEY = 256
PEER_EXPERTS = PEER_NKEYS * PEER_NKEYS
RMS_EPS = 1e-6
IN_SIZES = (D_CONV, D_CONV, D_CONV, N_HEADS * QK_DIM, KV_RANK, QK_ROPE, D_MODEL, D_MODEL)

BF16 = jnp.bfloat16
F32 = jnp.float32
VMEM_LIMIT_BYTES = 56 * 1024 * 1024
MASK_VALUE = -1e30
LANES = 128
NT_DIMS = (((1,), (1,)), ((), ()))

_PAIRS = tuple((p, q) for p in range(PEER_TOPK) for q in range(PEER_TOPK)
               if (p + 1) * (q + 1) <= PEER_TOPK)


def _pick(n, prefs):
    for p in prefs:
        if n % p == 0:
            return p
    return n


def _params(*sem):
    return pltpu.CompilerParams(dimension_semantics=sem, vmem_limit_bytes=VMEM_LIMIT_BYTES)


def _rms(x, g):
    ms = jnp.mean(x * x, axis=-1, keepdims=True)
    return x * lax.rsqrt(ms + RMS_EPS) * g


def _rms_kernel(x_ref, g_ref, o_ref):
    o_ref[...] = _rms(x_ref[...], g_ref[...]).astype(o_ref.dtype)


def _fill_call(kern, *, into, in_specs, args, tail=None, n_shared=1, **kwargs):
    if tail is not None:
        axis, n_real = tail
        n_in = len(in_specs)
        body = kern

        def kern(*refs):
            i = pl.program_id(axis)

            @pl.when(i < n_real)
            def _():
                body(*refs)

            @pl.when(i >= n_real)
            def _():
                for o in refs[n_in:n_in + n_shared]:
                    o[...] = jnp.zeros(o.shape, o.dtype)

    if into is None:
        return pl.pallas_call(kern, in_specs=in_specs, **kwargs)(*args)
    bufs = tuple(into) if isinstance(into, (tuple, list)) else (into,)
    assert len(bufs) == n_shared

    def filling(*refs):
        kern(*refs[n_shared:])

    return pl.pallas_call(filling, in_specs=[pl.BlockSpec(memory_space=pl.ANY)] * n_shared + list(in_specs),
                          input_output_aliases={k: k for k in range(n_shared)}, **kwargs)(*bufs, *args)


def _tail_blocks(into, end_row, total_rows, tm):
    if into is not None or end_row >= total_rows:
        return 0
    return -(-(total_rows - end_row) // tm)


def _rmsnorm(x, g, out_dtype, out_row0, out_rows, into=None):
    rows, d = x.shape
    tm = _pick(math.gcd(rows, out_row0), (1024, 640, 512, 256, 128))
    bo = out_row0 // tm
    nb = rows // tm
    n_tail = _tail_blocks(into, out_row0 + rows, out_rows, tm)
    return _fill_call(
        _rms_kernel, into=into, tail=(0, nb) if n_tail else None,
        grid=(nb + n_tail,),
        in_specs=[pl.BlockSpec((tm, d), lambda i: (jnp.minimum(i, nb - 1), 0)),
                  pl.BlockSpec((1, d), lambda i: (0, 0))],
        args=(x, g.reshape(1, d)),
        out_specs=pl.BlockSpec((tm, d), lambda i: (bo + i, 0)),
        out_shape=jax.ShapeDtypeStruct((out_rows, d), out_dtype),
        compiler_params=_params("parallel"),
        name="rmsnorm",
    )


def _mm_kernel(x_ref, w_ref, o_ref):
    o_ref[...] = jnp.dot(x_ref[...], w_ref[...], preferred_element_type=F32).astype(o_ref.dtype)


def _matmul(x, w, out_dtype, tn_prefs=(2048, 1024, 512, 256, 128), name="matmul"):
    t, k = x.shape
    n = w.shape[1]
    tm = _pick(t, (1280, 1024, 768, 512, 256, 128))
    tn = _pick(n, tn_prefs)
    return pl.pallas_call(
        _mm_kernel,
        grid=(t // tm, n // tn),
        in_specs=[pl.BlockSpec((tm, k), lambda i, j: (i, 0)),
                  pl.BlockSpec((k, tn), lambda i, j: (0, j))],
        out_specs=pl.BlockSpec((tm, tn), lambda i, j: (i, j)),
        out_shape=jax.ShapeDtypeStruct((t, n), out_dtype),
        compiler_params=_params("parallel", "arbitrary"),
        name=name,
    )(x, w)


def _out_proj_norm_kernel(m_ref, w_ref, x_ref, g_ref, h_ref, hn_ref):
    h = x_ref[...] + jnp.dot(m_ref[...], w_ref[...], preferred_element_type=F32)
    h_ref[...] = h
    hn_ref[...] = _rms(h, g_ref[...]).astype(hn_ref.dtype)


def _out_proj_norm(m, w, x_seg, g, row0, into=None):
    t, k = m.shape
    d = w.shape[1]
    rows = x_seg.shape[0]
    tm = _pick(math.gcd(rows, row0), (512, 256, 128))
    b0, nb = row0 // tm, rows // tm
    n_tail = _tail_blocks(into, row0 + rows, t, tm)
    real = lambda i: jnp.minimum(i, nb - 1)
    whole = lambda i: (0, 0)
    row = pl.BlockSpec((tm, d), lambda i: (b0 + i, 0))
    return _fill_call(
        _out_proj_norm_kernel, into=into, tail=(0, nb) if n_tail else None, n_shared=2,
        grid=(nb + n_tail,),
        in_specs=[pl.BlockSpec((tm, k), lambda i: (b0 + real(i), 0)),
                  pl.BlockSpec((k, d), whole),
                  pl.BlockSpec((tm, d), lambda i: (real(i), 0)),
                  pl.BlockSpec((1, d), whole)],
        args=(m, w, x_seg, g.reshape(1, d)),
        out_specs=[row, row],
        out_shape=[jax.ShapeDtypeStruct((t, d), F32), jax.ShapeDtypeStruct((t, d), BF16)],
        compiler_params=_params("parallel"),
        name="out_proj",
    )


def _kv_proj_t_kernel(c_ref, wk_ref, wvt_ref, k_ref, vt_ref):
    c = c_ref[...].astype(BF16)
    k = jnp.dot(c, wk_ref[...], preferred_element_type=F32)
    vt = lax.dot_general(wvt_ref[...], c, NT_DIMS, preferred_element_type=F32)
    for h in range(N_HEADS):
        k_ref[h] = k[:, h * QK_NOPE:(h + 1) * QK_NOPE].astype(k_ref.dtype)
        vt_ref[h] = vt[h * V_DIM:(h + 1) * V_DIM, :].astype(vt_ref.dtype)


def _kv_proj_t(c, wk, wvt, tk):
    r, kd = c.shape
    return pl.pallas_call(
        _kv_proj_t_kernel,
        grid=(r // tk,),
        in_specs=[pl.BlockSpec((tk, kd), lambda i: (i, 0)),
                  pl.BlockSpec((kd, N_HEADS * QK_NOPE), lambda i: (0, 0)),
                  pl.BlockSpec((N_HEADS * V_DIM, kd), lambda i: (0, 0))],
        out_specs=[pl.BlockSpec((N_HEADS, tk, QK_NOPE), lambda i: (0, i, 0)),
                   pl.BlockSpec((N_HEADS, None, V_DIM, tk), lambda i: (0, i, 0, 0))],
        out_shape=[jax.ShapeDtypeStruct((N_HEADS, r, QK_NOPE), BF16),
                   jax.ShapeDtypeStruct((N_HEADS, r // tk, V_DIM, tk), BF16)],
        compiler_params=_params("parallel"),
        name="kv_proj_t",
    )(c, wk, wvt)


def _post_small_kernel(z_ref, g_ref, cos_ref, sin_ref, c_ref, kr_ref):
    z = z_ref[...]
    c_ref[...] = _rms(z[:, :KV_RANK], g_ref[...])
    kr = z[:, KV_RANK:KV_RANK + QK_ROPE]
    kr_sw = z[:, KV_RANK + QK_ROPE:KV_RANK + 2 * QK_ROPE]
    kr_ref[...] = kr * cos_ref[...] + kr_sw * sin_ref[...]


def _post_small(z_small, g_kv, cos64, sin64, row0, rows):
    w = z_small.shape[1]
    tm = _pick(math.gcd(rows, row0), (1024, 512, 256, 128))
    b0 = row0 // tm
    return pl.pallas_call(
        _post_small_kernel,
        grid=(rows // tm,),
        in_specs=[pl.BlockSpec((tm, w), lambda i: (b0 + i, 0)),
                  pl.BlockSpec((1, KV_RANK), lambda i: (0, 0)),
                  pl.BlockSpec((tm, QK_ROPE), lambda i: (b0 + i, 0)),
                  pl.BlockSpec((tm, QK_ROPE), lambda i: (b0 + i, 0))],
        out_specs=[pl.BlockSpec((tm, KV_RANK), lambda i: (i, 0)),
                   pl.BlockSpec((tm, QK_ROPE), lambda i: (i, 0))],
        out_shape=[jax.ShapeDtypeStruct((rows, KV_RANK), F32),
                   jax.ShapeDtypeStruct((rows, QK_ROPE), F32)],
        compiler_params=_params("parallel"),
        name="latent_post",
    )(z_small, g_kv.reshape(1, KV_RANK), cos64, sin64)


def _q_prep_kernel(qn_ref, qr_ref, qsw_ref, cos_ref, sin_ref, o_ref):
    scale = QK_DIM ** -0.5 * math.log2(math.e)
    cos = jnp.tile(cos_ref[...], (1, N_HEADS))
    sin = jnp.tile(sin_ref[...], (1, N_HEADS))
    qr = (qr_ref[...].astype(F32) * cos + qsw_ref[...].astype(F32) * sin) * scale
    qn = qn_ref[...].astype(F32) * scale
    for h in range(N_HEADS):
        o_ref[h, :, 0:QK_NOPE] = qn[:, h * QK_NOPE:(h + 1) * QK_NOPE].astype(o_ref.dtype)
        o_ref[h, :, QK_NOPE:QK_DIM] = qr[:, h * QK_ROPE:(h + 1) * QK_ROPE].astype(o_ref.dtype)


def _q_prep(z_big, cos64, sin64, row0, rows):
    tm = _pick(math.gcd(rows, row0) if row0 else rows, (256, 128, 32))
    b0 = row0 // tm
    nope_blk = 3 * D_CONV // (N_HEADS * QK_NOPE)
    rope_blk = (3 * D_CONV + N_HEADS * QK_NOPE) // (N_HEADS * QK_ROPE)
    return pl.pallas_call(
        _q_prep_kernel,
        grid=(rows // tm,),
        in_specs=[pl.BlockSpec((tm, N_HEADS * QK_NOPE), lambda i: (b0 + i, nope_blk)),
                  pl.BlockSpec((tm, N_HEADS * QK_ROPE), lambda i: (b0 + i, rope_blk)),
                  pl.BlockSpec((tm, N_HEADS * QK_ROPE), lambda i: (b0 + i, rope_blk + 1)),
                  pl.BlockSpec((tm, QK_ROPE), lambda i: (b0 + i, 0)),
                  pl.BlockSpec((tm, QK_ROPE), lambda i: (b0 + i, 0))],
        out_specs=pl.BlockSpec((N_HEADS, tm, QK_DIM), lambda i: (0, i, 0)),
        out_shape=jax.ShapeDtypeStruct((N_HEADS, rows, QK_DIM), BF16),
        compiler_params=_params("parallel"),
        name="q_prep",
    )(z_big, z_big, z_big, cos64, sin64)


def _conv_kernel(state_ref, pc_ref, px_ref, b_ref, c_ref, x_ref, w_ref, o_ref, last_ref, *, tm, halo):
    i = pl.program_id(1)
    u = c_ref[...].astype(F32) * x_ref[...].astype(F32)
    prev = pc_ref[...].astype(F32) * px_ref[...].astype(F32)
    st = state_ref[...]
    first = i == 0
    um1 = jnp.where(first, st[7:8, :], prev[halo - 1:halo, :])
    um2 = jnp.where(first, st[6:7, :], prev[halo - 2:halo - 1, :])
    row = lax.broadcasted_iota(jnp.int32, u.shape, 0)
    s1 = jnp.where(row == 0, um1, pltpu.roll(u, 1, 0))
    s2 = jnp.where(row == 0, um2, jnp.where(row == 1, um1, pltpu.roll(u, 2, 0)))
    w = w_ref[...]
    y = w[0:1, :] * s2 + w[1:2, :] * s1 + w[2:3, :] * u
    o_ref[...] = (b_ref[...].astype(F32) * y).astype(o_ref.dtype)
    last_ref[...] = u[tm - 8:tm, :]


def _conv_gate(z_big, state, conv_w, row0, nseq, seq_len, into=None):
    tm = _pick(seq_len, (512, 256, 128, 32))
    halo = 16
    nb = seq_len // tm
    b0 = row0 // tm
    h0 = row0 // halo
    per = tm // halo
    d = D_CONV

    n_tail = _tail_blocks(into, row0 + nseq * seq_len, z_big.shape[0], tm)
    assert n_tail == 0 or nseq == 1
    real = lambda i: jnp.minimum(i, nb - 1)

    def hmap(col):
        return lambda s, i: (jnp.maximum(h0 + (s * nb + real(i)) * per - 1, 0), col)

    def bmap(col, clamp=True):
        return lambda s, i: (b0 + s * nb + (real(i) if clamp else i), col)

    return _fill_call(
        functools.partial(_conv_kernel, tm=tm, halo=halo), into=into, tail=(1, nb) if n_tail else None,
        grid=(nseq, nb + n_tail),
        in_specs=[pl.BlockSpec((None, 8, d), lambda s, i: (s, 0, 0)),
                  pl.BlockSpec((halo, d), hmap(1)),
                  pl.BlockSpec((halo, d), hmap(2)),
                  pl.BlockSpec((tm, d), bmap(0)),
                  pl.BlockSpec((tm, d), bmap(1)),
                  pl.BlockSpec((tm, d), bmap(2)),
                  pl.BlockSpec((8, d), lambda s, i: (0, 0))],
        args=(state, z_big, z_big, z_big, z_big, z_big, jnp.pad(conv_w, ((0, 5), (0, 0)))),
        out_specs=[pl.BlockSpec((tm, d), bmap(0, clamp=False)),
                   pl.BlockSpec((None, 8, d), lambda s, i: (s * nb + real(i), 0, 0))],
        out_shape=[jax.ShapeDtypeStruct((z_big.shape[0], d), BF16),
                   jax.ShapeDtypeStruct((nseq * nb, 8, d), F32)],
        compiler_params=_params("parallel", "arbitrary"),
        name="conv_gate",
    )


def _attn_latent_kernel(q_ref, wk_ref, wv_ref, cp_ref, krp_ref, cn_ref, krn_ref, o_ref, qa_ref, acc_ref,
                        *, past, tk):
    nh, dseq, _ = q_ref.shape
    rows = nh * dseq
    for h in range(nh):
        qh = q_ref[h]
        qa = lax.dot_general(qh[:, :QK_NOPE], wk_ref[h], NT_DIMS, preferred_element_type=F32)
        qa_ref[h * dseq:(h + 1) * dseq, 0:KV_RANK] = qa.astype(qa_ref.dtype)
        qa_ref[h * dseq:(h + 1) * dseq, KV_RANK:KV_RANK + QK_ROPE] = qh[:, QK_NOPE:QK_DIM]
    q = qa_ref[...]
    acc_ref[...] = jnp.zeros(acc_ref.shape, F32)

    def update(carry, s, c):
        m, l = carry
        m_new = jnp.maximum(m, jnp.max(s, axis=1, keepdims=True))
        p = jnp.exp2(s - m_new)
        alpha = jnp.exp2(m - m_new)
        acc_ref[...] = alpha * acc_ref[...] + jnp.dot(p.astype(BF16), c, preferred_element_type=F32)
        return m_new, alpha * l + jnp.sum(p, axis=1, keepdims=True)

    def past_step(j, carry):
        ks = pl.ds(pl.multiple_of(j * tk, tk), tk)
        c = cp_ref[ks, :].astype(BF16)
        k = jnp.concatenate([c, krp_ref[ks, :].astype(BF16)], axis=1)
        return update(carry, lax.dot_general(q, k, NT_DIMS, preferred_element_type=F32), c)

    carry = (jnp.full((rows, 1), MASK_VALUE, F32), jnp.zeros((rows, 1), F32))
    carry = lax.fori_loop(0, past // tk, past_step, carry)

    cn = cn_ref[...].astype(BF16)
    kn = jnp.concatenate([cn, krn_ref[...].astype(BF16)], axis=1)
    s = lax.dot_general(q, kn, NT_DIMS, preferred_element_type=F32)
    q_pos = past + lax.broadcasted_iota(jnp.int32, (rows, 1), 0) % dseq
    k_pos = past + lax.broadcasted_iota(jnp.int32, (1, dseq), 1)
    s = jnp.where(k_pos // CHUNK <= q_pos // CHUNK, s, MASK_VALUE)
    m, l = update(carry, s, cn)

    lat = (acc_ref[...] / l).astype(BF16)
    for h in range(nh):
        o = jnp.dot(lat[h * dseq:(h + 1) * dseq, :], wv_ref[h], preferred_element_type=F32)
        o_ref[:, h * V_DIM:(h + 1) * V_DIM] = o.astype(o_ref.dtype)


def _attention_latent(q, wk, wv, c_past, kr_past, c_new, kr_new, row0, into):
    nb, past, _ = c_past.shape
    dseq = c_new.shape[1]
    tk = _pick(past, (512, 256, 128))
    assert row0 % dseq == 0
    b0 = row0 // dseq
    per_b = lambda w: pl.BlockSpec((None,) + w, lambda b: (b, 0, 0))
    whole = lambda a: pl.BlockSpec(a.shape, lambda b: (0, 0, 0))
    return _fill_call(
        functools.partial(_attn_latent_kernel, past=past, tk=tk), into=into,
        grid=(nb,),
        in_specs=[pl.BlockSpec((N_HEADS, dseq, QK_DIM), lambda b: (0, b, 0)),
                  whole(wk), whole(wv),
                  per_b((past, KV_RANK)), per_b((past, QK_ROPE)),
                  per_b((dseq, KV_RANK)), per_b((dseq, QK_ROPE))],
        args=(q, wk, wv, c_past, kr_past, c_new, kr_new),
        out_specs=pl.BlockSpec((dseq, N_HEADS * V_DIM), lambda b: (b0 + b, 0)),
        out_shape=jax.ShapeDtypeStruct(into.shape, into.dtype),
        scratch_shapes=[pltpu.VMEM((N_HEADS * dseq, KV_RANK + QK_ROPE), BF16),
                        pltpu.VMEM((N_HEADS * dseq, KV_RANK), F32)],
        compiler_params=_params("parallel"),
        name="attention_latent",
    )


def _attn_t_kernel(q_ref, k_ref, kr_ref, vt_ref, o_ref, m_ref, l_ref, acc_ref, s0_ref, s1_ref, *, tb):
    i = pl.program_id(1)
    q = q_ref[...]
    q_pos = i * tb + lax.broadcasted_iota(jnp.int32, (1, tb), 1)
    col_end = (q_pos // CHUNK + 1) * CHUNK

    m_ref[...] = jnp.full(m_ref.shape, MASK_VALUE, F32)
    l_ref[...] = jnp.zeros(l_ref.shape, F32)
    acc_ref[...] = jnp.zeros(acc_ref.shape, F32)

    def produce(s_ref, j):
        ks = pl.ds(pl.multiple_of(j * tb, tb), tb)
        k = jnp.concatenate([k_ref[ks, :], kr_ref[ks, :]], axis=1)
        s_ref[:, :tb] = lax.dot_general(k, q, NT_DIMS, preferred_element_type=F32)

    def consume(s_ref, j, masked):
        s = s_ref[:, :tb]
        if masked:
            k_pos = j * tb + lax.broadcasted_iota(jnp.int32, (tb, 1), 0)
            s = jnp.where(k_pos < col_end, s, MASK_VALUE)
        m_old = m_ref[...]
        m_new = jnp.maximum(m_old, jnp.max(s, axis=0, keepdims=True))
        p = jnp.exp2(s - m_new)
        alpha = jnp.exp2(m_old - m_new)
        m_ref[...] = m_new
        l_ref[...] = alpha * l_ref[...] + jnp.sum(p, axis=0, keepdims=True)
        acc_ref[...] = alpha * acc_ref[...] + jnp.dot(vt_ref[j], p.astype(BF16), preferred_element_type=F32)

    produce(s0_ref, 0)

    bufs = (s0_ref, s1_ref)
    unroll = 8

    def group(g, carry):
        j0 = unroll * g
        for t in range(unroll):
            produce(bufs[(t + 1) % 2], j0 + t + 1)
            consume(bufs[t % 2], j0 + t, False)
        return carry

    lax.fori_loop(0, i // unroll, group, 0)
    j0 = i // unroll * unroll

    for rem in range(unroll):
        @pl.when(i - j0 == rem)
        def _(rem=rem):
            for t in range(rem + 1):
                if t < rem:
                    produce(bufs[(t + 1) % 2], j0 + t + 1)
                consume(bufs[t % 2], j0 + t, t == rem)

    o_ref[...] = (acc_ref[...] / l_ref[...]).T.astype(o_ref.dtype)


def _attention_t(q, k, kr, vt, seq, tk, out_rows):
    tq = tk
    assert tk % CHUNK == 0 and seq % tk == 0
    nq = seq // tq
    n_tail = _tail_blocks(None, seq, out_rows, tq)
    return _fill_call(
        functools.partial(_attn_t_kernel, tb=tk), into=None, tail=(1, nq) if n_tail else None,
        grid=(N_HEADS, nq + n_tail),
        in_specs=[pl.BlockSpec((None, tq, QK_DIM), lambda h, i: (h, jnp.minimum(i, nq - 1), 0)),
                  pl.BlockSpec((None, seq, QK_NOPE), lambda h, i: (h, 0, 0)),
                  pl.BlockSpec((seq, QK_ROPE), lambda h, i: (0, 0)),
                  pl.BlockSpec((None, seq // tk, V_DIM, tk), lambda h, i: (h, 0, 0, 0))],
        args=(q, k, kr, vt),
        out_specs=pl.BlockSpec((tq, V_DIM), lambda h, i: (i, h)),
        out_shape=jax.ShapeDtypeStruct((out_rows, N_HEADS * V_DIM), BF16),
        scratch_shapes=[pltpu.VMEM((1, tq), F32), pltpu.VMEM((1, tq), F32), pltpu.VMEM((V_DIM, tq), F32),
                        pltpu.VMEM((tk, tq + LANES), F32), pltpu.VMEM((tk, tq + LANES), F32)],
        compiler_params=_params("parallel", "arbitrary"),
        name="attention_t",
    )


def _merge_kernel(a_ref, o_ref, wa_ref, wb_ref, ga_ref, gb_ref, m_ref):
    ya = jnp.dot(a_ref[...], wa_ref[...], preferred_element_type=F32)
    yb = jnp.dot(o_ref[...], wb_ref[...], preferred_element_type=F32)
    m = jax.nn.sigmoid(ga_ref[...].astype(F32)) * ya + jax.nn.sigmoid(gb_ref[...].astype(F32)) * yb
    m_ref[...] = m.astype(m_ref.dtype)


def _merge(ya_in, o, wa, wb, z_big):
    t, k = ya_in.shape
    n = wa.shape[1]
    tm = _pick(t, (1280, 1024, 512, 256, 128))
    tn = _pick(n, (512, 256, 128))
    ga_blk = (3 * D_CONV + N_HEADS * (QK_NOPE + 2 * QK_ROPE)) // tn
    gb_blk = ga_blk + D_MODEL // tn
    return pl.pallas_call(
        _merge_kernel,
        grid=(t // tm, n // tn),
        in_specs=[pl.BlockSpec((tm, k), lambda i, j: (i, 0)),
                  pl.BlockSpec((tm, k), lambda i, j: (i, 0)),
                  pl.BlockSpec((k, tn), lambda i, j: (0, j)),
                  pl.BlockSpec((k, tn), lambda i, j: (0, j)),
                  pl.BlockSpec((tm, tn), lambda i, j: (i, ga_blk + j)),
                  pl.BlockSpec((tm, tn), lambda i, j: (i, gb_blk + j))],
        out_specs=pl.BlockSpec((tm, tn), lambda i, j: (i, j)),
        out_shape=jax.ShapeDtypeStruct((t, n), BF16),
        compiler_params=_params("parallel", "arbitrary"),
        name="merge",
    )(ya_in, o, wa, wb, z_big, z_big)


def _extract_top(work_ref, rank_ref, val_ref, idx_ref, *, track_rank):
    nk = PEER_NKEYS
    tm = work_ref.shape[1]
    neg = jnp.full((8, tm), -jnp.inf, F32)

    def round_(p, carry):
        idx_prev, pf = carry
        parts = [neg, neg, neg, neg]
        for k in range(nk):
            rows = slice(8 * k, 8 * k + 8)
            hit = idx_prev == float(k)
            w = jnp.where(hit, -jnp.inf, work_ref[rows, :])
            work_ref[rows, :] = w
            if track_rank:
                rank_ref[rows, :] = jnp.where(hit, pf - 1.0, rank_ref[rows, :])
            parts[k % 4] = jnp.maximum(parts[k % 4], w)
        m = jnp.maximum(jnp.maximum(parts[0], parts[1]), jnp.maximum(parts[2], parts[3]))
        big = jnp.full((8, tm), float(nk), F32)
        iparts = [big, big, big, big]
        for k in range(nk):
            w = work_ref[8 * k:8 * k + 8, :]
            iparts[k % 4] = jnp.minimum(iparts[k % 4], jnp.where(w == m, float(k), float(nk)))
        idx = jnp.minimum(jnp.minimum(iparts[0], iparts[1]), jnp.minimum(iparts[2], iparts[3]))
        val_ref[p] = m
        idx_ref[p] = idx
        return idx, pf + 1.0

    idx_last, _ = lax.fori_loop(0, PEER_TOPK, round_,
                                (jnp.full((8, tm), -1.0, F32), jnp.zeros((8, tm), F32)))
    if track_rank:
        for k in range(nk):
            rows = slice(8 * k, 8 * k + 8)
            rank_ref[rows, :] = jnp.where(idx_last == float(k), float(PEER_TOPK - 1), rank_ref[rows, :])


def _peer_select_kernel(hn_ref, wq_ref, k1_ref, k2_ref, k2h_ref, perm_ref,
                        r2_ref, e2_ref, n1_ref, w1_ref,
                        work_ref, rank_ref, a_ref, ia_ref, b_ref, ib_ref):
    nk = PEER_NKEYS
    qp = jnp.dot(hn_ref[...], wq_ref[...], preferred_element_type=F32).astype(BF16)
    tm = qp.shape[0]
    s2h = lax.dot_general(k2h_ref[...], qp, NT_DIMS, preferred_element_type=F32)
    for h in range(PEER_HEADS):
        blk = s2h[h * nk:(h + 1) * nk, :]
        e2_ref[h * nk:(h + 1) * nk, :] = jnp.exp(blk - jnp.max(blk, axis=0, keepdims=True))

    work_ref[...] = lax.dot_general(k1_ref[...], qp, NT_DIMS, preferred_element_type=F32)
    _extract_top(work_ref, rank_ref, a_ref, ia_ref, track_rank=False)
    work_ref[...] = lax.dot_general(k2_ref[...], qp, NT_DIMS, preferred_element_type=F32)
    rank_ref[...] = jnp.full(rank_ref.shape, float(PEER_TOPK), F32)
    _extract_top(work_ref, rank_ref, b_ref, ib_ref, track_rank=True)
    r2_ref[...] = jnp.dot(perm_ref[...], rank_ref[...].astype(BF16), preferred_element_type=F32)

    a = [a_ref[p] for p in range(PEER_TOPK)]
    b = [b_ref[q] for q in range(PEER_TOPK)]
    cand = [a[p] + b[q] for (p, q) in _PAIRS]
    npair = len(_PAIRS)
    beaten = [jnp.zeros((8, tm), F32) for _ in range(npair)]
    for x in range(npair):
        px, qx = _PAIRS[x]
        for y in range(x + 1, npair):
            py, qy = _PAIRS[y]
            if px <= py and qx <= qy:
                beaten[y] = beaten[y] + 1.0
            else:
                gt = jnp.where(cand[y] > cand[x], 1.0, 0.0)
                beaten[x] = beaten[x] + gt
                beaten[y] = beaten[y] + (1.0 - gt)
    sel = [jnp.where(bt < float(PEER_TOPK), 1.0, 0.0) for bt in beaten]
    ea = [jnp.exp(a[p] - a[0]) for p in range(PEER_TOPK)]
    eb = [jnp.exp(b[q] - b[0]) for q in range(PEER_TOPK)]
    z = jnp.zeros((8, tm), F32)
    cnt = [jnp.zeros((8, tm), F32) for _ in range(PEER_TOPK)]
    for x, (p, q) in enumerate(_PAIRS):
        z = z + sel[x] * (ea[p] * eb[q])
        cnt[p] = cnt[p] + sel[x]
    half_inv_z = 0.5 / z
    w1 = [ea[p] * half_inv_z for p in range(PEER_TOPK)]
    ia = [ia_ref[p] for p in range(PEER_TOPK)]
    zero = jnp.zeros((8, tm), F32)
    for k in range(nk):
        n1k = zero
        w1k = zero
        for p in range(PEER_TOPK):
            hit = ia[p] == float(k)
            n1k = jnp.where(hit, cnt[p], n1k)
            w1k = jnp.where(hit, w1[p], w1k)
        n1_ref[8 * k:8 * k + 8, :] = n1k
        w1_ref[8 * k:8 * k + 8, :] = w1k


def _peer_select(hn, wq, k1, k2, k2h, perm):
    t, d = hn.shape
    tm = _pick(t, (256, 128))
    rows = PEER_NKEYS * PEER_HEADS
    out = jax.ShapeDtypeStruct((rows, t), F32)
    ospec = pl.BlockSpec((rows, tm), lambda i: (0, i))

    def const(shape):
        return pl.BlockSpec(shape, lambda i: (0, 0), pipeline_mode=pl.Buffered(1))

    return pl.pallas_call(
        _peer_select_kernel,
        grid=(t // tm,),
        in_specs=[pl.BlockSpec((tm, d), lambda i: (i, 0)),
                  const(wq.shape),
                  const((rows, wq.shape[1])), const((rows, wq.shape[1])), const((rows, wq.shape[1])),
                  const((rows, rows))],
        out_specs=[ospec, ospec, ospec, ospec],
        out_shape=[out, out, out, out],
        scratch_shapes=[pltpu.VMEM((rows, tm), F32), pltpu.VMEM((rows, tm), F32),
                        pltpu.VMEM((PEER_TOPK, 8, tm), F32), pltpu.VMEM((PEER_TOPK, 8, tm), F32),
                        pltpu.VMEM((PEER_TOPK, 8, tm), F32), pltpu.VMEM((PEER_TOPK, 8, tm), F32)],
        compiler_params=_params("parallel"),
        name="peer_select",
    )(hn, wq, k1, k2, k2h, perm)


def _gelu_tanh_x2(x):
    c = math.sqrt(2.0 / math.pi)
    return x + x * jnp.tanh(x * (c + (0.044715 * c) * (x * x)))


def _peer_main_kernel(hn_ref, u_ref, va_ref, vb_ref, r2_ref, e2_ref, n1_ref, w1_ref,
                      o_ref, ga_ref, gb_ref, pa_ref, pb_ref, act_ref, *, tc):
    nk = PEER_NKEYS
    te, tm = ga_ref.shape
    j = pl.program_id(1)
    last = pl.num_programs(1) - 1

    nw = 256
    d = o_ref.shape[1]

    n_il = te // nk
    jr = 64

    def gate_tile(g_ref, nref, wref, half, c, jh):
        cols = slice(c * tc, (c + 1) * tc)
        accs = [jnp.zeros((jr, tc), F32) for _ in range(n_il)]
        for h in range(PEER_HEADS):
            hr = slice(h * nk + jh * jr, h * nk + (jh + 1) * jr)
            r2 = r2_ref[hr, cols]
            e2 = e2_ref[hr, cols]
            for il in range(n_il):
                r = (half * n_il + il) * PEER_HEADS + h
                keep = r2 < nref[r:r + 1, cols]
                accs[il] = accs[il] + jnp.where(keep, e2, 0.0) * wref[r:r + 1, cols]
        for il in range(n_il):
            g_ref[il * nk + jh * jr:il * nk + (jh + 1) * jr, cols] = accs[il]

    def gate(g_ref, nref, wref, half):
        for c in range(tm // tc):
            for jh in range(nk // jr):
                gate_tile(g_ref, nref, wref, half, c, jh)

    @pl.when(j == 0)
    def _():
        o_ref[...] = jnp.zeros(o_ref.shape, o_ref.dtype)
        pb_ref[...] = jnp.zeros(pb_ref.shape, pb_ref.dtype)

    def act_piece(half, n):
        u = u_ref[half * te + n * nw:half * te + (n + 1) * nw, :]
        return _gelu_tanh_x2(lax.dot_general(hn_ref[...], u, NT_DIMS, preferred_element_type=F32))

    def out_piece(p_ref, v_ref, n):
        cols = slice(n * nw, (n + 1) * nw)
        o_ref[:, cols] += jnp.dot(p_ref[...], v_ref[:, cols], preferred_element_type=F32)

    def sub_block(half, g_ref, p_new_ref, p_old_ref, v_old_ref, next_gate):
        for n in range(te // nw):
            act_ref[:, n * nw:(n + 1) * nw] = act_piece(half, n)
        for n in range(d // nw):
            out_piece(p_old_ref, v_old_ref, n)
        gate(*next_gate)
        for il in range(n_il):
            ecols = slice(il * nk, (il + 1) * nk)
            for c in range(tm // tc):
                rows = slice(c * tc, (c + 1) * tc)
                g = g_ref[ecols, rows].T
                p_new_ref[rows, ecols] = (g * act_ref[rows, ecols]).astype(p_new_ref.dtype)

    @pl.when(j < last)
    def _():
        sub_block(0, ga_ref, pa_ref, pb_ref, vb_ref, (ga_ref, n1_ref, w1_ref, 0))
        sub_block(1, gb_ref, pb_ref, pa_ref, va_ref, (gb_ref, n1_ref, w1_ref, 1))

    @pl.when(j == last)
    def _():
        for n in range(d // nw):
            out_piece(pb_ref, vb_ref, n)


def _peer_main(hn, u, v, r2, e2, n1, w1):
    t, d = hn.shape
    e = u.shape[0]
    tm = _pick(t, (640, 512, 256, 128))
    te = 512
    tc = 128
    rows = PEER_NKEYS * PEER_HEADS
    sub = 2 * te // PEER_NKEYS * PEER_HEADS
    nj = e // (2 * te)
    cur = lambda j: jnp.minimum(j, nj - 1)
    return pl.pallas_call(
        functools.partial(_peer_main_kernel, tc=tc),
        grid=(t // tm, nj + 1),
        in_specs=[pl.BlockSpec((tm, d), lambda i, j: (i, 0)),
                  pl.BlockSpec((2 * te, d), lambda i, j: (cur(j), 0)),
                  pl.BlockSpec((te, d), lambda i, j: (2 * cur(j), 0)),
                  pl.BlockSpec((te, d), lambda i, j: (jnp.maximum(2 * j - 1, 0), 0)),
                  pl.BlockSpec((rows, tm), lambda i, j: (0, i)),
                  pl.BlockSpec((rows, tm), lambda i, j: (0, i)),
                  pl.BlockSpec((sub, tm), lambda i, j: (cur(j), i)),
                  pl.BlockSpec((sub, tm), lambda i, j: (cur(j), i))],
        out_specs=pl.BlockSpec((tm, d), lambda i, j: (i, 0)),
        out_shape=jax.ShapeDtypeStruct((t, d), F32),
        scratch_shapes=[pltpu.VMEM((te, tm), F32), pltpu.VMEM((te, tm), F32),
                        pltpu.VMEM((tm, te), BF16), pltpu.VMEM((tm, te), BF16),
                        pltpu.VMEM((tm, te), F32)],
        compiler_params=_params("parallel", "arbitrary"),
        name="peer_main",
    )(hn, u, v, v, r2, e2, n1, w1)


def _ple_final_kernel(h_ref, f_ref, p_ref, wg_ref, wp_ref, gp_ref, gf_ref, o_ref):
    h = h_ref[...] + f_ref[...]
    gate = jnp.dot(_rms(h, gp_ref[...]).astype(BF16), wg_ref[...], preferred_element_type=F32)
    pe = jnp.dot(p_ref[...].astype(BF16), wp_ref[...], preferred_element_type=F32)
    o_ref[...] = _rms(h + pe * jax.nn.sigmoid(gate), gf_ref[...])


def _ple_final(h, f, p, wg, wp, g_ple, g_final, row0):
    d = h.shape[1]
    rows, pd = p.shape
    tm = _pick(math.gcd(rows, row0), (512, 256, 128))
    b0 = row0 // tm
    whole = lambda i: (0, 0)
    row = pl.BlockSpec((tm, d), lambda i: (b0 + i, 0))
    return pl.pallas_call(
        _ple_final_kernel,
        grid=(rows // tm,),
        in_specs=[row, row,
                  pl.BlockSpec((tm, pd), lambda i: (i, 0)),
                  pl.BlockSpec((d, d), whole, pipeline_mode=pl.Buffered(1)),
                  pl.BlockSpec((pd, d), whole, pipeline_mode=pl.Buffered(1)),
                  pl.BlockSpec((1, d), whole),
                  pl.BlockSpec((1, d), whole)],
        out_specs=pl.BlockSpec((tm, d), lambda i: (i, 0)),
        out_shape=jax.ShapeDtypeStruct((rows, d), F32),
        compiler_params=_params("parallel"),
        name="ple_final",
    )(h, f, p, wg, wp, g_ple.reshape(1, d), g_final.reshape(1, d))


def _swap_halves(w):
    half = w.shape[-1] // 2
    return jnp.concatenate([w[..., half:], w[..., :half]], axis=-1)


def _split_w_in(w_in):
    d = w_in.shape[0]
    splits = [int(s) for s in np.cumsum(IN_SIZES)[:-1]]
    wb, wc, wx, wq, wckv, wkr, wga, wgb = jnp.split(w_in, splits, axis=1)
    wq = wq.reshape(d, N_HEADS, QK_DIM)
    wq_n = wq[:, :, :QK_NOPE].reshape(d, N_HEADS * QK_NOPE)
    wq_r = wq[:, :, QK_NOPE:]
    w_big = jnp.concatenate(
        [wb, wc, wx, wq_n, wq_r.reshape(d, -1), _swap_halves(wq_r).reshape(d, -1), wga, wgb], axis=1)
    w_small = jnp.concatenate([wckv, wkr, _swap_halves(wkr)], axis=1)
    return w_big, w_small


def _rope_tables(pos):
    inv = 1.0 / (ROPE_THETA ** (jnp.arange(0, QK_ROPE, 2, dtype=F32) / QK_ROPE))
    ang = pos.astype(F32)[:, None] * inv[None, :]
    cos, sin = jnp.cos(ang), jnp.sin(ang)
    return jnp.concatenate([cos, cos], axis=1), jnp.concatenate([-sin, sin], axis=1)


def _key_matrices(sub_keys):
    nk, hd, half = PEER_NKEYS, PEER_HEADS, PEER_DKEY // 2
    eye = jnp.eye(hd, dtype=F32)

    def build(c, head_major):
        sel = jnp.zeros((2,), F32).at[c].set(1.0)
        m = sub_keys[c][:, None, None, None, :] * eye[None, :, :, None, None] * sel[None, None, None, :, None]
        if head_major:
            m = jnp.transpose(m, (1, 0, 2, 3, 4))
        return m.reshape(nk * hd, hd * PEER_DKEY).astype(BF16)

    r = np.arange(nk * hd)
    perm = np.zeros((nk * hd, nk * hd), np.float32)
    perm[(r % hd) * nk + r // hd, r] = 1.0
    return build(0, False), build(1, False), build(1, True), jnp.asarray(perm, BF16)


def kernel(x_prompt, x_sample, cache_conv, cache_ckv, cache_krope, p_prompt, p_sample, g_mix, w_in, conv_w, g_kv, w_kv_b, w_a_out, w_b_out, w_o, g_ffn, w_pq, sub_keys, u_tab, v_tab, g_ple, w_ple_gate, w_ple, g_final):
    assert x_prompt.shape[0] == 1 and w_in.shape[0] == 1
    seq = x_prompt.shape[1]
    nb, dseq = x_sample.shape[0], x_sample.shape[1]
    past = cache_ckv.shape[2]
    d = D_MODEL
    ns = nb * dseq
    t = seq + ns

    x_p, x_s = x_prompt.reshape(seq, d), x_sample.reshape(ns, d)
    p_p, p_s = p_prompt[0].reshape(seq, -1), p_sample[0].reshape(ns, -1)
    pos = jnp.concatenate([jnp.arange(seq), jnp.tile(past + jnp.arange(dseq), nb)])
    cos64, sin64 = _rope_tables(pos)
    w_big, w_small = _split_w_in(w_in[0].astype(BF16))

    xn = _rmsnorm(x_p, g_mix[0], BF16, 0, t)
    xn = _rmsnorm(x_s, g_mix[0], BF16, seq, t, into=xn)
    z_big = _matmul(xn, w_big, BF16, name="in_proj")
    z_small = _matmul(xn, w_small, F32, tn_prefs=(w_small.shape[1],), name="in_proj_small")
    c_p, kr_p = _post_small(z_small, g_kv[0], cos64, sin64, 0, seq)
    c_s, kr_s = _post_small(z_small, g_kv[0], cos64, sin64, seq, ns)

    zero_state = jnp.zeros((1, 8, D_CONV), F32)
    ya_in, last_p = _conv_gate(z_big, zero_state, conv_w[0], 0, 1, seq)
    state_s = jnp.pad(cache_conv[0], ((0, 0), (6, 0), (0, 0)))
    ya_in, last_s = _conv_gate(z_big, state_s, conv_w[0], seq, nb, dseq, into=ya_in)
    conv_p = last_p[-1, 6:8, :].reshape(1, 1, 2, D_CONV)
    if dseq >= 2:
        conv_s = last_s[:, 6:8, :].reshape(1, nb, 2, D_CONV)
    else:
        raise NotImplementedError("sample blocks shorter than the convolution state")

    w_kv_h = w_kv_b[0].astype(BF16).reshape(KV_RANK, N_HEADS, QK_NOPE + V_DIM)
    wk_h = jnp.transpose(w_kv_h[:, :, :QK_NOPE], (1, 0, 2))
    wv_h = jnp.transpose(w_kv_h[:, :, QK_NOPE:], (1, 0, 2))
    wk = w_kv_h[:, :, :QK_NOPE].reshape(KV_RANK, N_HEADS * QK_NOPE)
    wvt = w_kv_h[:, :, QK_NOPE:].reshape(KV_RANK, N_HEADS * V_DIM).T
    q_p = _q_prep(z_big, cos64, sin64, 0, seq)
    q_s = _q_prep(z_big, cos64, sin64, seq, ns)
    tk = _pick(seq, (512, 256, 128))
    k_p, vt_p = _kv_proj_t(c_p, wk, wvt, tk)
    o_all = _attention_t(q_p, k_p, kr_p.astype(BF16), vt_p, seq, tk, t)
    o_all = _attention_latent(q_s, wk_h, wv_h, cache_ckv[0], cache_krope[0],
                              c_s.reshape(nb, dseq, KV_RANK), kr_s.reshape(nb, dseq, QK_ROPE), seq, o_all)

    m = _merge(ya_in, o_all, w_a_out[0].astype(BF16), w_b_out[0].astype(BF16), z_big)
    w_out = w_o[0].astype(BF16)
    h1, hn = _out_proj_norm(m, w_out, x_p, g_ffn[0], 0)
    h1, hn = _out_proj_norm(m, w_out, x_s, g_ffn[0], seq, into=(h1, hn))

    k1, k2, k2h, perm = _key_matrices(sub_keys[0])
    r2, e2, n1, w1 = _peer_select(hn, w_pq[0].astype(BF16), k1, k2, k2h, perm)
    peer = _peer_main(hn, u_tab[0].astype(BF16), v_tab[0].astype(BF16), r2, e2, n1, w1)

    wg, wp = w_ple_gate[0].astype(BF16), w_ple[0].astype(BF16)
    y_prompt = _ple_final(h1, peer, p_p, wg, wp, g_ple[0], g_final, 0).reshape(1, seq, d)
    y_sample = _ple_final(h1, peer, p_s, wg, wp, g_ple[0], g_final, seq).reshape(nb, dseq, d)

    ckv_p = c_p.reshape(1, 1, seq, KV_RANK)
    kr_p = kr_p.reshape(1, 1, seq, QK_ROPE)
    ckv_s = c_s.reshape(1, nb, dseq, KV_RANK)
    kr_s = kr_s.reshape(1, nb, dseq, QK_ROPE)
    return (y_prompt, y_sample, conv_p, ckv_p, kr_p, conv_s, ckv_s, kr_s)
```
